```python
import math
import jax, jax.numpy as jnp
from jax import lax
import numpy as np

D_MODEL = 1024
BATCH = 2
SEQ = 8192
DEPTH = 2

HEAD_DIM = 64
N_MLSTM = 4
N_RET = 4
N_SB = 4
N_SWA_Q = 4
N_SWA_KV = 2
N_BRANCH = 4
W_BRANCH = 4 * HEAD_DIM
W_SWA_KV = N_SWA_KV * HEAD_DIM
CHUNK = 128
BLOCK = 128
WINDOW = 128
CONV_K = 4
D_FF = 4 * D_MODEL
N_BUCKETS = 32
MAX_DIST = 128
ROPE_BASE = 10000.0
RET_DECAY_BASE = 5.0
EPS = 1e-6

MLSTM_WIDTHS = (W_BRANCH, W_BRANCH, W_BRANCH, N_MLSTM, N_MLSTM, W_BRANCH)
RET_WIDTHS = (W_BRANCH, W_BRANCH, W_BRANCH, W_BRANCH)
SB_WIDTHS = (W_BRANCH, W_BRANCH, W_BRANCH)
SWA_WIDTHS = (N_SWA_Q * HEAD_DIM, W_SWA_KV, W_SWA_KV)
GATE_WIDTH = N_BRANCH * D_MODEL
IN_WIDTHS = (sum(MLSTM_WIDTHS), sum(RET_WIDTHS), sum(SB_WIDTHS), sum(SWA_WIDTHS), GATE_WIDTH)
D_IN = sum(IN_WIDTHS)

kernel_name = 'hybrid_gated_mlstm_ret_sb_swa_trunk'


def split_cols(x, widths):
    return jnp.split(x, [int(i) for i in np.cumsum(widths)[:-1]], axis=-1)


def rms_norm(x, g):
    xf = x.astype(jnp.float32)
    y = xf * lax.rsqrt(jnp.mean(xf * xf, axis=-1, keepdims=True) + EPS)
    return (y * g.astype(jnp.float32)).astype(x.dtype)


def to_heads(t, n):
    b, s, _ = t.shape
    return t.reshape(b, s, n, HEAD_DIM).transpose(0, 2, 1, 3)


def from_heads(t):
    b, n, s, d = t.shape
    return t.transpose(0, 2, 1, 3).reshape(b, s, n * d)


def causal_dwconv(u, w, b):
    k, ch = w.shape
    out = lax.conv_general_dilated(u, w[:, None, :], window_strides=(1,), padding=[(k - 1, 0)],
                                   dimension_numbers=('NWC', 'WIO', 'NWC'), feature_group_count=ch)
    return out + b


def rotary(t, positions):
    half = t.shape[-1] // 2
    inv = ROPE_BASE ** (-jnp.arange(half, dtype=jnp.float32) / half)
    ang = positions[:, None, :, None].astype(jnp.float32) * inv
    cos, sin = jnp.cos(ang), jnp.sin(ang)
    t1, t2 = t[..., :half], t[..., half:]
    return jnp.concatenate([t1 * cos - t2 * sin, t1 * sin + t2 * cos], axis=-1)


def t5_bucket(rel):
    n = jnp.maximum(rel, 0)
    max_exact = N_BUCKETS // 2
    nf = jnp.maximum(n, 1).astype(jnp.float32)
    large = max_exact + (jnp.log(nf / max_exact) / math.log(MAX_DIST / max_exact)
                         * (N_BUCKETS - max_exact)).astype(jnp.int32)
    large = jnp.minimum(large, N_BUCKETS - 1)
    return jnp.where(n < max_exact, n, large)


def mlstm(q, k, v, i_pre, f_pre):
    b, h, s, d = q.shape
    shp = (b, h, s // CHUNK, CHUNK)
    qc = (q * d ** -0.5).reshape(shp + (d,))
    kc = k.reshape(shp + (d,))
    vc = v.reshape(shp + (d,))
    ig = i_pre.reshape(shp)
    a = jnp.cumsum(jax.nn.log_sigmoid(f_pre).reshape(shp), axis=-1)
    a_last = a[..., -1]
    causal = jnp.tril(jnp.ones((CHUNK, CHUNK), dtype=bool))
    d_intra = jnp.where(causal, a[..., :, None] - a[..., None, :] + ig[..., None, :], -jnp.inf)
    g_loc = a_last[..., None] - a + ig
    m_loc = jnp.max(g_loc, axis=-1)
    w_loc = jnp.exp(g_loc - m_loc[..., None])
    c_loc = jnp.einsum('bhnsk,bhnsv->bhnkv', kc * w_loc[..., None], vc)
    n_loc = jnp.einsum('bhns,bhnsk->bhnk', w_loc, kc)

    def step(carry, inp):
        c_st, n_st, m_st = carry
        c_l, n_l, m_l, a_l = inp
        m_new = jnp.maximum(a_l + m_st, m_l)
        s_old = jnp.exp(a_l + m_st - m_new)
        s_new = jnp.exp(m_l - m_new)
        c_new = s_old[..., None, None] * c_st + s_new[..., None, None] * c_l
        n_new = s_old[..., None] * n_st + s_new[..., None] * n_l
        return (c_new, n_new, m_new), (c_st, n_st, m_st)

    init = (jnp.zeros((b, h, d, d), jnp.float32), jnp.zeros((b, h, d), jnp.float32),
            jnp.zeros((b, h), jnp.float32))
    xs = tuple(jnp.moveaxis(t, 2, 0) for t in (c_loc, n_loc, m_loc, a_last))
    _, (c_in, n_in, m_in) = lax.scan(step, init, xs)
    c_in = jnp.moveaxis(c_in, 0, 2)
    n_in = jnp.moveaxis(n_in, 0, 2)
    m_in = jnp.moveaxis(m_in, 0, 2)
    m_inter = a + m_in[..., None]
    m_tot = jnp.maximum(m_inter, jnp.max(d_intra, axis=-1))
    s_inter = jnp.exp(m_inter - m_tot)
    p = jnp.exp(d_intra - m_tot[..., None]) * jnp.einsum('bhnld,bhnsd->bhnls', qc, kc)
    num = (s_inter[..., None] * jnp.einsum('bhnlk,bhnkv->bhnlv', qc, c_in)
           + jnp.einsum('bhnls,bhnsv->bhnlv', p, vc))
    den = s_inter * jnp.einsum('bhnlk,bhnk->bhnl', qc, n_in) + jnp.sum(p, axis=-1)
    out = num / jnp.maximum(jnp.abs(den), jnp.exp(-m_tot))[..., None]
    return out.reshape(b, h, s, d)


def retention(q, k, v, log_gamma):
    b, h, s, d = q.shape
    n_c = s // CHUNK

    def chunk(t):
        return jnp.moveaxis(t.reshape(b, h, n_c, CHUNK, d), 2, 0)

    idx = jnp.arange(CHUNK, dtype=jnp.float32)
    lg = log_gamma[:, None]
    rel = idx[:, None] - idx[None, :]
    decay_intra = jnp.where(rel >= 0, jnp.exp(jnp.maximum(rel, 0.0)[None] * lg[..., None]), 0.0)
    decay_q = jnp.exp((idx + 1.0)[None] * lg)
    decay_k = jnp.exp((CHUNK - 1.0 - idx)[None] * lg)
    decay_c = jnp.exp(CHUNK * log_gamma)

    def step(state, inp):
        qc, kc, vc = inp
        sc = jnp.einsum('bhld,bhsd->bhls', qc, kc) * decay_intra
        o = (jnp.einsum('bhls,bhsv->bhlv', sc, vc)
             + jnp.einsum('bhlk,bhkv->bhlv', qc, state) * decay_q[..., None])
        state = decay_c[:, None, None] * state + jnp.einsum('bhsk,bhsv->bhkv', kc * decay_k[..., None], vc)
        return state, o

    _, o = lax.scan(step, jnp.zeros((b, h, d, d), jnp.float32), (chunk(q), chunk(k), chunk(v)))
    return jnp.moveaxis(o, 0, 2).reshape(b, h, s, d)


def stick_breaking(q, k, v):
    b, h, s, d = q.shape
    n_b = s // BLOCK
    scale = d ** -0.5
    qb = jnp.moveaxis(q.reshape(b, h, n_b, BLOCK, d), 2, 0)
    key_pos = jnp.arange(s)

    def one_block(args):
        q_blk, start = args
        z = jnp.einsum('bhtd,bhsd->bhts', q_blk, k) * scale
        before = key_pos[None, :] < (start + jnp.arange(BLOCK))[:, None]
        log_keep = jnp.where(before, jax.nn.log_sigmoid(-z), 0.0)
        between = lax.cumsum(log_keep, axis=3, reverse=True) - log_keep
        weight = jnp.where(before, jnp.exp(jax.nn.log_sigmoid(z) + between), 0.0)
        return jnp.einsum('bhts,bhsd->bhtd', weight, v)

    out = lax.map(one_block, (qb, jnp.arange(n_b) * BLOCK))
    return jnp.moveaxis(out, 0, 2).reshape(b, h, s, d)


def swa_with_sinks(q, k, v, sinks, rel_bias, positions):
    b, hq, s, d = q.shape
    g = k.shape[1]
    r = hq // g
    n_b = s // BLOCK
    qb = q.reshape(b, g, r, n_b, BLOCK, d)

    def band(t):
        tb = t.reshape(b, g, n_b, BLOCK, d)
        prev = jnp.concatenate([jnp.zeros_like(tb[:, :, :1]), tb[:, :, :-1]], axis=2)
        return jnp.concatenate([prev, tb], axis=3)

    kw, vw = band(k), band(v)
    pos_b = positions.reshape(b, n_b, BLOCK)
    pos_prev = jnp.concatenate([pos_b[:, :1] - BLOCK, pos_b[:, :-1]], axis=1)
    pos_k = jnp.concatenate([pos_prev, pos_b], axis=2)
    bucket = t5_bucket(pos_b[..., :, None] - pos_k[..., None, :])
    bias = jnp.moveaxis(rel_bias[bucket].astype(jnp.float32), -1, 1).reshape(b, g, r, n_b, BLOCK, 2 * BLOCK)
    kj = jnp.arange(2 * BLOCK)[None, :]
    dist = (jnp.arange(BLOCK)[:, None] + BLOCK) - kj
    key_abs = (jnp.arange(n_b) * BLOCK - BLOCK)[:, None, None] + kj[None]
    valid = (dist >= 0) & (dist < WINDOW) & (key_abs >= 0)
    logits = jnp.einsum('bgrnid,bgnjd->bgrnij', qb, kw) * d ** -0.5 + bias
    logits = jnp.where(valid, logits, -jnp.inf)
    sink = sinks.reshape(g, r)[None, :, :, None, None, None]
    m = jnp.maximum(jnp.max(logits, axis=-1, keepdims=True), sink)
    p = jnp.exp(logits - m)
    den = jnp.sum(p, axis=-1, keepdims=True) + jnp.exp(sink - m)
    out = jnp.einsum('bgrnij,bgnjd->bgrnid', p / den, vw)
    return out.reshape(b, hq, s, d)


def token_mixer(h, positions, w_in, conv_w, conv_b, gate_b, ret_g, q_g, k_g, sinks, rel_bias,
                w_up, w_out, log_gamma):
    f32 = jnp.float32
    b, s, _ = h.shape
    p_ml, p_ret, p_sb, p_swa, p_gate = split_cols(h @ w_in, IN_WIDTHS)

    mq, mk, mv, mi, mf, mo = split_cols(p_ml, MLSTM_WIDTHS)
    qk = jax.nn.silu(causal_dwconv(jnp.concatenate([mq, mk], axis=-1), conv_w, conv_b))
    mq, mk = jnp.split(qk, 2, axis=-1)
    i_pre = jnp.moveaxis((mi + gate_b[0]).astype(f32), -1, 1)
    f_pre = jnp.moveaxis((mf + gate_b[1]).astype(f32), -1, 1)
    h_ml = mlstm(to_heads(mq, N_MLSTM).astype(f32), to_heads(mk, N_MLSTM).astype(f32),
                 to_heads(mv, N_MLSTM).astype(f32), i_pre, f_pre)
    out_ml = from_heads(h_ml) * jax.nn.sigmoid(mo.astype(f32))

    rq, rk, rv, rg = split_cols(p_ret, RET_WIDTHS)
    rq = rotary(to_heads(rq, N_RET).astype(f32), positions)
    rk = rotary(to_heads(rk, N_RET).astype(f32), positions) * HEAD_DIM ** -0.5
    h_ret = from_heads(retention(rq, rk, to_heads(rv, N_RET).astype(f32), log_gamma))
    h_ret = rms_norm(h_ret.reshape(b, s, N_RET, HEAD_DIM), ret_g.reshape(N_RET, HEAD_DIM)).reshape(b, s, W_BRANCH)
    out_ret = h_ret * jax.nn.silu(rg.astype(f32))

    sq, sk, sv = split_cols(p_sb, SB_WIDTHS)
    out_sb = from_heads(stick_breaking(to_heads(sq, N_SB).astype(f32), to_heads(sk, N_SB).astype(f32),
                                       to_heads(sv, N_SB).astype(f32)))

    aq, ak, av = split_cols(p_swa, SWA_WIDTHS)
    aq = rms_norm(to_heads(aq, N_SWA_Q).astype(f32), q_g)
    ak = rms_norm(to_heads(ak, N_SWA_KV).astype(f32), k_g)
    out_swa = from_heads(swa_with_sinks(aq, ak, to_heads(av, N_SWA_KV).astype(f32),
                                        sinks.astype(f32), rel_bias, positions))

    branches = jnp.stack([out_ml, out_ret, out_sb, out_swa], axis=2).astype(h.dtype)
    up = jnp.einsum('bsnw,nwd->bsnd', branches, w_up)
    gates = jax.nn.sigmoid(p_gate.reshape(b, s, N_BRANCH, D_MODEL))
    merged = jnp.sum(gates * up, axis=2)
    return merged @ w_out


def setup_inputs(seed: int = 0) -> dict:
    key = jax.random.key(seed)
    ks = jax.random.split(key, 20)
    nrm = jax.random.normal
    x = nrm(ks[0], (BATCH, SEQ, D_MODEL), jnp.float32)
    c = nrm(ks[1], (BATCH, D_MODEL), jnp.float32)
    start = jax.random.randint(ks[2], (BATCH, 1), 0, 1024, dtype=jnp.int32)
    positions = (start + jnp.arange(SEQ, dtype=jnp.int32)[None, :]).astype(jnp.int32)
    w_ada = nrm(ks[3], (DEPTH, 2, D_MODEL, 3 * D_MODEL), jnp.float32) * (0.5 * D_MODEL ** -0.5)
    b_ada = nrm(ks[4], (DEPTH, 2, 3 * D_MODEL), jnp.float32) * 0.02
    norm_g = 1.0 + 0.02 * nrm(ks[5], (DEPTH, 2, D_MODEL), jnp.float32)
    w_in = nrm(ks[6], (DEPTH, D_MODEL, D_IN), jnp.float32) * D_MODEL ** -0.5
    mlstm_conv_w = nrm(ks[7], (DEPTH, CONV_K, 2 * W_BRANCH), jnp.float32) * CONV_K ** -0.5
    mlstm_conv_b = 0.02 * nrm(ks[8], (DEPTH, 2 * W_BRANCH), jnp.float32)
    i_bias = 0.1 * nrm(ks[9], (DEPTH, 1, N_MLSTM), jnp.float32)
    f_bias = 3.0 + 3.0 * jax.random.uniform(ks[10], (DEPTH, 1, N_MLSTM), jnp.float32)
    mlstm_gate_b = jnp.concatenate([i_bias, f_bias], axis=1)
    ret_norm_g = 1.0 + 0.02 * nrm(ks[11], (DEPTH, W_BRANCH), jnp.float32)
    swa_q_norm_g = 1.0 + 0.02 * nrm(ks[12], (DEPTH, HEAD_DIM), jnp.float32)
    swa_k_norm_g = 1.0 + 0.02 * nrm(ks[13], (DEPTH, HEAD_DIM), jnp.float32)
    swa_sinks = nrm(ks[14], (DEPTH, N_SWA_Q), jnp.float32)
    rel_bias = 0.5 * nrm(ks[15], (N_BUCKETS, N_SWA_Q), jnp.float32)
    w_up = nrm(ks[16], (DEPTH, N_BRANCH, W_BRANCH, D_MODEL), jnp.float32) * W_BRANCH ** -0.5
    w_out = nrm(ks[17], (DEPTH, D_MODEL, D_MODEL), jnp.float32) * D_MODEL ** -0.5
    w_ff1 = nrm(ks[18], (DEPTH, D_MODEL, D_FF), jnp.float32) * D_MODEL ** -0.5
    w_ff2 = nrm(ks[19], (DEPTH, D_FF, D_MODEL), jnp.float32) * D_FF ** -0.5
    return {'x': x, 'c': c, 'positions': positions, 'w_ada': w_ada, 'b_ada': b_ada, 'norm_g': norm_g,
            'w_in': w_in, 'mlstm_conv_w': mlstm_conv_w, 'mlstm_conv_b': mlstm_conv_b,
            'mlstm_gate_b': mlstm_gate_b, 'ret_norm_g': ret_norm_g, 'swa_q_norm_g': swa_q_norm_g,
            'swa_k_norm_g': swa_k_norm_g, 'swa_sinks': swa_sinks, 'rel_bias': rel_bias, 'w_up': w_up,
            'w_out': w_out, 'w_ff1': w_ff1, 'w_ff2': w_ff2}


def reference(x, c, positions, w_ada, b_ada, norm_g, w_in, mlstm_conv_w, mlstm_conv_b, mlstm_gate_b,
              ret_norm_g, swa_q_norm_g, swa_k_norm_g, swa_sinks, rel_bias, w_up, w_out, w_ff1, w_ff2):
    log_gamma = jnp.log(1.0 - jnp.exp2(-(RET_DECAY_BASE + jnp.arange(N_RET, dtype=jnp.float32))))
    c_act = jax.nn.silu(c)
    for l in range(DEPTH):
        mod = jnp.einsum('bd,jde->bje', c_act, w_ada[l]) + b_ada[l]
        shift1, scale1, gate1 = jnp.split(mod[:, 0], 3, axis=-1)
        shift2, scale2, gate2 = jnp.split(mod[:, 1], 3, axis=-1)
        h = rms_norm(x, norm_g[l, 0]) * (1.0 + scale1[:, None, :]) + shift1[:, None, :]
        mix = token_mixer(h, positions, w_in[l], mlstm_conv_w[l], mlstm_conv_b[l], mlstm_gate_b[l],
                          ret_norm_g[l], swa_q_norm_g[l], swa_k_norm_g[l], swa_sinks[l], rel_bias,
                          w_up[l], w_out[l], log_gamma)
        x = x + gate1[:, None, :] * mix.astype(x.dtype)
        h2 = rms_norm(x, norm_g[l, 1]) * (1.0 + scale2[:, None, :]) + shift2[:, None, :]
        ff = jnp.square(jax.nn.relu(h2 @ w_ff1[l])) @ w_ff2[l]
        x = x + gate2[:, None, :] * ff
    return x
```

```python
import functools
import math

import numpy as np
import jax
import jax.numpy as jnp
from jax import lax
from jax.experimental import pallas as pl
from jax.experimental.pallas import tpu as pltpu

F32 = jnp.float32
BF16 = jnp.bfloat16

D_MODEL = 1024
HEAD_DIM = 64
N_HEADS = 4
W_BRANCH = N_HEADS * HEAD_DIM
LANES = 128
CHUNK = 128
CONV_K = 4
D_FF = 4 * D_MODEL
N_BUCKETS = 32
MAX_DIST = 128
ROPE_BASE = 10000.0
RET_DECAY_BASE = 5.0
EPS = 1e-6
NEG = -1e30
VMEM_LIMIT = 56 * 1024 * 1024

SEG_ML = (0, 1024)
SEG_IFC = (1024, 1280)
SEG_RET = (1280, 2304)
SEG_SB = (2304, 3072)
SEG_SWA = (3072, 3840)
N_IN = 3840


def _dot(a, b):
    return jnp.dot(a, b, preferred_element_type=F32)


def _dot_nt(a, b):
    return lax.dot_general(a, b, (((1,), (1,)), ((), ())), preferred_element_type=F32)


def _dot_tn(a, b):
    return lax.dot_general(a, b, (((0,), (0,)), ((), ())), preferred_element_type=F32)


def _split2(x):
    hi = x.astype(BF16)
    lo = (x - hi.astype(F32)).astype(BF16)
    return hi, lo


def _split3(x):
    hi = x.astype(BF16)
    r = x - hi.astype(F32)
    mid = r.astype(BF16)
    lo = (r - mid.astype(F32)).astype(BF16)
    return hi, mid, lo


def _iota(shape, axis):
    return lax.broadcasted_iota(jnp.int32, shape, axis)


def _log_sigmoid(x):
    return jnp.minimum(x, 0.0) - jnp.log(1.0 + jnp.exp(-jnp.abs(x)))


def _sigmoid(x):
    return 1.0 / (1.0 + jnp.exp(-x))


def _silu(x):
    return x * _sigmoid(x)


def _norm_mod(x, g, scale, shift):
    ms = jnp.mean(x * x, axis=-1, keepdims=True)
    y = x * lax.rsqrt(ms + EPS)
    return (y * g) * (1.0 + scale) + shift


def _head_mean_sq(x):
    lane = _iota(x.shape, 1)
    sq = x * x
    s0 = jnp.sum(jnp.where(lane < HEAD_DIM, sq, 0.0), axis=-1, keepdims=True)
    s1 = jnp.sum(jnp.where(lane >= HEAD_DIM, sq, 0.0), axis=-1, keepdims=True)
    return jnp.where(lane < HEAD_DIM, s0, s1) * (1.0 / HEAD_DIM)


def _params(sem):
    return pltpu.CompilerParams(dimension_semantics=sem, vmem_limit_bytes=VMEM_LIMIT)


def _ada_kernel(c_ref, w_ref, b_ref, o_ref):
    c = c_ref[...]
    ch, cl = _split2(_silu(c))
    wh, wl = _split2(w_ref[0])
    o_ref[0] = _dot(ch, wh) + _dot(ch, wl) + _dot(cl, wh) + b_ref[0]


def _ada(c8, w_ada, b_ada):
    n_mod, d, n3 = w_ada.shape
    tn = 1024
    return pl.pallas_call(
        _ada_kernel,
        grid=(n_mod, n3 // tn),
        in_specs=[pl.BlockSpec((8, d), lambda m, n: (0, 0)),
                  pl.BlockSpec((1, d, tn), lambda m, n: (m, 0, n)),
                  pl.BlockSpec((1, 1, tn), lambda m, n: (m, 0, n))],
        out_specs=pl.BlockSpec((1, 8, tn), lambda m, n: (m, 0, n)),
        out_shape=jax.ShapeDtypeStruct((n_mod, 8, n3), F32),
        compiler_params=_params(("parallel", "parallel")),
        name="ada_mod",
    )(c8, w_ada, b_ada)


def _rope_kernel(pos_ref, inv_ref, cos_ref, sin_ref):
    ang = pos_ref[...].astype(F32) * inv_ref[...]
    cos_ref[...] = jnp.cos(ang)
    sin_ref[...] = jnp.sin(ang)


def _rope_tables(pos_b, inv_row):
    t = pos_b.shape[0]
    tm = 1024
    spec = pl.BlockSpec((tm, LANES), lambda i: (i, 0))
    return pl.pallas_call(
        _rope_kernel,
        grid=(t // tm,),
        in_specs=[spec, pl.BlockSpec((1, LANES), lambda i: (0, 0))],
        out_specs=[spec, spec],
        out_shape=[jax.ShapeDtypeStruct((t, LANES), F32)] * 2,
        compiler_params=_params(("parallel",)),
        name="rope_tables",
    )(pos_b, inv_row)


def _bias_kernel(tab_ref, posq_ref, posc_ref, posp_ref, o_ref):
    pos_q = posq_ref[...]
    rel = jnp.concatenate([pos_q - posp_ref[0], pos_q - posc_ref[0]], axis=1)
    n = jnp.maximum(rel, 0)
    max_exact = N_BUCKETS // 2
    nf = jnp.maximum(n, 1).astype(F32)
    large = max_exact + (jnp.log(nf / max_exact) / math.log(MAX_DIST / max_exact)
                         * (N_BUCKETS - max_exact)).astype(jnp.int32)
    large = jnp.minimum(large, N_BUCKETS - 1)
    bucket = jnp.where(n < max_exact, n, large)
    acc = [jnp.zeros(rel.shape, F32) for _ in range(N_HEADS)]
    for b in range(N_BUCKETS):
        hit = bucket == b
        for h in range(N_HEADS):
            acc[h] = jnp.where(hit, tab_ref[b, h], acc[h])
    for h in range(N_HEADS):
        o_ref[0, h] = acc[h]


def _bias_tiles(rel_bias, pos_b, pos_r, batch, n_b):
    nblk = batch * n_b
    return pl.pallas_call(
        _bias_kernel,
        grid=(nblk,),
        in_specs=[pl.BlockSpec(memory_space=pltpu.SMEM),
                  pl.BlockSpec((CHUNK, LANES), lambda i: (i, 0)),
                  pl.BlockSpec((1, 1, CHUNK), lambda i: (i, 0, 0)),
                  pl.BlockSpec((1, 1, CHUNK), lambda i: (jnp.maximum(i - 1, 0), 0, 0))],
        out_specs=pl.BlockSpec((1, N_HEADS, CHUNK, 2 * CHUNK), lambda i: (i, 0, 0, 0)),
        out_shape=jax.ShapeDtypeStruct((nblk, N_HEADS, CHUNK, 2 * CHUNK), F32),
        compiler_params=_params(("parallel",)),
        name="t5_bias",
    )(rel_bias, pos_b, pos_r, pos_r)


def _in_kernel(x_ref, g_ref, sc_ref, sh_ref, w_ref, wt_ref, gb_ref, gbt_ref,
               ml_ref, ifc_ref, ret_ref, sb_ref, swa_ref, ift_ref):
    h = _norm_mod(x_ref[...], g_ref[...], sc_ref[0], sh_ref[0]).astype(BF16)

    def seg(s):
        return _dot(h, w_ref[:, s[0]:s[1]])

    ml_ref[...] = seg(SEG_ML)
    ifc_ref[...] = seg(SEG_IFC) + gb_ref[...]
    ret_ref[...] = seg(SEG_RET)
    sb_ref[...] = seg(SEG_SB).astype(BF16)
    swa_ref[...] = seg(SEG_SWA)
    ift_ref[...] = _dot_nt(wt_ref[...], h) + gbt_ref[:, 0:1]


def _in_proj(x, g, scale, shift, w_perm, w_ift, gb_row, gbt, seq):
    t, d = x.shape
    tm = 512
    tpb = seq // tm
    row = lambda n: pl.BlockSpec((tm, n), lambda i: (i, 0))
    full = lambda a: pl.BlockSpec(a.shape, lambda i: (0,) * a.ndim)
    mod = pl.BlockSpec((1, 1, d), lambda i: (i // tpb, 0, 0))
    widths = [s[1] - s[0] for s in (SEG_ML, SEG_IFC, SEG_RET, SEG_SB, SEG_SWA)]
    dts = [F32, F32, F32, BF16, F32]
    return pl.pallas_call(
        _in_kernel,
        grid=(t // tm,),
        in_specs=[row(d), full(g), mod, mod, full(w_perm), full(w_ift), full(gb_row), full(gbt)],
        out_specs=[row(n) for n in widths] + [pl.BlockSpec((16, tm), lambda i: (0, i))],
        out_shape=[jax.ShapeDtypeStruct((t, n), dt) for n, dt in zip(widths, dts)]
        + [jax.ShapeDtypeStruct((16, t), F32)],
        compiler_params=_params(("parallel",)),
        name="in_proj",
    )(x, g, scale, shift, w_perm, w_ift, gb_row, gbt)


def _ml_kernel(q_ref, k_ref, v_ref, o_ref, ifc_ref, ift_ref, cwq_ref, cwk_ref, cbq_ref, cbk_ref,
               out_ref, halo_q, halo_k, c_st, m_st):
    L = CHUNK
    HALO = 8

    @pl.when(pl.program_id(2) == 0)
    def _init():
        halo_q[0:HALO, :] = jnp.zeros((HALO, LANES), F32)
        halo_k[0:HALO, :] = jnp.zeros((HALO, LANES), F32)
        c_st[...] = jnp.zeros(c_st.shape, F32)
        m_st[...] = jnp.zeros(m_st.shape, F32)

    def conv_silu(raw_ref, halo, w_ref, b_ref):
        halo[HALO:HALO + L, :] = raw_ref[...]
        acc = jnp.zeros((L, LANES), F32) + b_ref[...]
        for j in range(CONV_K):
            off = HALO - (CONV_K - 1) + j
            acc = acc + halo[off:off + L, :] * w_ref[j:j + 1, :]
        halo[0:HALO, :] = halo[L:L + HALO, :]
        return _silu(acc)

    q = conv_silu(q_ref, halo_q, cwq_ref, cbq_ref)
    k = conv_silu(k_ref, halo_k, cwk_ref, cbk_ref)
    v = v_ref[...]
    k_bf = k.astype(BF16)

    lane = _iota((L, LANES), 1)
    row = _iota((L, LANES), 0)
    causal = row >= lane
    tri = jnp.where(causal, 1.0, 0.0).astype(BF16)
    upper = jnp.where(row <= lane, 1.0, 0.0).astype(BF16)

    ifc = ifc_ref[...]
    ift = ift_ref[...]
    lf_h, lf_m, lf_l = _split3(_log_sigmoid(ifc))
    a_cols = _dot(tri, lf_h) + _dot(tri, lf_m) + _dot(tri, lf_l)
    lr_h, lr_m, lr_l = _split3(_log_sigmoid(ift))
    a_rows = _dot(lr_h, upper) + _dot(lr_m, upper) + _dot(lr_l, upper)

    outs = []
    for hh in range(2):
        hm = (lane >= HEAD_DIM) if hh else (lane < HEAD_DIM)
        qh = (jnp.where(hm, q, 0.0) * (HEAD_DIM ** -0.5)).astype(BF16)
        kh = jnp.where(hm, k, 0.0)
        v_aug = jnp.where(hm, v, 1.0).astype(BF16)
        i_col = ifc[:, hh:hh + 1]
        a_col = a_cols[:, 2 + hh:3 + hh]
        i_row = ift[hh:hh + 1, :]
        a_row = a_rows[2 + hh:3 + hh, :]
        a_last = a_rows[2 + hh:3 + hh, L - 1:L]
        m_in = m_st[hh, 0:1, 0:1]
        c_in = c_st[hh]

        d_intra = jnp.where(causal, a_col - (a_row - i_row), NEG)
        m_intra = jnp.max(d_intra, axis=-1, keepdims=True)
        m_inter = a_col + m_in
        m_tot = jnp.maximum(m_inter, m_intra)
        s_inter = jnp.exp(m_inter - m_tot)
        p = jnp.exp(d_intra - m_tot) * _dot_nt(qh, k_bf)
        o_aug = s_inter * _dot(qh, c_in.astype(BF16)) + _dot(p.astype(BF16), v_aug)
        den_lane = HEAD_DIM if hh == 0 else 0
        den = o_aug[:, den_lane:den_lane + 1]
        outs.append(o_aug / jnp.maximum(jnp.abs(den), jnp.exp(-m_tot)))

        g_col = a_last - a_col + i_col
        m_loc = jnp.max(g_col, axis=0, keepdims=True)
        kw = (kh * jnp.exp(g_col - m_loc)).astype(BF16)
        c_loc = _dot_tn(kw, v_aug)
        m_new = jnp.maximum(a_last + m_in, m_loc)
        c_st[hh] = jnp.exp(a_last + m_in - m_new) * c_in + jnp.exp(m_loc - m_new) * c_loc
        m_st[hh] = jnp.zeros((8, LANES), F32) + m_new

    h_ml = jnp.where(lane < HEAD_DIM, outs[0], outs[1])
    out_ref[...] = (h_ml * _sigmoid(o_ref[...])).astype(BF16)


def _mlstm(ml, ifc, ift, conv_w, conv_b, batch, seq):
    nc = seq // CHUNK
    blk = lambda off: pl.BlockSpec((CHUNK, LANES), lambda b, p, c: (b * nc + c, off + p))
    cw = lambda off: pl.BlockSpec((CONV_K, LANES), lambda b, p, c: (0, off + p))
    cb = lambda off: pl.BlockSpec((1, LANES), lambda b, p, c: (0, off + p))
    return pl.pallas_call(
        _ml_kernel,
        grid=(batch, 2, nc),
        in_specs=[blk(0), blk(2), blk(4), blk(6), blk(0),
                  pl.BlockSpec((8, CHUNK), lambda b, p, c: (p, b * nc + c)),
                  cw(0), cw(2), cb(0), cb(2)],
        out_specs=blk(0),
        out_shape=jax.ShapeDtypeStruct((batch * seq, W_BRANCH), BF16),
        scratch_shapes=[pltpu.VMEM((CHUNK + 8, LANES), F32), pltpu.VMEM((CHUNK + 8, LANES), F32),
                        pltpu.VMEM((2, LANES, LANES), F32), pltpu.VMEM((2, 8, LANES), F32)],
        compiler_params=_params(("parallel", "parallel", "arbitrary")),
        name="mlstm",
    )(ml, ml, ml, ml, ifc, ift, conv_w, conv_w, conv_b, conv_b)


def _ret_kernel(q_ref, k_ref, v_ref, g_ref, cos_ref, sin_ref, lg_ref, ng_ref, out_ref, st):
    L = CHUNK

    @pl.when(pl.program_id(2) == 0)
    def _init():
        st[...] = jnp.zeros(st.shape, F32)

    lane = _iota((L, LANES), 1)
    row = _iota((L, LANES), 0)
    cos = cos_ref[...]
    sin = sin_ref[...]
    first = (lane % HEAD_DIM) < (HEAD_DIM // 2)

    def rot(t):
        partner = jnp.where(first, -pltpu.roll(t, LANES - HEAD_DIM // 2, 1), pltpu.roll(t, HEAD_DIM // 2, 1))
        return t * cos + partner * sin

    lg = lg_ref[0, 0:1, :]
    rowf = row.astype(F32)
    q = rot(q_ref[...])
    k = rot(k_ref[...]) * (HEAD_DIM ** -0.5)
    v_bf = v_ref[...].astype(BF16)
    q_bf = q.astype(BF16)
    k_bf = k.astype(BF16)
    state = st[...]

    relf = (row - lane).astype(F32)
    intra = []
    for hh in range(2):
        hm = (lane >= HEAD_DIM) if hh else (lane < HEAD_DIM)
        lg_h = lg[:, hh * HEAD_DIM:hh * HEAD_DIM + 1]
        decay = jnp.where(row >= lane, jnp.exp(jnp.maximum(relf, 0.0) * lg_h), 0.0)
        qh = jnp.where(hm, q, 0.0).astype(BF16)
        sc = _dot_nt(qh, k_bf) * decay
        intra.append(_dot(sc.astype(BF16), v_bf))
    o = jnp.where(lane < HEAD_DIM, intra[0], intra[1])
    o = o + _dot(q_bf, state.astype(BF16)) * jnp.exp((rowf + 1.0) * lg)

    same_head = (row // HEAD_DIM) == (lane // HEAD_DIM)
    kd = (k * jnp.exp((L - 1.0 - rowf) * lg)).astype(BF16)
    st[...] = jnp.exp(float(L) * lg) * state + jnp.where(same_head, _dot_tn(kd, v_bf), 0.0)

    y = o * lax.rsqrt(_head_mean_sq(o) + EPS) * ng_ref[...]
    out_ref[...] = (y * _silu(g_ref[...])).astype(BF16)


def _retention(ret, cos_t, sin_t, lg_tab, ret_g, batch, seq):
    nc = seq // CHUNK
    blk = lambda off: pl.BlockSpec((CHUNK, LANES), lambda b, p, c: (b * nc + c, off + p))
    tab = pl.BlockSpec((CHUNK, LANES), lambda b, p, c: (b * nc + c, 0))
    return pl.pallas_call(
        _ret_kernel,
        grid=(batch, 2, nc),
        in_specs=[blk(0), blk(2), blk(4), blk(6), tab, tab,
                  pl.BlockSpec((1, 8, LANES), lambda b, p, c: (p, 0, 0)),
                  pl.BlockSpec((1, LANES), lambda b, p, c: (0, p))],
        out_specs=blk(0),
        out_shape=jax.ShapeDtypeStruct((batch * seq, W_BRANCH), BF16),
        scratch_shapes=[pltpu.VMEM((LANES, LANES), F32)],
        compiler_params=_params(("parallel", "parallel", "arbitrary")),
        name="retention",
    )(ret, ret, ret, ret, cos_t, sin_t, lg_tab, ret_g)


def _sb_kernel(q_ref, k_ref, v_ref, out_ref):
    L = CHUNK
    i = pl.program_id(2)
    lane = _iota((L, LANES), 1)
    row = _iota((L, LANES), 0)
    q = q_ref[...] * (HEAD_DIM ** -0.5)
    qh = [jnp.where(lane < HEAD_DIM, q, 0.0).astype(BF16), jnp.where(lane >= HEAD_DIM, q, 0.0).astype(BF16)]

    r2 = _iota((2 * L, 2 * L), 0) % L
    c2 = _iota((2 * L, 2 * L), 1)
    suffix_op = jnp.where((c2 >= L) | (r2 >= c2), 1.0, 0.0).astype(BF16)

    def tile(z, mask, carry, acc, v_blk):
        lk = -(jnp.maximum(z, 0.0) + jnp.log(1.0 + jnp.exp(-jnp.abs(z))))
        if mask is not None:
            lk = jnp.where(mask, lk, 0.0)
        hi, lo = _split2(lk)
        rr = _dot(jnp.concatenate([hi, lo], axis=1), suffix_op)
        w = jnp.exp(z + rr[:, :L] + carry)
        if mask is not None:
            w = jnp.where(mask, w, 0.0)
        return carry + rr[:, L:], acc + _dot(w.astype(BF16), v_blk)

    off = pl.multiple_of(i * L, L)
    k_blk = k_ref[pl.ds(off, L), :]
    v_blk = v_ref[pl.ds(off, L), :]
    strict = lane < row
    zero = jnp.zeros((L, LANES), F32)
    state = []
    for hh in range(2):
        state.extend(tile(_dot_nt(qh[hh], k_blk), strict, zero, zero, v_blk))

    def body(jj, st):
        j = i - 1 - jj
        o = pl.multiple_of(j * L, L)
        kb = k_ref[pl.ds(o, L), :]
        vb = v_ref[pl.ds(o, L), :]
        new = []
        for hh in range(2):
            new.extend(tile(_dot_nt(qh[hh], kb), None, st[2 * hh], st[2 * hh + 1], vb))
        return tuple(new)

    st = lax.fori_loop(0, i, body, tuple(state))
    out_ref[...] = jnp.where(lane < HEAD_DIM, st[1], st[3]).astype(BF16)


def _stick_breaking(sb, batch, seq):
    nb = seq // CHUNK
    kv = lambda off: pl.BlockSpec((seq, LANES), lambda b, p, i: (b, off + p))
    return pl.pallas_call(
        _sb_kernel,
        grid=(batch, 2, nb),
        in_specs=[pl.BlockSpec((CHUNK, LANES), lambda b, p, i: (b * nb + i, p)), kv(2), kv(4)],
        out_specs=pl.BlockSpec((CHUNK, LANES), lambda b, p, i: (b * nb + i, p)),
        out_shape=jax.ShapeDtypeStruct((batch * seq, W_BRANCH), BF16),
        compiler_params=_params(("parallel", "parallel", "arbitrary")),
        name="stick_breaking",
    )(sb, sb, sb)


def _swa_kernel(sink_ref, q_ref, kc_ref, kp_ref, vc_ref, vp_ref, bias_ref, qg_ref, kg_ref, out_ref, *, nb):
    L = CHUNK
    i = pl.program_id(0) % nb
    lane = _iota((L, LANES), 1)
    t = _iota((L, 2 * L), 0)
    j = _iota((L, 2 * L), 1)
    dist = t + L - j
    valid = (dist >= 0) & (dist < L) & ((j >= L) | (i > 0))

    def qk_norm(x, g_ref):
        return x * lax.rsqrt(_head_mean_sq(x) + EPS) * g_ref[...]

    for g in range(2):
        sl = slice(g * LANES, (g + 1) * LANES)
        qn = qk_norm(q_ref[:, sl], qg_ref) * (HEAD_DIM ** -0.5)
        kcat = jnp.concatenate([qk_norm(kp_ref[:, sl], kg_ref), qk_norm(kc_ref[:, sl], kg_ref)], axis=0).astype(BF16)
        vcat = jnp.concatenate([vp_ref[:, sl], vc_ref[:, sl]], axis=0).astype(BF16)
        outs = []
        for r in range(2):
            hq = 2 * g + r
            hm = (lane >= HEAD_DIM) if r else (lane < HEAD_DIM)
            qh = jnp.where(hm, qn, 0.0).astype(BF16)
            logits = jnp.where(valid, _dot_nt(qh, kcat) + bias_ref[0, hq], NEG)
            sink = sink_ref[hq]
            m = jnp.maximum(jnp.max(logits, axis=-1, keepdims=True), sink)
            p = jnp.exp(logits - m)
            den = jnp.sum(p, axis=-1, keepdims=True) + jnp.exp(sink - m)
            outs.append(_dot((p / den).astype(BF16), vcat))
        out_ref[:, sl] = jnp.where(lane < HEAD_DIM, outs[0], outs[1]).astype(BF16)


def _swa(swa, bias, sinks, q_g, k_g, batch, seq):
    nb = seq // CHUNK
    cur = lambda off: pl.BlockSpec((CHUNK, 2 * LANES), lambda i: (i, off))
    prev = lambda off: pl.BlockSpec((CHUNK, 2 * LANES), lambda i: (jnp.maximum(i - 1, 0), off))
    gain = pl.BlockSpec((1, LANES), lambda i: (0, 0))
    return pl.pallas_call(
        functools.partial(_swa_kernel, nb=nb),
        grid=(batch * nb,),
        in_specs=[pl.BlockSpec(memory_space=pltpu.SMEM), cur(0), cur(1), prev(1), cur(2), prev(2),
                  pl.BlockSpec((1, N_HEADS, CHUNK, 2 * CHUNK), lambda i: (i, 0, 0, 0)), gain, gain],
        out_specs=pl.BlockSpec((CHUNK, W_BRANCH), lambda i: (i, 0)),
        out_shape=jax.ShapeDtypeStruct((batch * seq, W_BRANCH), BF16),
        compiler_params=_params(("parallel",)),
        name="swa",
    )(sinks, swa, swa, swa, swa, swa, bias, q_g, k_g)


def _merge_kernel(x_ref, g_ref, sc_ref, sh_ref, gt_ref, b0_ref, b1_ref, b2_ref, b3_ref,
                  wg_ref, wu_ref, wo_ref, out_ref):
    x = x_ref[...]
    h = _norm_mod(x, g_ref[...], sc_ref[0], sh_ref[0]).astype(BF16)
    merged = None
    for n, b_ref in enumerate((b0_ref, b1_ref, b2_ref, b3_ref)):
        gate = _sigmoid(_dot(h, wg_ref[:, n * D_MODEL:(n + 1) * D_MODEL]))
        term = gate * _dot(b_ref[...], wu_ref[n])
        merged = term if merged is None else merged + term
    mix = _dot(merged.astype(BF16), wo_ref[...])
    out_ref[...] = x + gt_ref[0] * mix


def _merge(x, g, scale, shift, gate, branches, w_gate, w_up, w_out, seq):
    t, d = x.shape
    tm = 256
    tpb = seq // tm
    row = lambda n: pl.BlockSpec((tm, n), lambda i: (i, 0))
    full = lambda a: pl.BlockSpec(a.shape, lambda i: (0,) * a.ndim)
    mod = pl.BlockSpec((1, 1, d), lambda i: (i // tpb, 0, 0))
    return pl.pallas_call(
        _merge_kernel,
        grid=(t // tm,),
        in_specs=[row(d), full(g), mod, mod, mod] + [row(W_BRANCH)] * 4 + [full(w_gate), full(w_up), full(w_out)],
        out_specs=row(d),
        out_shape=jax.ShapeDtypeStruct((t, d), F32),
        compiler_params=_params(("parallel",)),
        name="merge",
    )(x, g, scale, shift, gate, *branches, w_gate, w_up, w_out)


def _ffn_kernel(x_ref, g_ref, sc_ref, sh_ref, gt_ref, w1_ref, w2_ref, out_ref):
    x = x_ref[...]
    h = _norm_mod(x, g_ref[...], sc_ref[0], sh_ref[0]).astype(BF16)
    tf = 1024
    acc = None
    for n in range(D_FF // tf):
        a = jnp.maximum(_dot(h, w1_ref[:, n * tf:(n + 1) * tf]), 0.0)
        part = _dot((a * a).astype(BF16), w2_ref[n * tf:(n + 1) * tf, :])
        acc = part if acc is None else acc + part
    out_ref[...] = x + gt_ref[0] * acc


def _ffn(x, g, scale, shift, gate, w1, w2, seq):
    t, d = x.shape
    tm = 256
    tpb = seq // tm
    row = pl.BlockSpec((tm, d), lambda i: (i, 0))
    full = lambda a: pl.BlockSpec(a.shape, lambda i: (0,) * a.ndim)
    mod = pl.BlockSpec((1, 1, d), lambda i: (i // tpb, 0, 0))
    return pl.pallas_call(
        _ffn_kernel,
        grid=(t // tm,),
        in_specs=[row, full(g), mod, mod, mod, full(w1), full(w2)],
        out_specs=row,
        out_shape=jax.ShapeDtypeStruct((t, d), F32),
        compiler_params=_params(("parallel",)),
        name="ffn",
    )(x, g, scale, shift, gate, w1, w2)


def _permute_w_in(w):
    d = w.shape[0]
    hd = HEAD_DIM
    o_ret = 3 * W_BRANCH + 2 * N_HEADS + W_BRANCH
    o_sb = o_ret + 4 * W_BRANCH
    o_swa = o_sb + 3 * W_BRANCH
    mi = w[:, 768:772]
    mf = w[:, 772:776]
    zpad = jnp.zeros((d, LANES - 4), w.dtype)
    ifc = jnp.concatenate([mi[:, 0:2], mf[:, 0:2], zpad, mi[:, 2:4], mf[:, 2:4], zpad], axis=1)
    aq = w[:, o_swa:o_swa + 256]
    ak = w[:, o_swa + 256:o_swa + 384]
    av = w[:, o_swa + 384:o_swa + 512]
    dup = lambda a: jnp.concatenate([a[:, :hd], a[:, :hd], a[:, hd:], a[:, hd:]], axis=1)
    w_perm = jnp.concatenate([w[:, 0:768], w[:, 776:1032], ifc, w[:, o_ret:o_sb], w[:, o_sb:o_swa],
                              aq, dup(ak), dup(av)], axis=1)
    z4 = jnp.zeros((d, 4), w.dtype)
    ift = jnp.concatenate([mi[:, 0:2], mf[:, 0:2], z4, mi[:, 2:4], mf[:, 2:4], z4], axis=1).T
    return w_perm.astype(BF16), ift.astype(BF16)


def _gate_bias_layout(gate_b):
    ib, fb = gate_b[0], gate_b[1]
    z = jnp.zeros((LANES - 4,), F32)
    row = jnp.concatenate([ib[0:2], fb[0:2], z, ib[2:4], fb[2:4], z]).reshape(1, 2 * LANES)
    z4 = jnp.zeros((4,), F32)
    col = jnp.concatenate([ib[0:2], fb[0:2], z4, ib[2:4], fb[2:4], z4])
    return row, jnp.broadcast_to(col[:, None], (16, LANES))


def kernel(x, c, positions, w_ada, b_ada, norm_g, w_in, mlstm_conv_w, mlstm_conv_b, mlstm_gate_b,
           ret_norm_g, swa_q_norm_g, swa_k_norm_g, swa_sinks, rel_bias, w_up, w_out, w_ff1, w_ff2):
    batch, seq, d = x.shape
    depth = w_in.shape[0]
    t = batch * seq
    nb = seq // CHUNK

    c8 = jnp.concatenate([c, jnp.zeros((8 - batch, d), F32)], axis=0)
    mod = _ada(c8, w_ada.reshape(depth * 2, d, 3 * d), b_ada.reshape(depth * 2, 1, 3 * d))
    mod = mod[:, :batch].reshape(depth, 2, batch, 3, 1, d)

    pos_col = jnp.broadcast_to(positions.reshape(t, 1), (t, LANES))
    pos_row = positions.reshape(batch * nb, 1, CHUNK)
    half = HEAD_DIM // 2
    inv = ROPE_BASE ** (-(np.arange(LANES) % half).astype(np.float64) / half)
    cos_t, sin_t = _rope_tables(pos_col, jnp.asarray(inv, F32).reshape(1, LANES))
    bias = _bias_tiles(rel_bias, pos_col, pos_row, batch, nb)

    log_gamma = np.log(1.0 - np.exp2(-(RET_DECAY_BASE + np.arange(N_HEADS, dtype=np.float64))))
    lg_tab = jnp.asarray(np.broadcast_to(np.repeat(log_gamma, HEAD_DIM).reshape(2, 1, LANES), (2, 8, LANES)), F32)

    xt = x.reshape(t, d)
    for l in range(depth):
        w_perm, w_ift = _permute_w_in(w_in[l])
        gb_row, gbt = _gate_bias_layout(mlstm_gate_b[l])
        g1 = norm_g[l, 0].reshape(1, d)
        g2 = norm_g[l, 1].reshape(1, d)
        shift1, scale1, gate1 = mod[l, 0, :, 0], mod[l, 0, :, 1], mod[l, 0, :, 2]
        shift2, scale2, gate2 = mod[l, 1, :, 0], mod[l, 1, :, 1], mod[l, 1, :, 2]

        ml, ifc, ret, sb, swa, ift = _in_proj(xt, g1, scale1, shift1, w_perm, w_ift, gb_row, gbt, seq)
        out_ml = _mlstm(ml, ifc, ift, mlstm_conv_w[l], mlstm_conv_b[l].reshape(1, 2 * W_BRANCH), batch, seq)
        out_ret = _retention(ret, cos_t, sin_t, lg_tab, ret_norm_g[l].reshape(1, W_BRANCH), batch, seq)
        out_sb = _stick_breaking(sb, batch, seq)
        qg = jnp.tile(swa_q_norm_g[l], 2).reshape(1, LANES)
        kg = jnp.tile(swa_k_norm_g[l], 2).reshape(1, LANES)
        out_swa = _swa(swa, bias, swa_sinks[l], qg, kg, batch, seq)

        w_gate = w_in[l][:, w_in.shape[2] - N_HEADS * d:].astype(BF16)
        xt = _merge(xt, g1, scale1, shift1, gate1, (out_ml, out_ret, out_sb, out_swa),
                    w_gate, w_up[l].astype(BF16), w_out[l].astype(BF16), seq)
        xt = _ffn(xt, g2, scale2, shift2, gate2, w_ff1[l].astype(BF16), w_ff2[l].astype(BF16), seq)
    return xt.reshape(batch, seq, d)
```

```python
import functools
import math

import numpy as np
import jax
import jax.numpy as jnp
from jax import lax
from jax.experimental import pallas as pl
from jax.experimental.pallas import tpu as pltpu

F32 = jnp.float32
BF16 = jnp.bfloat16

D_MODEL = 1024
HEAD_DIM = 64
N_HEADS = 4
W_BRANCH = N_HEADS * HEAD_DIM
LANES = 128
CHUNK = 128
CONV_K = 4
D_FF = 4 * D_MODEL
N_BUCKETS = 32
MAX_DIST = 128
ROPE_BASE = 10000.0
RET_DECAY_BASE = 5.0
EPS = 1e-6
NEG = -1e30
F32_EXP_UNDERFLOW = -104.0
VMEM_LIMIT = 56 * 1024 * 1024

SEG_ML = (0, 1024)
SEG_IFC = (1024, 1280)
SEG_RET = (1280, 2304)
SEG_SB = (2304, 3072)
SEG_SWA = (3072, 3840)
N_IN = 3840


def _dot(a, b):
    return jnp.dot(a, b, preferred_element_type=F32)


def _dot_nt(a, b):
    return lax.dot_general(a, b, (((1,), (1,)), ((), ())), preferred_element_type=F32)


def _dot_tn(a, b):
    return lax.dot_general(a, b, (((0,), (0,)), ((), ())), preferred_element_type=F32)


def _split2(x):
    hi = x.astype(BF16)
    lo = (x - hi.astype(F32)).astype(BF16)
    return hi, lo


def _split3(x):
    hi = x.astype(BF16)
    r = x - hi.astype(F32)
    mid = r.astype(BF16)
    lo = (r - mid.astype(F32)).astype(BF16)
    return hi, mid, lo


def _iota(shape, axis):
    return lax.broadcasted_iota(jnp.int32, shape, axis)


def _log_sigmoid(x):
    return jnp.minimum(x, 0.0) - jnp.log(1.0 + jnp.exp(-jnp.abs(x)))


def _sigmoid(x):
    return 1.0 / (1.0 + jnp.exp(-x))


def _silu(x):
    return x * _sigmoid(x)


def _norm_mod(x, g, scale, shift):
    ms = jnp.mean(x * x, axis=-1, keepdims=True)
    y = x * lax.rsqrt(ms + EPS)
    return (y * g) * (1.0 + scale) + shift


def _head_mean_sq(x):
    lane = _iota(x.shape, 1)
    sq = x * x
    s0 = jnp.sum(jnp.where(lane < HEAD_DIM, sq, 0.0), axis=-1, keepdims=True)
    s1 = jnp.sum(jnp.where(lane >= HEAD_DIM, sq, 0.0), axis=-1, keepdims=True)
    return jnp.where(lane < HEAD_DIM, s0, s1) * (1.0 / HEAD_DIM)


def _params(sem):
    return pltpu.CompilerParams(dimension_semantics=sem, vmem_limit_bytes=VMEM_LIMIT)


def _ada_kernel(c_ref, w_ref, b_ref, o_ref):
    c = c_ref[...]
    ch, cl = _split2(_silu(c))
    wh, wl = _split2(w_ref[0])
    o_ref[0] = _dot(ch, wh) + _dot(ch, wl) + _dot(cl, wh) + b_ref[0]


def _ada(c8, w_ada, b_ada):
    n_mod, d, n3 = w_ada.shape
    tn = 1024
    return pl.pallas_call(
        _ada_kernel,
        grid=(n_mod, n3 // tn),
        in_specs=[pl.BlockSpec((8, d), lambda m, n: (0, 0)),
                  pl.BlockSpec((1, d, tn), lambda m, n: (m, 0, n)),
                  pl.BlockSpec((1, 1, tn), lambda m, n: (m, 0, n))],
        out_specs=pl.BlockSpec((1, 8, tn), lambda m, n: (m, 0, n)),
        out_shape=jax.ShapeDtypeStruct((n_mod, 8, n3), F32),
        compiler_params=_params(("parallel", "parallel")),
        name="ada_mod",
    )(c8, w_ada, b_ada)


def _rope_kernel(pos_ref, inv_ref, cos_ref, sin_ref):
    ang = pos_ref[...].astype(F32) * inv_ref[...]
    cos_ref[...] = jnp.cos(ang)
    sin_ref[...] = jnp.sin(ang)


def _rope_tables(pos_b, inv_row):
    t = pos_b.shape[0]
    tm = 1024
    spec = pl.BlockSpec((tm, LANES), lambda i: (i, 0))
    return pl.pallas_call(
        _rope_kernel,
        grid=(t // tm,),
        in_specs=[spec, pl.BlockSpec((1, LANES), lambda i: (0, 0))],
        out_specs=[spec, spec],
        out_shape=[jax.ShapeDtypeStruct((t, LANES), F32)] * 2,
        compiler_params=_params(("parallel",)),
        name="rope_tables",
    )(pos_b, inv_row)


def _bias_kernel(tab_ref, posq_ref, posc_ref, posp_ref, o_ref):
    pos_q = posq_ref[...]
    rel = jnp.concatenate([pos_q - posp_ref[0], pos_q - posc_ref[0]], axis=1)
    n = jnp.maximum(rel, 0)
    max_exact = N_BUCKETS // 2
    nf = jnp.maximum(n, 1).astype(F32)
    large = max_exact + (jnp.log(nf / max_exact) / math.log(MAX_DIST / max_exact)
                         * (N_BUCKETS - max_exact)).astype(jnp.int32)
    large = jnp.minimum(large, N_BUCKETS - 1)
    bucket = jnp.where(n < max_exact, n, large)
    acc = [jnp.zeros(rel.shape, F32) for _ in range(N_HEADS)]
    for b in range(N_BUCKETS):
        hit = bucket == b
        for h in range(N_HEADS):
            acc[h] = jnp.where(hit, tab_ref[b, h], acc[h])
    for h in range(N_HEADS):
        o_ref[0, h] = acc[h]


def _bias_tiles(rel_bias, pos_b, pos_r, batch, n_b):
    nblk = batch * n_b
    return pl.pallas_call(
        _bias_kernel,
        grid=(nblk,),
        in_specs=[pl.BlockSpec(memory_space=pltpu.SMEM),
                  pl.BlockSpec((CHUNK, LANES), lambda i: (i, 0)),
                  pl.BlockSpec((1, 1, CHUNK), lambda i: (i, 0, 0)),
                  pl.BlockSpec((1, 1, CHUNK), lambda i: (jnp.maximum(i - 1, 0), 0, 0))],
        out_specs=pl.BlockSpec((1, N_HEADS, CHUNK, 2 * CHUNK), lambda i: (i, 0, 0, 0)),
        out_shape=jax.ShapeDtypeStruct((nblk, N_HEADS, CHUNK, 2 * CHUNK), F32),
        compiler_params=_params(("parallel",)),
        name="t5_bias",
    )(rel_bias, pos_b, pos_r, pos_r)


def _in_kernel(x_ref, g_ref, sc_ref, sh_ref, w_ref, wt_ref, gb_ref, gbt_ref,
               ml_ref, ifc_ref, ret_ref, sb_ref, swa_ref, ift_ref):
    h = _norm_mod(x_ref[...], g_ref[...], sc_ref[0], sh_ref[0]).astype(BF16)

    def seg(s):
        return _dot(h, w_ref[:, s[0]:s[1]])

    ml_ref[...] = seg(SEG_ML)
    ifc_ref[...] = seg(SEG_IFC) + gb_ref[...]
    ret_ref[...] = seg(SEG_RET)
    sb_ref[...] = seg(SEG_SB).astype(BF16)
    swa_ref[...] = seg(SEG_SWA)
    ift_ref[...] = _dot_nt(wt_ref[...], h) + gbt_ref[:, 0:1]


def _in_proj(x, g, scale, shift, w_perm, w_ift, gb_row, gbt, seq):
    t, d = x.shape
    tm = 512
    tpb = seq // tm
    row = lambda n: pl.BlockSpec((tm, n), lambda i: (i, 0))
    full = lambda a: pl.BlockSpec(a.shape, lambda i: (0,) * a.ndim)
    mod = pl.BlockSpec((1, 1, d), lambda i: (i // tpb, 0, 0))
    widths = [s[1] - s[0] for s in (SEG_ML, SEG_IFC, SEG_RET, SEG_SB, SEG_SWA)]
    dts = [F32, F32, F32, BF16, F32]
    return pl.pallas_call(
        _in_kernel,
        grid=(t // tm,),
        in_specs=[row(d), full(g), mod, mod, full(w_perm), full(w_ift), full(gb_row), full(gbt)],
        out_specs=[row(n) for n in widths] + [pl.BlockSpec((16, tm), lambda i: (0, i))],
        out_shape=[jax.ShapeDtypeStruct((t, n), dt) for n, dt in zip(widths, dts)]
        + [jax.ShapeDtypeStruct((16, t), F32)],
        compiler_params=_params(("parallel",)),
        name="in_proj",
    )(x, g, scale, shift, w_perm, w_ift, gb_row, gbt)


def _ml_kernel(q_ref, k_ref, v_ref, o_ref, ifc_ref, ift_ref, cwq_ref, cwk_ref, cbq_ref, cbk_ref,
               out_ref, halo_q, halo_k, c_st, m_st):
    L = CHUNK
    HALO = 8

    @pl.when(pl.program_id(2) == 0)
    def _init():
        halo_q[0:HALO, :] = jnp.zeros((HALO, LANES), F32)
        halo_k[0:HALO, :] = jnp.zeros((HALO, LANES), F32)
        c_st[...] = jnp.zeros(c_st.shape, F32)
        m_st[...] = jnp.zeros(m_st.shape, F32)

    def conv_silu(raw_ref, halo, w_ref, b_ref):
        halo[HALO:HALO + L, :] = raw_ref[...]
        acc = jnp.zeros((L, LANES), F32) + b_ref[...]
        for j in range(CONV_K):
            off = HALO - (CONV_K - 1) + j
            acc = acc + halo[off:off + L, :] * w_ref[j:j + 1, :]
        halo[0:HALO, :] = halo[L:L + HALO, :]
        return _silu(acc)

    q = conv_silu(q_ref, halo_q, cwq_ref, cbq_ref)
    k = conv_silu(k_ref, halo_k, cwk_ref, cbk_ref)
    v = v_ref[...]
    k_bf = k.astype(BF16)

    lane = _iota((L, LANES), 1)
    row = _iota((L, LANES), 0)
    causal = row >= lane
    tri = jnp.where(causal, 1.0, 0.0).astype(BF16)
    upper = jnp.where(row <= lane, 1.0, 0.0).astype(BF16)

    ifc = ifc_ref[...]
    ift = ift_ref[...]
    lf_h, lf_m, lf_l = _split3(_log_sigmoid(ifc))
    a_cols = _dot(tri, lf_h) + _dot(tri, lf_m) + _dot(tri, lf_l)
    lr_h, lr_m, lr_l = _split3(_log_sigmoid(ift))
    a_rows = _dot(lr_h, upper) + _dot(lr_m, upper) + _dot(lr_l, upper)

    outs = []
    for hh in range(2):
        hm = (lane >= HEAD_DIM) if hh else (lane < HEAD_DIM)
        qh = (jnp.where(hm, q, 0.0) * (HEAD_DIM ** -0.5)).astype(BF16)
        kh = jnp.where(hm, k, 0.0)
        v_aug = jnp.where(hm, v, 1.0).astype(BF16)
        i_col = ifc[:, hh:hh + 1]
        a_col = a_cols[:, 2 + hh:3 + hh]
        i_row = ift[hh:hh + 1, :]
        a_row = a_rows[2 + hh:3 + hh, :]
        a_last = a_rows[2 + hh:3 + hh, L - 1:L]
        m_in = m_st[hh, 0:1, 0:1]
        c_in = c_st[hh]

        d_intra = jnp.where(causal, a_col - (a_row - i_row), NEG)
        m_intra = jnp.max(d_intra, axis=-1, keepdims=True)
        m_inter = a_col + m_in
        m_tot = jnp.maximum(m_inter, m_intra)
        s_inter = jnp.exp(m_inter - m_tot)
        p = jnp.exp(d_intra - m_tot) * _dot_nt(qh, k_bf)
        o_aug = s_inter * _dot(qh, c_in.astype(BF16)) + _dot(p.astype(BF16), v_aug)
        den_lane = HEAD_DIM if hh == 0 else 0
        den = o_aug[:, den_lane:den_lane + 1]
        outs.append(o_aug / jnp.maximum(jnp.abs(den), jnp.exp(-m_tot)))

        g_col = a_last - a_col + i_col
        m_loc = jnp.max(g_col, axis=0, keepdims=True)
        kw = (kh * jnp.exp(g_col - m_loc)).astype(BF16)
        c_loc = _dot_tn(kw, v_aug)
        m_new = jnp.maximum(a_last + m_in, m_loc)
        c_st[hh] = jnp.exp(a_last + m_in - m_new) * c_in + jnp.exp(m_loc - m_new) * c_loc
        m_st[hh] = jnp.zeros((8, LANES), F32) + m_new

    h_ml = jnp.where(lane < HEAD_DIM, outs[0], outs[1])
    out_ref[...] = (h_ml * _sigmoid(o_ref[...])).astype(BF16)


def _mlstm(ml, ifc, ift, conv_w, conv_b, batch, seq):
    nc = seq // CHUNK
    blk = lambda off: pl.BlockSpec((CHUNK, LANES), lambda b, p, c: (b * nc + c, off + p))
    cw = lambda off: pl.BlockSpec((CONV_K, LANES), lambda b, p, c: (0, off + p))
    cb = lambda off: pl.BlockSpec((1, LANES), lambda b, p, c: (0, off + p))
    return pl.pallas_call(
        _ml_kernel,
        grid=(batch, 2, nc),
        in_specs=[blk(0), blk(2), blk(4), blk(6), blk(0),
                  pl.BlockSpec((8, CHUNK), lambda b, p, c: (p, b * nc + c)),
                  cw(0), cw(2), cb(0), cb(2)],
        out_specs=blk(0),
        out_shape=jax.ShapeDtypeStruct((batch * seq, W_BRANCH), BF16),
        scratch_shapes=[pltpu.VMEM((CHUNK + 8, LANES), F32), pltpu.VMEM((CHUNK + 8, LANES), F32),
                        pltpu.VMEM((2, LANES, LANES), F32), pltpu.VMEM((2, 8, LANES), F32)],
        compiler_params=_params(("parallel", "parallel", "arbitrary")),
        name="mlstm",
    )(ml, ml, ml, ml, ifc, ift, conv_w, conv_w, conv_b, conv_b)


def _ret_kernel(q_ref, k_ref, v_ref, g_ref, cos_ref, sin_ref, lg_ref, ng_ref, out_ref, st):
    L = CHUNK

    @pl.when(pl.program_id(2) == 0)
    def _init():
        st[...] = jnp.zeros(st.shape, F32)

    lane = _iota((L, LANES), 1)
    row = _iota((L, LANES), 0)
    cos = cos_ref[...]
    sin = sin_ref[...]
    first = (lane % HEAD_DIM) < (HEAD_DIM // 2)

    def rot(t):
        partner = jnp.where(first, -pltpu.roll(t, LANES - HEAD_DIM // 2, 1), pltpu.roll(t, HEAD_DIM // 2, 1))
        return t * cos + partner * sin

    lg = lg_ref[0, 0:1, :]
    rowf = row.astype(F32)
    q = rot(q_ref[...])
    k = rot(k_ref[...]) * (HEAD_DIM ** -0.5)
    v_bf = v_ref[...].astype(BF16)
    q_bf = q.astype(BF16)
    k_bf = k.astype(BF16)
    state = st[...]

    relf = (row - lane).astype(F32)
    intra = []
    for hh in range(2):
        hm = (lane >= HEAD_DIM) if hh else (lane < HEAD_DIM)
        lg_h = lg[:, hh * HEAD_DIM:hh * HEAD_DIM + 1]
        decay = jnp.where(row >= lane, jnp.exp(jnp.maximum(relf, 0.0) * lg_h), 0.0)
        qh = jnp.where(hm, q, 0.0).astype(BF16)
        sc = _dot_nt(qh, k_bf) * decay
        intra.append(_dot(sc.astype(BF16), v_bf))
    o = jnp.where(lane < HEAD_DIM, intra[0], intra[1])
    o = o + _dot(q_bf, state.astype(BF16)) * jnp.exp((rowf + 1.0) * lg)

    same_head = (row // HEAD_DIM) == (lane // HEAD_DIM)
    kd = (k * jnp.exp((L - 1.0 - rowf) * lg)).astype(BF16)
    st[...] = jnp.exp(float(L) * lg) * state + jnp.where(same_head, _dot_tn(kd, v_bf), 0.0)

    y = o * lax.rsqrt(_head_mean_sq(o) + EPS) * ng_ref[...]
    out_ref[...] = (y * _silu(g_ref[...])).astype(BF16)


def _retention(ret, cos_t, sin_t, lg_tab, ret_g, batch, seq):
    nc = seq // CHUNK
    blk = lambda off: pl.BlockSpec((CHUNK, LANES), lambda b, p, c: (b * nc + c, off + p))
    tab = pl.BlockSpec((CHUNK, LANES), lambda b, p, c: (b * nc + c, 0))
    return pl.pallas_call(
        _ret_kernel,
        grid=(batch, 2, nc),
        in_specs=[blk(0), blk(2), blk(4), blk(6), tab, tab,
                  pl.BlockSpec((1, 8, LANES), lambda b, p, c: (p, 0, 0)),
                  pl.BlockSpec((1, LANES), lambda b, p, c: (0, p))],
        out_specs=blk(0),
        out_shape=jax.ShapeDtypeStruct((batch * seq, W_BRANCH), BF16),
        scratch_shapes=[pltpu.VMEM((LANES, LANES), F32)],
        compiler_params=_params(("parallel", "parallel", "arbitrary")),
        name="retention",
    )(ret, ret, ret, ret, cos_t, sin_t, lg_tab, ret_g)


def _sb_kernel(q_ref, k_ref, v_ref, out_ref):
    L = CHUNK
    i = pl.program_id(2)
    lane = _iota((L, LANES), 1)
    row = _iota((L, LANES), 0)
    q = q_ref[...] * (HEAD_DIM ** -0.5)
    qh = [jnp.where(lane < HEAD_DIM, q, 0.0).astype(BF16), jnp.where(lane >= HEAD_DIM, q, 0.0).astype(BF16)]

    r2 = _iota((2 * L, 2 * L), 0) % L
    c2 = _iota((2 * L, 2 * L), 1)
    suffix_op = jnp.where((c2 >= L) | (r2 >= c2), 1.0, 0.0).astype(BF16)

    def tile(z, mask, carry, acc, v_blk):
        lk = -(jnp.maximum(z, 0.0) + jnp.log(1.0 + jnp.exp(-jnp.abs(z))))
        if mask is not None:
            lk = jnp.where(mask, lk, 0.0)
        hi, lo = _split2(lk)
        rr = _dot(jnp.concatenate([hi, lo], axis=1), suffix_op)
        w = jnp.exp(z + rr[:, :L] + carry)
        if mask is not None:
            w = jnp.where(mask, w, 0.0)
        return carry + rr[:, L:], acc + _dot(w.astype(BF16), v_blk)

    off = pl.multiple_of(i * L, L)
    k_blk = k_ref[pl.ds(off, L), :]
    v_blk = v_ref[pl.ds(off, L), :]
    strict = lane < row
    zero = jnp.zeros((L, LANES), F32)
    state = []
    for hh in range(2):
        state.extend(tile(_dot_nt(qh[hh], k_blk), strict, zero, zero, v_blk))

    def cond(st):
        return jnp.logical_and(st[0] >= 0, st[1] == 0)

    def body(st):
        j = st[0]
        o = pl.multiple_of(j * L, L)
        kb = k_ref[pl.ds(o, L), :]
        vb = v_ref[pl.ds(o, L), :]
        new = []
        for hh in range(2):
            new.extend(tile(_dot_nt(qh[hh], kb), None, st[2 + 2 * hh], st[3 + 2 * hh], vb))
        done = (jnp.max(jnp.maximum(new[0], new[2])) <= F32_EXP_UNDERFLOW).astype(jnp.int32)
        return (j - 1, done) + tuple(new)

    st = lax.while_loop(cond, body, (i - 1, jnp.int32(0)) + tuple(state))
    out_ref[...] = jnp.where(lane < HEAD_DIM, st[3], st[5]).astype(BF16)


def _stick_breaking(sb, batch, seq):
    nb = seq // CHUNK
    kv = lambda off: pl.BlockSpec((seq, LANES), lambda b, p, i: (b, off + p))
    return pl.pallas_call(
        _sb_kernel,
        grid=(batch, 2, nb),
        in_specs=[pl.BlockSpec((CHUNK, LANES), lambda b, p, i: (b * nb + i, p)), kv(2), kv(4)],
        out_specs=pl.BlockSpec((CHUNK, LANES), lambda b, p, i: (b * nb + i, p)),
        out_shape=jax.ShapeDtypeStruct((batch * seq, W_BRANCH), BF16),
        compiler_params=_params(("parallel", "parallel", "arbitrary")),
        name="stick_breaking",
    )(sb, sb, sb)


def _swa_kernel(sink_ref, q_ref, kc_ref, kp_ref, vc_ref, vp_ref, bias_ref, qg_ref, kg_ref, out_ref, *, nb):
    L = CHUNK
    i = pl.program_id(0) % nb
    lane = _iota((L, LANES), 1)
    t = _iota((L, 2 * L), 0)
    j = _iota((L, 2 * L), 1)
    dist = t + L - j
    valid = (dist >= 0) & (dist < L) & ((j >= L) | (i > 0))

    def qk_norm(x, g_ref):
        return x * lax.rsqrt(_head_mean_sq(x) + EPS) * g_ref[...]

    for g in range(2):
        sl = slice(g * LANES, (g + 1) * LANES)
        qn = qk_norm(q_ref[:, sl], qg_ref) * (HEAD_DIM ** -0.5)
        kcat = jnp.concatenate([qk_norm(kp_ref[:, sl], kg_ref), qk_norm(kc_ref[:, sl], kg_ref)], axis=0).astype(BF16)
        vcat = jnp.concatenate([vp_ref[:, sl], vc_ref[:, sl]], axis=0).astype(BF16)
        outs = []
        for r in range(2):
            hq = 2 * g + r
            hm = (lane >= HEAD_DIM) if r else (lane < HEAD_DIM)
            qh = jnp.where(hm, qn, 0.0).astype(BF16)
            logits = jnp.where(valid, _dot_nt(qh, kcat) + bias_ref[0, hq], NEG)
            sink = sink_ref[hq]
            m = jnp.maximum(jnp.max(logits, axis=-1, keepdims=True), sink)
            p = jnp.exp(logits - m)
            den = jnp.sum(p, axis=-1, keepdims=True) + jnp.exp(sink - m)
            outs.append(_dot((p / den).astype(BF16), vcat))
        out_ref[:, sl] = jnp.where(lane < HEAD_DIM, outs[0], outs[1]).astype(BF16)


def _swa(swa, bias, sinks, q_g, k_g, batch, seq):
    nb = seq // CHUNK
    cur = lambda off: pl.BlockSpec((CHUNK, 2 * LANES), lambda i: (i, off))
    prev = lambda off: pl.BlockSpec((CHUNK, 2 * LANES), lambda i: (jnp.maximum(i - 1, 0), off))
    gain = pl.BlockSpec((1, LANES), lambda i: (0, 0))
    return pl.pallas_call(
        functools.partial(_swa_kernel, nb=nb),
        grid=(batch * nb,),
        in_specs=[pl.BlockSpec(memory_space=pltpu.SMEM), cur(0), cur(1), prev(1), cur(2), prev(2),
                  pl.BlockSpec((1, N_HEADS, CHUNK, 2 * CHUNK), lambda i: (i, 0, 0, 0)), gain, gain],
        out_specs=pl.BlockSpec((CHUNK, W_BRANCH), lambda i: (i, 0)),
        out_shape=jax.ShapeDtypeStruct((batch * seq, W_BRANCH), BF16),
        compiler_params=_params(("parallel",)),
        name="swa",
    )(sinks, swa, swa, swa, swa, swa, bias, q_g, k_g)


def _merge_kernel(x_ref, g_ref, sc_ref, sh_ref, gt_ref, b0_ref, b1_ref, b2_ref, b3_ref,
                  wg_ref, wu_ref, wo_ref, out_ref):
    x = x_ref[...]
    h = _norm_mod(x, g_ref[...], sc_ref[0], sh_ref[0]).astype(BF16)
    merged = None
    for n, b_ref in enumerate((b0_ref, b1_ref, b2_ref, b3_ref)):
        gate = _sigmoid(_dot(h, wg_ref[:, n * D_MODEL:(n + 1) * D_MODEL]))
        term = gate * _dot(b_ref[...], wu_ref[n])
        merged = term if merged is None else merged + term
    mix = _dot(merged.astype(BF16), wo_ref[...])
    out_ref[...] = x + gt_ref[0] * mix


def _merge(x, g, scale, shift, gate, branches, w_gate, w_up, w_out, seq):
    t, d = x.shape
    tm = 256
    tpb = seq // tm
    row = lambda n: pl.BlockSpec((tm, n), lambda i: (i, 0))
    full = lambda a: pl.BlockSpec(a.shape, lambda i: (0,) * a.ndim)
    mod = pl.BlockSpec((1, 1, d), lambda i: (i // tpb, 0, 0))
    return pl.pallas_call(
        _merge_kernel,
        grid=(t // tm,),
        in_specs=[row(d), full(g), mod, mod, mod] + [row(W_BRANCH)] * 4 + [full(w_gate), full(w_up), full(w_out)],
        out_specs=row(d),
        out_shape=jax.ShapeDtypeStruct((t, d), F32),
        compiler_params=_params(("parallel",)),
        name="merge",
    )(x, g, scale, shift, gate, *branches, w_gate, w_up, w_out)


def _ffn_kernel(x_ref, g_ref, sc_ref, sh_ref, gt_ref, w1_ref, w2_ref, out_ref):
    x = x_ref[...]
    h = _norm_mod(x, g_ref[...], sc_ref[0], sh_ref[0]).astype(BF16)
    tf = 1024
    acc = None
    for n in range(D_FF // tf):
        a = jnp.maximum(_dot(h, w1_ref[:, n * tf:(n + 1) * tf]), 0.0)
        part = _dot((a * a).astype(BF16), w2_ref[n * tf:(n + 1) * tf, :])
        acc = part if acc is None else acc + part
    out_ref[...] = x + gt_ref[0] * acc


def _ffn(x, g, scale, shift, gate, w1, w2, seq):
    t, d = x.shape
    tm = 256
    tpb = seq // tm
    row = pl.BlockSpec((tm, d), lambda i: (i, 0))
    full = lambda a: pl.BlockSpec(a.shape, lambda i: (0,) * a.ndim)
    mod = pl.BlockSpec((1, 1, d), lambda i: (i // tpb, 0, 0))
    return pl.pallas_call(
        _ffn_kernel,
        grid=(t // tm,),
        in_specs=[row, full(g), mod, mod, mod, full(w1), full(w2)],
        out_specs=row,
        out_shape=jax.ShapeDtypeStruct((t, d), F32),
        compiler_params=_params(("parallel",)),
        name="ffn",
    )(x, g, scale, shift, gate, w1, w2)


def _permute_w_in(w):
    d = w.shape[0]
    hd = HEAD_DIM
    o_ret = 3 * W_BRANCH + 2 * N_HEADS + W_BRANCH
    o_sb = o_ret + 4 * W_BRANCH
    o_swa = o_sb + 3 * W_BRANCH
    mi = w[:, 768:772]
    mf = w[:, 772:776]
    zpad = jnp.zeros((d, LANES - 4), w.dtype)
    ifc = jnp.concatenate([mi[:, 0:2], mf[:, 0:2], zpad, mi[:, 2:4], mf[:, 2:4], zpad], axis=1)
    aq = w[:, o_swa:o_swa + 256]
    ak = w[:, o_swa + 256:o_swa + 384]
    av = w[:, o_swa + 384:o_swa + 512]
    dup = lambda a: jnp.concatenate([a[:, :hd], a[:, :hd], a[:, hd:], a[:, hd:]], axis=1)
    w_perm = jnp.concatenate([w[:, 0:768], w[:, 776:1032], ifc, w[:, o_ret:o_sb], w[:, o_sb:o_swa],
                              aq, dup(ak), dup(av)], axis=1)
    z4 = jnp.zeros((d, 4), w.dtype)
    ift = jnp.concatenate([mi[:, 0:2], mf[:, 0:2], z4, mi[:, 2:4], mf[:, 2:4], z4], axis=1).T
    return w_perm.astype(BF16), ift.astype(BF16)


def _gate_bias_layout(gate_b):
    ib, fb = gate_b[0], gate_b[1]
    z = jnp.zeros((LANES - 4,), F32)
    row = jnp.concatenate([ib[0:2], fb[0:2], z, ib[2:4], fb[2:4], z]).reshape(1, 2 * LANES)
    z4 = jnp.zeros((4,), F32)
    col = jnp.concatenate([ib[0:2], fb[0:2], z4, ib[2:4], fb[2:4], z4])
    return row, jnp.broadcast_to(col[:, None], (16, LANES))


def kernel(x, c, positions, w_ada, b_ada, norm_g, w_in, mlstm_conv_w, mlstm_conv_b, mlstm_gate_b,
           ret_norm_g, swa_q_norm_g, swa_k_norm_g, swa_sinks, rel_bias, w_up, w_out, w_ff1, w_ff2):
    batch, seq, d = x.shape
    depth = w_in.shape[0]
    t = batch * seq
    nb = seq // CHUNK

    c8 = jnp.concatenate([c, jnp.zeros((8 - batch, d), F32)], axis=0)
    mod = _ada(c8, w_ada.reshape(depth * 2, d, 3 * d), b_ada.reshape(depth * 2, 1, 3 * d))
    mod = mod[:, :batch].reshape(depth, 2, batch, 3, 1, d)

    pos_col = jnp.broadcast_to(positions.reshape(t, 1), (t, LANES))
    pos_row = positions.reshape(batch * nb, 1, CHUNK)
    half = HEAD_DIM // 2
    inv = ROPE_BASE ** (-(np.arange(LANES) % half).astype(np.float64) / half)
    cos_t, sin_t = _rope_tables(pos_col, jnp.asarray(inv, F32).reshape(1, LANES))
    bias = _bias_tiles(rel_bias, pos_col, pos_row, batch, nb)

    log_gamma = np.log(1.0 - np.exp2(-(RET_DECAY_BASE + np.arange(N_HEADS, dtype=np.float64))))
    lg_tab = jnp.asarray(np.broadcast_to(np.repeat(log_gamma, HEAD_DIM).reshape(2, 1, LANES), (2, 8, LANES)), F32)

    xt = x.reshape(t, d)
    for l in range(depth):
        w_perm, w_ift = _permute_w_in(w_in[l])
        gb_row, gbt = _gate_bias_layout(mlstm_gate_b[l])
        g1 = norm_g[l, 0].reshape(1, d)
        g2 = norm_g[l, 1].reshape(1, d)
        shift1, scale1, gate1 = mod[l, 0, :, 0], mod[l, 0, :, 1], mod[l, 0, :, 2]
        shift2, scale2, gate2 = mod[l, 1, :, 0], mod[l, 1, :, 1], mod[l, 1, :, 2]

        ml, ifc, ret, sb, swa, ift = _in_proj(xt, g1, scale1, shift1, w_perm, w_ift, gb_row, gbt, seq)
        out_ml = _mlstm(ml, ifc, ift, mlstm_conv_w[l], mlstm_conv_b[l].reshape(1, 2 * W_BRANCH), batch, seq)
        out_ret = _retention(ret, cos_t, sin_t, lg_tab, ret_norm_g[l].reshape(1, W_BRANCH), batch, seq)
        out_sb = _stick_breaking(sb, batch, seq)
        qg = jnp.tile(swa_q_norm_g[l], 2).reshape(1, LANES)
        kg = jnp.tile(swa_k_norm_g[l], 2).reshape(1, LANES)
        out_swa = _swa(swa, bias, swa_sinks[l], qg, kg, batch, seq)

        w_gate = w_in[l][:, w_in.shape[2] - N_HEADS * d:].astype(BF16)
        xt = _merge(xt, g1, scale1, shift1, gate1, (out_ml, out_ret, out_sb, out_swa),
                    w_gate, w_up[l].astype(BF16), w_out[l].astype(BF16), seq)
        xt = _ffn(xt, g2, scale2, shift2, gate2, w_ff1[l].astype(BF16), w_ff2[l].astype(BF16), seq)
    return xt.reshape(batch, seq, d)
```

```python
import functools
import math

import numpy as np
import jax
import jax.numpy as jnp
from jax import lax
from jax.experimental import pallas as pl
from jax.experimental.pallas import tpu as pltpu

F32 = jnp.float32
BF16 = jnp.bfloat16

D_MODEL = 1024
HEAD_DIM = 64
N_HEADS = 4
W_BRANCH = N_HEADS * HEAD_DIM
LANES = 128
CHUNK = 128
CONV_K = 4
D_FF = 4 * D_MODEL
N_BUCKETS = 32
MAX_DIST = 128
ROPE_BASE = 10000.0
RET_DECAY_BASE = 5.0
EPS = 1e-6
NEG = -1e30
F32_EXP_UNDERFLOW = -104.0
VMEM_LIMIT = 56 * 1024 * 1024

SEG_ML = (0, 1024)
SEG_IFC = (1024, 1280)
SEG_RET = (1280, 2304)
SEG_SB = (2304, 3072)
SEG_SWA = (3072, 3840)
N_IN = 3840


def _dot(a, b):
    return jnp.dot(a, b, preferred_element_type=F32)


def _dot_nt(a, b):
    return lax.dot_general(a, b, (((1,), (1,)), ((), ())), preferred_element_type=F32)


def _dot_tn(a, b):
    return lax.dot_general(a, b, (((0,), (0,)), ((), ())), preferred_element_type=F32)


def _split2(x):
    hi = x.astype(BF16)
    lo = (x - hi.astype(F32)).astype(BF16)
    return hi, lo


def _split3(x):
    hi = x.astype(BF16)
    r = x - hi.astype(F32)
    mid = r.astype(BF16)
    lo = (r - mid.astype(F32)).astype(BF16)
    return hi, mid, lo


def _iota(shape, axis):
    return lax.broadcasted_iota(jnp.int32, shape, axis)


def _log_sigmoid(x):
    return jnp.minimum(x, 0.0) - jnp.log(1.0 + jnp.exp(-jnp.abs(x)))


def _sigmoid(x):
    return 1.0 / (1.0 + jnp.exp(-x))


def _silu(x):
    return x * _sigmoid(x)


def _norm_mod(x, g, scale, shift):
    ms = jnp.mean(x * x, axis=-1, keepdims=True)
    y = x * lax.rsqrt(ms + EPS)
    return (y * g) * (1.0 + scale) + shift


def _head_mean_sq(x):
    lane = _iota(x.shape, 1)
    sq = x * x
    s0 = jnp.sum(jnp.where(lane < HEAD_DIM, sq, 0.0), axis=-1, keepdims=True)
    s1 = jnp.sum(jnp.where(lane >= HEAD_DIM, sq, 0.0), axis=-1, keepdims=True)
    return jnp.where(lane < HEAD_DIM, s0, s1) * (1.0 / HEAD_DIM)


def _params(sem):
    return pltpu.CompilerParams(dimension_semantics=sem, vmem_limit_bytes=VMEM_LIMIT)


def _interleave(chains):
    live = list(chains)
    while live:
        still = []
        for ch in live:
            try:
                next(ch)
                still.append(ch)
            except StopIteration:
                pass
        live = still


def _resident(a):
    return pl.BlockSpec(a.shape, lambda *_: (0,) * a.ndim, pipeline_mode=pl.Buffered(1))


def _ada_kernel(c_ref, w_ref, b_ref, o_ref):
    c = c_ref[...]
    ch, cl = _split2(_silu(c))
    wh, wl = _split2(w_ref[0])
    o_ref[0] = _dot(ch, wh) + _dot(ch, wl) + _dot(cl, wh) + b_ref[0]


def _ada(c8, w_ada, b_ada):
    n_mod, d, n3 = w_ada.shape
    tn = 1024
    return pl.pallas_call(
        _ada_kernel,
        grid=(n_mod, n3 // tn),
        in_specs=[pl.BlockSpec((8, d), lambda m, n: (0, 0)),
                  pl.BlockSpec((1, d, tn), lambda m, n: (m, 0, n)),
                  pl.BlockSpec((1, 1, tn), lambda m, n: (m, 0, n))],
        out_specs=pl.BlockSpec((1, 8, tn), lambda m, n: (m, 0, n)),
        out_shape=jax.ShapeDtypeStruct((n_mod, 8, n3), F32),
        compiler_params=_params(("parallel", "parallel")),
        name="ada_mod",
    )(c8, w_ada, b_ada)


def _rope_kernel(pos_ref, inv_ref, cos_ref, sin_ref):
    ang = pos_ref[...].astype(F32) * inv_ref[...]
    cos_ref[...] = jnp.cos(ang)
    sin_ref[...] = jnp.sin(ang)


def _rope_tables(pos_b, inv_row):
    t = pos_b.shape[0]
    tm = 1024
    spec = pl.BlockSpec((tm, LANES), lambda i: (i, 0))
    return pl.pallas_call(
        _rope_kernel,
        grid=(t // tm,),
        in_specs=[spec, pl.BlockSpec((1, LANES), lambda i: (0, 0))],
        out_specs=[spec, spec],
        out_shape=[jax.ShapeDtypeStruct((t, LANES), F32)] * 2,
        compiler_params=_params(("parallel",)),
        name="rope_tables",
    )(pos_b, inv_row)


def _bias_kernel(tab_ref, posq_ref, posc_ref, posp_ref, o_ref):
    pos_q = posq_ref[...]
    rel = jnp.concatenate([pos_q - posp_ref[0], pos_q - posc_ref[0]], axis=1)
    n = jnp.maximum(rel, 0)
    max_exact = N_BUCKETS // 2
    nf = jnp.maximum(n, 1).astype(F32)
    large = max_exact + (jnp.log(nf / max_exact) / math.log(MAX_DIST / max_exact)
                         * (N_BUCKETS - max_exact)).astype(jnp.int32)
    large = jnp.minimum(large, N_BUCKETS - 1)
    bucket = jnp.where(n < max_exact, n, large)
    acc = [jnp.zeros(rel.shape, F32) for _ in range(N_HEADS)]
    for b in range(N_BUCKETS):
        hit = bucket == b
        for h in range(N_HEADS):
            acc[h] = jnp.where(hit, tab_ref[b, h], acc[h])
    for h in range(N_HEADS):
        o_ref[0, h] = acc[h]


def _bias_tiles(rel_bias, pos_b, pos_r, batch, n_b):
    nblk = batch * n_b
    return pl.pallas_call(
        _bias_kernel,
        grid=(nblk,),
        in_specs=[pl.BlockSpec(memory_space=pltpu.SMEM),
                  pl.BlockSpec((CHUNK, LANES), lambda i: (i, 0)),
                  pl.BlockSpec((1, 1, CHUNK), lambda i: (i, 0, 0)),
                  pl.BlockSpec((1, 1, CHUNK), lambda i: (jnp.maximum(i - 1, 0), 0, 0))],
        out_specs=pl.BlockSpec((1, N_HEADS, CHUNK, 2 * CHUNK), lambda i: (i, 0, 0, 0)),
        out_shape=jax.ShapeDtypeStruct((nblk, N_HEADS, CHUNK, 2 * CHUNK), F32),
        compiler_params=_params(("parallel",)),
        name="t5_bias",
    )(rel_bias, pos_b, pos_r, pos_r)


def _in_kernel(x_ref, g_ref, sc_ref, sh_ref, w_ref, wt_ref, gb_ref, gbt_ref,
               ml_ref, ifc_ref, ret_ref, sb_ref, swa_ref, ift_ref):
    h = _norm_mod(x_ref[...], g_ref[...], sc_ref[0], sh_ref[0]).astype(BF16)

    def seg(s):
        return _dot(h, w_ref[:, s[0]:s[1]])

    ml_ref[...] = seg(SEG_ML)
    ifc_ref[...] = seg(SEG_IFC) + gb_ref[...]
    ret_ref[...] = seg(SEG_RET)
    sb_ref[...] = seg(SEG_SB).astype(BF16)
    swa_ref[...] = seg(SEG_SWA)
    ift_ref[0] = _dot_nt(wt_ref[...], h) + gbt_ref[:, 0:1]


def _in_proj(x, g, scale, shift, w_perm, w_ift, gb_row, gbt, batch, seq):
    t, d = x.shape
    tm = 512
    tpb = seq // tm
    row = lambda n: pl.BlockSpec((tm, n), lambda i: (i, 0))
    mod = pl.BlockSpec((1, 1, d), lambda i: (i // tpb, 0, 0))
    widths = [s[1] - s[0] for s in (SEG_ML, SEG_IFC, SEG_RET, SEG_SB, SEG_SWA)]
    dts = [F32, F32, F32, BF16, F32]
    return pl.pallas_call(
        _in_kernel,
        grid=(t // tm,),
        in_specs=[row(d), _resident(g), mod, mod, _resident(w_perm), _resident(w_ift), _resident(gb_row),
                  _resident(gbt)],
        out_specs=[row(n) for n in widths] + [pl.BlockSpec((1, 16, tm), lambda i: (i // tpb, 0, i % tpb))],
        out_shape=[jax.ShapeDtypeStruct((t, n), dt) for n, dt in zip(widths, dts)]
        + [jax.ShapeDtypeStruct((batch, 16, seq), F32)],
        compiler_params=_params(("parallel",)),
        name="in_proj",
    )(x, g, scale, shift, w_perm, w_ift, gb_row, gbt)


def _ml_kernel(ml_ref, ifc_ref, ift_ref, cw_ref, cb_ref, out_ref, halo, c_st, m_st):
    L = CHUNK
    HALO = 8
    batch = ml_ref.shape[0]

    @pl.when(pl.program_id(0) == 0)
    def _init():
        halo[:, 0:HALO, :] = jnp.zeros((halo.shape[0], HALO, LANES), F32)
        c_st[...] = jnp.zeros(c_st.shape, F32)
        m_st[...] = jnp.zeros(m_st.shape, F32)

    lane = _iota((L, LANES), 1)
    row = _iota((L, LANES), 0)
    causal = row >= lane
    tri = jnp.where(causal, 1.0, 0.0).astype(BF16)
    upper = jnp.where(row <= lane, 1.0, 0.0).astype(BF16)
    eye = jnp.where(row == lane, 1.0, 0.0).astype(BF16)
    spread = jnp.where(_iota((LANES, 4 * LANES), 0) == _iota((LANES, 4 * LANES), 1) // LANES,
                       1.0, 0.0).astype(BF16)

    def conv_silu(b, col, slot):
        hl = halo.at[slot]
        hl[HALO:HALO + L, :] = ml_ref[b, :, col:col + LANES]
        acc = jnp.zeros((L, LANES), F32) + cb_ref[:, col:col + LANES]
        for j in range(CONV_K):
            off = HALO - (CONV_K - 1) + j
            acc = acc + hl[off:off + L, :] * cw_ref[j:j + 1, col:col + LANES]
        hl[0:HALO, :] = hl[L:L + HALO, :]
        return _silu(acc)

    def pair_chain(b, p):
        pc = p * LANES
        ift = ift_ref[b, p * 8:p * 8 + 8, :]
        lr_h, lr_m, lr_l = _split3(_log_sigmoid(ift))
        a_rows = _dot(lr_h, upper) + _dot(lr_m, upper) + _dot(lr_l, upper)
        g_h, g_m, g_l = _split3(ifc_ref[b, :, pc:pc + LANES])
        gates_b = _dot(g_h, spread) + _dot(g_m, spread) + _dot(g_l, spread)
        q = conv_silu(b, pc, (b * 2 + p) * 2)
        k = conv_silu(b, W_BRANCH + pc, (b * 2 + p) * 2 + 1)
        v = ml_ref[b, :, 2 * W_BRANCH + pc:2 * W_BRANCH + pc + LANES]
        k_bf = k.astype(BF16)
        yield

        lf_h, lf_m, lf_l = _split3(_log_sigmoid(gates_b[:, 2 * LANES:]))
        a_b = _dot(tri, lf_h) + _dot(tri, lf_m) + _dot(tri, lf_l)
        heads = []
        for hh in range(2):
            ch = (b * 2 + p) * 2 + hh
            hm = (lane >= HEAD_DIM) if hh else (lane < HEAD_DIM)
            qh = (jnp.where(hm, q, 0.0) * (HEAD_DIM ** -0.5)).astype(BF16)
            c_in = c_st[ch]
            heads.append(dict(ch=ch, hm=hm, c_in=c_in, s_qk=_dot_nt(qh, k_bf), q_c=_dot(qh, c_in.astype(BF16))))
        yield

        for hh, hd in enumerate(heads):
            i_col = gates_b[:, hh * LANES:(hh + 1) * LANES]
            a_col = a_b[:, hh * LANES:(hh + 1) * LANES]
            a_last = a_col[L - 1:L, :]
            m_in = m_st[hd["ch"], 0:1, :]
            v_aug = jnp.where(hd["hm"], v, 1.0).astype(BF16)

            d_intra = jnp.where(causal, a_col - (a_rows[2 + hh:3 + hh, :] - ift[hh:hh + 1, :]), NEG)
            m_intra = jnp.max(d_intra, axis=-1, keepdims=True)
            m_inter = a_col + m_in
            m_tot = jnp.maximum(m_inter, m_intra)
            pm = jnp.exp(d_intra - m_tot) * hd["s_qk"]
            hd["p_v"] = _dot(pm.astype(BF16), v_aug)
            hd["s_inter"] = jnp.exp(m_inter - m_tot)
            hd["floor"] = jnp.exp(-m_tot)

            g_col = a_last - a_col + i_col
            m_loc = jnp.max(g_col, axis=0, keepdims=True)
            kw = (jnp.where(hd["hm"], k, 0.0) * jnp.exp(g_col - m_loc)).astype(BF16)
            hd["kw_t"] = _dot_nt(eye, kw)
            hd["v_aug"] = v_aug
            m_new = jnp.maximum(a_last + m_in, m_loc)
            hd["w_old"] = jnp.exp(a_last + m_in - m_new)
            hd["w_new"] = jnp.exp(m_loc - m_new)
            m_st[hd["ch"]] = jnp.zeros((8, LANES), F32) + m_new
        yield

        for hd in heads:
            hd["c_loc"] = _dot(hd["kw_t"].astype(BF16), hd["v_aug"])
            hd["o_aug"] = hd["s_inter"] * hd["q_c"] + hd["p_v"]
        den = pltpu.roll(jnp.where(lane < HEAD_DIM, heads[1]["o_aug"], heads[0]["o_aug"]), HEAD_DIM, 1)
        num = jnp.where(lane < HEAD_DIM, heads[0]["o_aug"], heads[1]["o_aug"])
        floor = jnp.where(lane < HEAD_DIM, heads[0]["floor"], heads[1]["floor"])
        h_ml = num / jnp.maximum(jnp.abs(den), floor)
        o_gate = ml_ref[b, :, 3 * W_BRANCH + pc:3 * W_BRANCH + pc + LANES]
        out_ref[b, :, pc:pc + LANES] = (h_ml * _sigmoid(o_gate)).astype(BF16)
        yield

        for hd in heads:
            c_st[hd["ch"]] = hd["w_old"] * hd["c_in"] + hd["w_new"] * hd["c_loc"]

    _interleave(pair_chain(b, p) for b in range(batch) for p in range(2))


def _mlstm(ml, ifc, ift, conv_w, conv_b, batch, seq):
    nc = seq // CHUNK
    n_chain = batch * N_HEADS
    step = lambda n: pl.BlockSpec((batch, CHUNK, n), lambda c: (0, c, 0))
    return pl.pallas_call(
        _ml_kernel,
        grid=(nc,),
        in_specs=[step(4 * W_BRANCH), step(2 * LANES), pl.BlockSpec((batch, 16, CHUNK), lambda c: (0, 0, c)),
                  _resident(conv_w), _resident(conv_b)],
        out_specs=step(W_BRANCH),
        out_shape=jax.ShapeDtypeStruct((batch, seq, W_BRANCH), BF16),
        scratch_shapes=[pltpu.VMEM((n_chain, CHUNK + 8, LANES), F32),
                        pltpu.VMEM((n_chain, LANES, LANES), F32), pltpu.VMEM((n_chain, 8, LANES), F32)],
        compiler_params=_params(("arbitrary",)),
        name="mlstm",
    )(ml.reshape(batch, seq, -1), ifc.reshape(batch, seq, -1), ift, conv_w, conv_b)


def _ret_kernel(ret_ref, cos_ref, sin_ref, lg_ref, ng_ref, out_ref, st):
    L = CHUNK
    batch = ret_ref.shape[0]

    @pl.when(pl.program_id(0) == 0)
    def _init():
        st[...] = jnp.zeros(st.shape, F32)

    lane = _iota((L, LANES), 1)
    row = _iota((L, LANES), 0)
    rowf = row.astype(F32)
    relf = jnp.maximum(row - lane, 0).astype(F32)
    first = (lane % HEAD_DIM) < (HEAD_DIM // 2)
    same_head = (row // HEAD_DIM) == (lane // HEAD_DIM)

    def rot(t, cos, sin):
        partner = jnp.where(first, -pltpu.roll(t, LANES - HEAD_DIM // 2, 1), pltpu.roll(t, HEAD_DIM // 2, 1))
        return t * cos + partner * sin

    decays = []
    for p in range(2):
        lg = lg_ref[p, 0:1, :]
        decays.append(dict(
            q=jnp.exp((rowf + 1.0) * lg), k=jnp.exp((L - 1.0 - rowf) * lg), c=jnp.exp(float(L) * lg),
            intra=[jnp.where(row >= lane, jnp.exp(relf * lg[:, hh * HEAD_DIM:hh * HEAD_DIM + 1]), 0.0)
                   for hh in range(2)]))

    def pair_chain(b, p):
        pc = p * LANES
        dec = decays[p]
        cos = cos_ref[b]
        sin = sin_ref[b]
        q = rot(ret_ref[b, :, pc:pc + LANES], cos, sin)
        k = rot(ret_ref[b, :, W_BRANCH + pc:W_BRANCH + pc + LANES], cos, sin) * (HEAD_DIM ** -0.5)
        v_bf = ret_ref[b, :, 2 * W_BRANCH + pc:2 * W_BRANCH + pc + LANES].astype(BF16)
        k_bf = k.astype(BF16)
        state = st[b * 2 + p]
        s_qk = [_dot_nt(jnp.where((lane >= HEAD_DIM) if hh else (lane < HEAD_DIM), q, 0.0).astype(BF16), k_bf)
                for hh in range(2)]
        inter = _dot(q.astype(BF16), state.astype(BF16))
        kv = _dot_tn((k * dec["k"]).astype(BF16), v_bf)
        yield

        intra = [_dot((s_qk[hh] * dec["intra"][hh]).astype(BF16), v_bf) for hh in range(2)]
        st[b * 2 + p] = dec["c"] * state + jnp.where(same_head, kv, 0.0)
        yield

        o = jnp.where(lane < HEAD_DIM, intra[0], intra[1]) + inter * dec["q"]
        y = o * lax.rsqrt(_head_mean_sq(o) + EPS) * ng_ref[:, pc:pc + LANES]
        gate = ret_ref[b, :, 3 * W_BRANCH + pc:3 * W_BRANCH + pc + LANES]
        out_ref[b, :, pc:pc + LANES] = (y * _silu(gate)).astype(BF16)

    _interleave(pair_chain(b, p) for b in range(batch) for p in range(2))


def _retention(ret, cos_t, sin_t, lg_tab, ret_g, batch, seq):
    nc = seq // CHUNK
    step = lambda n: pl.BlockSpec((batch, CHUNK, n), lambda c: (0, c, 0))
    return pl.pallas_call(
        _ret_kernel,
        grid=(nc,),
        in_specs=[step(4 * W_BRANCH), step(LANES), step(LANES), _resident(lg_tab), _resident(ret_g)],
        out_specs=step(W_BRANCH),
        out_shape=jax.ShapeDtypeStruct((batch, seq, W_BRANCH), BF16),
        scratch_shapes=[pltpu.VMEM((batch * 2, LANES, LANES), F32)],
        compiler_params=_params(("arbitrary",)),
        name="retention",
    )(ret.reshape(batch, seq, -1), cos_t.reshape(batch, seq, LANES), sin_t.reshape(batch, seq, LANES), lg_tab, ret_g)


def _sb_kernel(q_ref, k_ref, v_ref, out_ref, qh_s, carry_s, acc_s):
    L = CHUNK
    batch = q_ref.shape[0]
    i = pl.program_id(0)
    lane = _iota((L, LANES), 1)
    row = _iota((L, LANES), 0)
    chains = [(b, p, hh) for b in range(batch) for p in range(2) for hh in range(2)]

    r2 = _iota((2 * L, 2 * L), 0) % L
    c2 = _iota((2 * L, 2 * L), 1)
    suffix_op = jnp.where((c2 >= L) | (r2 >= c2), 1.0, 0.0).astype(BF16)

    def tile_chain(c, b, p, off, mask):
        pc = p * LANES
        z = _dot_nt(qh_s[c], k_ref[b, pl.ds(off, L), pc:pc + LANES])
        yield
        lk = -(jnp.maximum(z, 0.0) + jnp.log(1.0 + jnp.exp(-jnp.abs(z))))
        if mask is not None:
            lk = jnp.where(mask, lk, 0.0)
        hi, lo = _split2(lk)
        rr = _dot(jnp.concatenate([hi, lo], axis=1), suffix_op)
        yield
        w = jnp.exp(z + rr[:, :L]) if mask is not None else jnp.exp(z + rr[:, :L] + carry_s[c])
        if mask is not None:
            w = jnp.where(mask, w, 0.0)
        part = _dot(w.astype(BF16), v_ref[b, pl.ds(off, L), pc:pc + LANES])
        carry_s[c] = rr[:, L:] if mask is not None else carry_s[c] + rr[:, L:]
        yield
        acc_s[c] = part if mask is not None else acc_s[c] + part

    for c, (b, p, hh) in enumerate(chains):
        hm = (lane >= HEAD_DIM) if hh else (lane < HEAD_DIM)
        qh_s[c] = jnp.where(hm, q_ref[b, :, p * LANES:(p + 1) * LANES] * (HEAD_DIM ** -0.5), 0.0).astype(BF16)
    diag_off = pl.multiple_of(i * L, L)
    strict = lane < row
    _interleave(tile_chain(c, b, p, diag_off, strict) for c, (b, p, hh) in enumerate(chains))

    def cond(st):
        return jnp.logical_and(st[0] >= 0, st[1] == 0)

    def body(st):
        off = pl.multiple_of(st[0] * L, L)
        _interleave(tile_chain(c, b, p, off, None) for c, (b, p, hh) in enumerate(chains))
        worst = jnp.max(jnp.max(carry_s[...], axis=0))
        return (st[0] - 1, (worst <= F32_EXP_UNDERFLOW).astype(jnp.int32))

    lax.while_loop(cond, body, (i - 1, jnp.int32(0)))
    for b in range(batch):
        for p in range(2):
            c0 = (b * 2 + p) * 2
            out_ref[b, :, p * LANES:(p + 1) * LANES] = jnp.where(lane < HEAD_DIM, acc_s[c0], acc_s[c0 + 1]).astype(BF16)


def _stick_breaking(sb, batch, seq):
    nb = seq // CHUNK
    n_chain = batch * N_HEADS
    sb3 = sb.reshape(batch, seq, -1)
    kv = lambda blk: pl.BlockSpec((batch, seq, W_BRANCH), lambda i: (0, 0, blk), pipeline_mode=pl.Buffered(1))
    return pl.pallas_call(
        _sb_kernel,
        grid=(nb,),
        in_specs=[pl.BlockSpec((batch, CHUNK, W_BRANCH), lambda i: (0, i, 0)), kv(1), kv(2)],
        out_specs=pl.BlockSpec((batch, CHUNK, W_BRANCH), lambda i: (0, i, 0)),
        out_shape=jax.ShapeDtypeStruct((batch, seq, W_BRANCH), BF16),
        scratch_shapes=[pltpu.VMEM((n_chain, CHUNK, LANES), BF16), pltpu.VMEM((n_chain, CHUNK, LANES), F32),
                        pltpu.VMEM((n_chain, CHUNK, LANES), F32)],
        compiler_params=_params(("arbitrary",)),
        name="stick_breaking",
    )(sb3, sb3, sb3)


def _swa_kernel(sink_ref, q_ref, kc_ref, kp_ref, vc_ref, vp_ref, bias_ref, qg_ref, kg_ref, out_ref):
    L = CHUNK
    batch = q_ref.shape[0]
    i = pl.program_id(0)
    lane = _iota((L, LANES), 1)
    t = _iota((L, 2 * L), 0)
    j = _iota((L, 2 * L), 1)
    dist = t + L - j
    valid = (dist >= 0) & (dist < L) & ((j >= L) | (i > 0))

    def qk_norm(x, g_ref):
        return x * lax.rsqrt(_head_mean_sq(x) + EPS) * g_ref[...]

    def group_chain(b, g):
        sl = slice(g * LANES, (g + 1) * LANES)
        qn = qk_norm(q_ref[b, :, sl], qg_ref) * (HEAD_DIM ** -0.5)
        kcat = jnp.concatenate([qk_norm(kp_ref[b, :, sl], kg_ref), qk_norm(kc_ref[b, :, sl], kg_ref)],
                               axis=0).astype(BF16)
        qk = [_dot_nt(jnp.where((lane >= HEAD_DIM) if r else (lane < HEAD_DIM), qn, 0.0).astype(BF16), kcat)
              for r in range(2)]
        yield
        vcat = jnp.concatenate([vp_ref[b, :, sl], vc_ref[b, :, sl]], axis=0).astype(BF16)
        outs = []
        for r in range(2):
            hq = 2 * g + r
            logits = jnp.where(valid, qk[r] + bias_ref[b, 0, hq], NEG)
            sink = sink_ref[hq]
            m = jnp.maximum(jnp.max(logits, axis=-1, keepdims=True), sink)
            pr = jnp.exp(logits - m)
            den = jnp.sum(pr, axis=-1, keepdims=True) + jnp.exp(sink - m)
            outs.append(_dot((pr / den).astype(BF16), vcat))
        yield
        out_ref[b, :, sl] = jnp.where(lane < HEAD_DIM, outs[0], outs[1]).astype(BF16)

    _interleave(group_chain(b, g) for b in range(batch) for g in range(2))


def _swa(swa, bias, sinks, q_g, k_g, batch, seq):
    nb = seq // CHUNK
    swa3 = swa.reshape(batch, seq, -1)
    cur = lambda blk: pl.BlockSpec((batch, CHUNK, W_BRANCH), lambda i: (0, i, blk))
    prev = lambda blk: pl.BlockSpec((batch, CHUNK, W_BRANCH), lambda i: (0, jnp.maximum(i - 1, 0), blk))
    return pl.pallas_call(
        _swa_kernel,
        grid=(nb,),
        in_specs=[pl.BlockSpec(memory_space=pltpu.SMEM), cur(0), cur(1), prev(1), cur(2), prev(2),
                  pl.BlockSpec((batch, 1, N_HEADS, CHUNK, 2 * CHUNK), lambda i: (0, i, 0, 0, 0)),
                  _resident(q_g), _resident(k_g)],
        out_specs=pl.BlockSpec((batch, CHUNK, W_BRANCH), lambda i: (0, i, 0)),
        out_shape=jax.ShapeDtypeStruct((batch, seq, W_BRANCH), BF16),
        compiler_params=_params(("parallel",)),
        name="swa",
    )(sinks, swa3, swa3, swa3, swa3, swa3, bias.reshape(batch, nb, N_HEADS, CHUNK, 2 * CHUNK), q_g, k_g)


def _merge_kernel(x_ref, g_ref, sc_ref, sh_ref, gt_ref, b0_ref, b1_ref, b2_ref, b3_ref,
                  wg_ref, wu_ref, wo_ref, out_ref):
    x = x_ref[...]
    h = _norm_mod(x, g_ref[...], sc_ref[0], sh_ref[0]).astype(BF16)
    merged = None
    for n, b_ref in enumerate((b0_ref, b1_ref, b2_ref, b3_ref)):
        gate = _sigmoid(_dot(h, wg_ref[:, n * D_MODEL:(n + 1) * D_MODEL]))
        term = gate * _dot(b_ref[...], wu_ref[n])
        merged = term if merged is None else merged + term
    mix = _dot(merged.astype(BF16), wo_ref[...])
    out_ref[...] = x + gt_ref[0] * mix


def _merge(x, g, scale, shift, gate, branches, w_gate, w_up, w_out, seq):
    t, d = x.shape
    tm = 256
    tpb = seq // tm
    row = lambda n: pl.BlockSpec((tm, n), lambda i: (i, 0))
    mod = pl.BlockSpec((1, 1, d), lambda i: (i // tpb, 0, 0))
    return pl.pallas_call(
        _merge_kernel,
        grid=(t // tm,),
        in_specs=[row(d), _resident(g), mod, mod, mod] + [row(W_BRANCH)] * 4
        + [_resident(w_gate), _resident(w_up), _resident(w_out)],
        out_specs=row(d),
        out_shape=jax.ShapeDtypeStruct((t, d), F32),
        compiler_params=_params(("parallel",)),
        name="merge",
    )(x, g, scale, shift, gate, *branches, w_gate, w_up, w_out)


def _ffn_kernel(x_ref, g_ref, sc_ref, sh_ref, gt_ref, w1_ref, w2_ref, out_ref):
    x = x_ref[...]
    h = _norm_mod(x, g_ref[...], sc_ref[0], sh_ref[0]).astype(BF16)
    tf = 1024
    acc = None
    for n in range(D_FF // tf):
        a = jnp.maximum(_dot(h, w1_ref[:, n * tf:(n + 1) * tf]), 0.0)
        part = _dot((a * a).astype(BF16), w2_ref[n * tf:(n + 1) * tf, :])
        acc = part if acc is None else acc + part
    out_ref[...] = x + gt_ref[0] * acc


def _ffn(x, g, scale, shift, gate, w1, w2, seq):
    t, d = x.shape
    tm = 256
    tpb = seq // tm
    row = pl.BlockSpec((tm, d), lambda i: (i, 0))
    mod = pl.BlockSpec((1, 1, d), lambda i: (i // tpb, 0, 0))
    return pl.pallas_call(
        _ffn_kernel,
        grid=(t // tm,),
        in_specs=[row, _resident(g), mod, mod, mod, _resident(w1), _resident(w2)],
        out_specs=row,
        out_shape=jax.ShapeDtypeStruct((t, d), F32),
        compiler_params=_params(("parallel",)),
        name="ffn",
    )(x, g, scale, shift, gate, w1, w2)


def _permute_w_in(w):
    d = w.shape[0]
    hd = HEAD_DIM
    o_ret = 3 * W_BRANCH + 2 * N_HEADS + W_BRANCH
    o_sb = o_ret + 4 * W_BRANCH
    o_swa = o_sb + 3 * W_BRANCH
    mi = w[:, 768:772]
    mf = w[:, 772:776]
    zpad = jnp.zeros((d, LANES - 4), w.dtype)
    ifc = jnp.concatenate([mi[:, 0:2], mf[:, 0:2], zpad, mi[:, 2:4], mf[:, 2:4], zpad], axis=1)
    aq = w[:, o_swa:o_swa + 256]
    ak = w[:, o_swa + 256:o_swa + 384]
    av = w[:, o_swa + 384:o_swa + 512]
    dup = lambda a: jnp.concatenate([a[:, :hd], a[:, :hd], a[:, hd:], a[:, hd:]], axis=1)
    w_perm = jnp.concatenate([w[:, 0:768], w[:, 776:1032], ifc, w[:, o_ret:o_sb], w[:, o_sb:o_swa],
                              aq, dup(ak), dup(av)], axis=1)
    z4 = jnp.zeros((d, 4), w.dtype)
    ift = jnp.concatenate([mi[:, 0:2], mf[:, 0:2], z4, mi[:, 2:4], mf[:, 2:4], z4], axis=1).T
    return w_perm.astype(BF16), ift.astype(BF16)


def _gate_bias_layout(gate_b):
    ib, fb = gate_b[0], gate_b[1]
    z = jnp.zeros((LANES - 4,), F32)
    row = jnp.concatenate([ib[0:2], fb[0:2], z, ib[2:4], fb[2:4], z]).reshape(1, 2 * LANES)
    z4 = jnp.zeros((4,), F32)
    col = jnp.concatenate([ib[0:2], fb[0:2], z4, ib[2:4], fb[2:4], z4])
    return row, jnp.broadcast_to(col[:, None], (16, LANES))


def kernel(x, c, positions, w_ada, b_ada, norm_g, w_in, mlstm_conv_w, mlstm_conv_b, mlstm_gate_b,
           ret_norm_g, swa_q_norm_g, swa_k_norm_g, swa_sinks, rel_bias, w_up, w_out, w_ff1, w_ff2):
    batch, seq, d = x.shape
    depth = w_in.shape[0]
    t = batch * seq
    nb = seq // CHUNK

    c8 = jnp.concatenate([c, jnp.zeros((8 - batch, d), F32)], axis=0)
    mod = _ada(c8, w_ada.reshape(depth * 2, d, 3 * d), b_ada.reshape(depth * 2, 1, 3 * d))
    mod = mod[:, :batch].reshape(depth, 2, batch, 3, 1, d)

    pos_col = jnp.broadcast_to(positions.reshape(t, 1), (t, LANES))
    pos_row = positions.reshape(batch * nb, 1, CHUNK)
    half = HEAD_DIM // 2
    inv = ROPE_BASE ** (-(np.arange(LANES) % half).astype(np.float64) / half)
    cos_t, sin_t = _rope_tables(pos_col, jnp.asarray(inv, F32).reshape(1, LANES))
    bias = _bias_tiles(rel_bias, pos_col, pos_row, batch, nb)

    log_gamma = np.log(1.0 - np.exp2(-(RET_DECAY_BASE + np.arange(N_HEADS, dtype=np.float64))))
    lg_tab = jnp.asarray(np.broadcast_to(np.repeat(log_gamma, HEAD_DIM).reshape(2, 1, LANES), (2, 8, LANES)), F32)

    xt = x.reshape(t, d)
    for l in range(depth):
        w_perm, w_ift = _permute_w_in(w_in[l])
        gb_row, gbt = _gate_bias_layout(mlstm_gate_b[l])
        g1 = norm_g[l, 0].reshape(1, d)
        g2 = norm_g[l, 1].reshape(1, d)
        shift1, scale1, gate1 = mod[l, 0, :, 0], mod[l, 0, :, 1], mod[l, 0, :, 2]
        shift2, scale2, gate2 = mod[l, 1, :, 0], mod[l, 1, :, 1], mod[l, 1, :, 2]

        ml, ifc, ret, sb, swa, ift = _in_proj(xt, g1, scale1, shift1, w_perm, w_ift, gb_row, gbt, batch, seq)
        out_ml = _mlstm(ml, ifc, ift, mlstm_conv_w[l], mlstm_conv_b[l].reshape(1, 2 * W_BRANCH), batch, seq)
        out_ret = _retention(ret, cos_t, sin_t, lg_tab, ret_norm_g[l].reshape(1, W_BRANCH), batch, seq)
        out_sb = _stick_breaking(sb, batch, seq)
        qg = jnp.tile(swa_q_norm_g[l], 2).reshape(1, LANES)
        kg = jnp.tile(swa_k_norm_g[l], 2).reshape(1, LANES)
        out_swa = _swa(swa, bias, swa_sinks[l], qg, kg, batch, seq)

        w_gate = w_in[l][:, w_in.shape[2] - N_HEADS * d:].astype(BF16)
        branches = tuple(o.reshape(t, W_BRANCH) for o in (out_ml, out_ret, out_sb, out_swa))
        xt = _merge(xt, g1, scale1, shift1, gate1, branches,
                    w_gate, w_up[l].astype(BF16), w_out[l].astype(BF16), seq)
        xt = _ffn(xt, g2, scale2, shift2, gate2, w_ff1[l].astype(BF16), w_ff2[l].astype(BF16), seq)
    return xt.reshape(batch, seq, d)
```

```python
import functools
import math

import numpy as np
import jax
import jax.numpy as jnp
from jax import lax
from jax.experimental import pallas as pl
from jax.experimental.pallas import tpu as pltpu

F32 = jnp.float32
BF16 = jnp.bfloat16

D_MODEL = 1024
HEAD_DIM = 64
N_HEADS = 4
W_BRANCH = N_HEADS * HEAD_DIM
LANES = 128
CHUNK = 128
CONV_K = 4
D_FF = 4 * D_MODEL
N_BUCKETS = 32
MAX_DIST = 128
ROPE_BASE = 10000.0
RET_DECAY_BASE = 5.0
EPS = 1e-6
NEG = -1e30
F32_EXP_UNDERFLOW = -104.0
VMEM_LIMIT = 56 * 1024 * 1024

SEG_ML = (0, 1024)
SEG_IFC = (1024, 1280)
SEG_RET = (1280, 2304)
SEG_SB = (2304, 3072)
SEG_SWA = (3072, 3840)
N_IN = 3840


def _dot(a, b):
    return jnp.dot(a, b, preferred_element_type=F32)


def _dot_nt(a, b):
    return lax.dot_general(a, b, (((1,), (1,)), ((), ())), preferred_element_type=F32)


def _dot_tn(a, b):
    return lax.dot_general(a, b, (((0,), (0,)), ((), ())), preferred_element_type=F32)


def _split2(x):
    hi = x.astype(BF16)
    lo = (x - hi.astype(F32)).astype(BF16)
    return hi, lo


def _split3(x):
    hi = x.astype(BF16)
    r = x - hi.astype(F32)
    mid = r.astype(BF16)
    lo = (r - mid.astype(F32)).astype(BF16)
    return hi, mid, lo


def _iota(shape, axis):
    return lax.broadcasted_iota(jnp.int32, shape, axis)


def _log_sigmoid(x):
    return jnp.minimum(x, 0.0) - jnp.log(1.0 + jnp.exp(-jnp.abs(x)))


def _sigmoid(x):
    return 1.0 / (1.0 + jnp.exp(-x))


def _silu(x):
    return x * _sigmoid(x)


def _norm_mod(x, g, scale, shift):
    ms = jnp.mean(x * x, axis=-1, keepdims=True)
    y = x * lax.rsqrt(ms + EPS)
    return (y * g) * (1.0 + scale) + shift


def _head_mean_sq(x):
    lane = _iota(x.shape, 1)
    sq = x * x
    s0 = jnp.sum(jnp.where(lane < HEAD_DIM, sq, 0.0), axis=-1, keepdims=True)
    s1 = jnp.sum(jnp.where(lane >= HEAD_DIM, sq, 0.0), axis=-1, keepdims=True)
    return jnp.where(lane < HEAD_DIM, s0, s1) * (1.0 / HEAD_DIM)


def _params(sem):
    return pltpu.CompilerParams(dimension_semantics=sem, vmem_limit_bytes=VMEM_LIMIT)


def _interleave(chains):
    live = list(chains)
    while live:
        still = []
        for ch in live:
            try:
                next(ch)
                still.append(ch)
            except StopIteration:
                pass
        live = still


def _resident(a):
    return pl.BlockSpec(a.shape, lambda *_: (0,) * a.ndim, pipeline_mode=pl.Buffered(1))


def _layer_of(a, layer):
    return pl.BlockSpec((None,) + a.shape[1:], lambda *_: (layer,) + (0,) * (a.ndim - 1),
                        pipeline_mode=pl.Buffered(1))


def _ada_kernel(c_ref, w_ref, b_ref, o_ref):
    c = c_ref[...]
    ch, cl = _split2(_silu(c))
    wh, wl = _split2(w_ref[0])
    o_ref[0] = _dot(ch, wh) + _dot(ch, wl) + _dot(cl, wh) + b_ref[0]


def _ada(c8, w_ada, b_ada):
    n_mod, d, n3 = w_ada.shape
    tn = 1024
    return pl.pallas_call(
        _ada_kernel,
        grid=(n_mod, n3 // tn),
        in_specs=[pl.BlockSpec((8, d), lambda m, n: (0, 0)),
                  pl.BlockSpec((1, d, tn), lambda m, n: (m, 0, n)),
                  pl.BlockSpec((1, 1, tn), lambda m, n: (m, 0, n))],
        out_specs=pl.BlockSpec((1, 8, tn), lambda m, n: (m, 0, n)),
        out_shape=jax.ShapeDtypeStruct((n_mod, 8, n3), F32),
        compiler_params=_params(("parallel", "parallel")),
        name="ada_mod",
    )(c8, w_ada, b_ada)


def _rope_kernel(pos_ref, inv_ref, cos_ref, sin_ref):
    ang = pos_ref[...].astype(F32) * inv_ref[...]
    cos_ref[...] = jnp.cos(ang)
    sin_ref[...] = jnp.sin(ang)


def _rope_tables(pos_b, inv_row):
    t = pos_b.shape[0]
    tm = 1024
    spec = pl.BlockSpec((tm, LANES), lambda i: (i, 0))
    return pl.pallas_call(
        _rope_kernel,
        grid=(t // tm,),
        in_specs=[spec, pl.BlockSpec((1, LANES), lambda i: (0, 0))],
        out_specs=[spec, spec],
        out_shape=[jax.ShapeDtypeStruct((t, LANES), F32)] * 2,
        compiler_params=_params(("parallel",)),
        name="rope_tables",
    )(pos_b, inv_row)


def _bias_kernel(tab_ref, posq_ref, posc_ref, posp_ref, o_ref):
    pos_q = posq_ref[...]
    rel = jnp.concatenate([pos_q - posp_ref[0], pos_q - posc_ref[0]], axis=1)
    n = jnp.maximum(rel, 0)
    max_exact = N_BUCKETS // 2
    nf = jnp.maximum(n, 1).astype(F32)
    large = max_exact + (jnp.log(nf / max_exact) / math.log(MAX_DIST / max_exact)
                         * (N_BUCKETS - max_exact)).astype(jnp.int32)
    large = jnp.minimum(large, N_BUCKETS - 1)
    bucket = jnp.where(n < max_exact, n, large)
    halves = (bucket[:, :CHUNK], bucket[:, CHUNK:])
    for h in range(N_HEADS):
        row = jnp.broadcast_to(tab_ref[h:h + 1, :], (CHUNK, LANES))
        o_ref[0, h] = jnp.concatenate([jnp.take_along_axis(row, idx, axis=1) for idx in halves], axis=1)


def _bias_tiles(rel_bias, pos_b, pos_r, batch, n_b):
    nblk = batch * n_b
    return pl.pallas_call(
        _bias_kernel,
        grid=(nblk,),
        in_specs=[pl.BlockSpec((N_HEADS, LANES), lambda i: (0, 0)),
                  pl.BlockSpec((CHUNK, LANES), lambda i: (i, 0)),
                  pl.BlockSpec((1, 1, CHUNK), lambda i: (i, 0, 0)),
                  pl.BlockSpec((1, 1, CHUNK), lambda i: (jnp.maximum(i - 1, 0), 0, 0))],
        out_specs=pl.BlockSpec((1, N_HEADS, CHUNK, 2 * CHUNK), lambda i: (i, 0, 0, 0)),
        out_shape=jax.ShapeDtypeStruct((nblk, N_HEADS, CHUNK, 2 * CHUNK), F32),
        compiler_params=_params(("parallel",)),
        name="t5_bias",
    )(rel_bias, pos_b, pos_r, pos_r)


def _in_kernel(x_ref, g_ref, sc_ref, sh_ref, w_ref, wt_ref, gb_ref, gbt_ref,
               ml_ref, ifc_ref, ret_ref, sb_ref, swa_ref, ift_ref):
    h = _norm_mod(x_ref[...], g_ref[...], sc_ref[0], sh_ref[0]).astype(BF16)

    def seg(s):
        return _dot(h, w_ref[:, s[0]:s[1]])

    ml_ref[...] = seg(SEG_ML)
    ifc_ref[...] = seg(SEG_IFC) + gb_ref[...]
    ret_ref[...] = seg(SEG_RET)
    sb_ref[...] = seg(SEG_SB).astype(BF16)
    swa_ref[...] = seg(SEG_SWA)
    ift_ref[0] = _dot_nt(wt_ref[...], h) + gbt_ref[:, 0:1]


def _in_proj(x, g, scale, shift, w_perm, w_ift, gb_row, gbt, layer, batch, seq):
    t, d = x.shape
    tm = 512
    tpb = seq // tm
    row = lambda n: pl.BlockSpec((tm, n), lambda i: (i, 0))
    mod = pl.BlockSpec((1, 1, d), lambda i: (i // tpb, 0, 0))
    widths = [s[1] - s[0] for s in (SEG_ML, SEG_IFC, SEG_RET, SEG_SB, SEG_SWA)]
    dts = [F32, F32, F32, BF16, F32]
    return pl.pallas_call(
        _in_kernel,
        grid=(t // tm,),
        in_specs=[row(d), _resident(g), mod, mod, _layer_of(w_perm, layer), _layer_of(w_ift, layer), _resident(gb_row),
                  _resident(gbt)],
        out_specs=[row(n) for n in widths] + [pl.BlockSpec((1, 16, tm), lambda i: (i // tpb, 0, i % tpb))],
        out_shape=[jax.ShapeDtypeStruct((t, n), dt) for n, dt in zip(widths, dts)]
        + [jax.ShapeDtypeStruct((batch, 16, seq), F32)],
        compiler_params=_params(("parallel",)),
        name="in_proj",
    )(x, g, scale, shift, w_perm, w_ift, gb_row, gbt)


def _ml_kernel(ml_ref, ifc_ref, ift_ref, cw_ref, cb_ref, out_ref, halo, c_st, m_st):
    L = CHUNK
    HALO = 8
    batch = ml_ref.shape[0]

    @pl.when(pl.program_id(0) == 0)
    def _init():
        halo[:, 0:HALO, :] = jnp.zeros((halo.shape[0], HALO, LANES), F32)
        c_st[...] = jnp.zeros(c_st.shape, F32)
        m_st[...] = jnp.zeros(m_st.shape, F32)

    lane = _iota((L, LANES), 1)
    row = _iota((L, LANES), 0)
    causal = row >= lane
    tri = jnp.where(causal, 1.0, 0.0).astype(BF16)
    upper = jnp.where(row <= lane, 1.0, 0.0).astype(BF16)
    eye = jnp.where(row == lane, 1.0, 0.0).astype(BF16)
    spread = jnp.where(_iota((LANES, 4 * LANES), 0) == _iota((LANES, 4 * LANES), 1) // LANES,
                       1.0, 0.0).astype(BF16)

    def conv_silu(b, col, slot):
        hl = halo.at[slot]
        hl[HALO:HALO + L, :] = ml_ref[b, :, col:col + LANES]
        acc = jnp.zeros((L, LANES), F32) + cb_ref[:, col:col + LANES]
        for j in range(CONV_K):
            off = HALO - (CONV_K - 1) + j
            acc = acc + hl[off:off + L, :] * cw_ref[j:j + 1, col:col + LANES]
        hl[0:HALO, :] = hl[L:L + HALO, :]
        return _silu(acc)

    def pair_chain(b, p):
        pc = p * LANES
        ift = ift_ref[b, p * 8:p * 8 + 8, :]
        lr_h, lr_m, lr_l = _split3(_log_sigmoid(ift))
        a_rows = _dot(lr_h, upper) + _dot(lr_m, upper) + _dot(lr_l, upper)
        g_h, g_m, g_l = _split3(ifc_ref[b, :, pc:pc + LANES])
        gates_b = _dot(g_h, spread) + _dot(g_m, spread) + _dot(g_l, spread)
        q = conv_silu(b, pc, (b * 2 + p) * 2)
        k = conv_silu(b, W_BRANCH + pc, (b * 2 + p) * 2 + 1)
        v = ml_ref[b, :, 2 * W_BRANCH + pc:2 * W_BRANCH + pc + LANES]
        k_bf = k.astype(BF16)
        yield

        lf_h, lf_m, lf_l = _split3(_log_sigmoid(gates_b[:, 2 * LANES:]))
        a_b = _dot(tri, lf_h) + _dot(tri, lf_m) + _dot(tri, lf_l)
        heads = []
        for hh in range(2):
            ch = (b * 2 + p) * 2 + hh
            hm = (lane >= HEAD_DIM) if hh else (lane < HEAD_DIM)
            qh = (jnp.where(hm, q, 0.0) * (HEAD_DIM ** -0.5)).astype(BF16)
            c_in = c_st[ch]
            heads.append(dict(ch=ch, hm=hm, c_in=c_in, s_qk=_dot_nt(qh, k_bf), q_c=_dot(qh, c_in.astype(BF16))))
        yield

        for hh, hd in enumerate(heads):
            i_col = gates_b[:, hh * LANES:(hh + 1) * LANES]
            a_col = a_b[:, hh * LANES:(hh + 1) * LANES]
            a_last = a_col[L - 1:L, :]
            m_in = m_st[hd["ch"], 0:1, :]
            v_aug = jnp.where(hd["hm"], v, 1.0).astype(BF16)

            d_intra = jnp.where(causal, a_col - (a_rows[2 + hh:3 + hh, :] - ift[hh:hh + 1, :]), NEG)
            m_intra = jnp.max(d_intra, axis=-1, keepdims=True)
            m_inter = a_col + m_in
            m_tot = jnp.maximum(m_inter, m_intra)
            pm = jnp.exp(d_intra - m_tot) * hd["s_qk"]
            hd["p_v"] = _dot(pm.astype(BF16), v_aug)
            hd["s_inter"] = jnp.exp(m_inter - m_tot)
            hd["floor"] = jnp.exp(-m_tot)

            g_col = a_last - a_col + i_col
            m_loc = jnp.max(g_col, axis=0, keepdims=True)
            kw = (jnp.where(hd["hm"], k, 0.0) * jnp.exp(g_col - m_loc)).astype(BF16)
            hd["kw_t"] = _dot_nt(eye, kw)
            hd["v_aug"] = v_aug
            m_new = jnp.maximum(a_last + m_in, m_loc)
            hd["w_old"] = jnp.exp(a_last + m_in - m_new)
            hd["w_new"] = jnp.exp(m_loc - m_new)
            m_st[hd["ch"]] = jnp.zeros((8, LANES), F32) + m_new
        yield

        for hd in heads:
            hd["c_loc"] = _dot(hd["kw_t"].astype(BF16), hd["v_aug"])
            hd["o_aug"] = hd["s_inter"] * hd["q_c"] + hd["p_v"]
        den = pltpu.roll(jnp.where(lane < HEAD_DIM, heads[1]["o_aug"], heads[0]["o_aug"]), HEAD_DIM, 1)
        num = jnp.where(lane < HEAD_DIM, heads[0]["o_aug"], heads[1]["o_aug"])
        floor = jnp.where(lane < HEAD_DIM, heads[0]["floor"], heads[1]["floor"])
        h_ml = num / jnp.maximum(jnp.abs(den), floor)
        o_gate = ml_ref[b, :, 3 * W_BRANCH + pc:3 * W_BRANCH + pc + LANES]
        out_ref[b, :, pc:pc + LANES] = (h_ml * _sigmoid(o_gate)).astype(BF16)
        yield

        for hd in heads:
            c_st[hd["ch"]] = hd["w_old"] * hd["c_in"] + hd["w_new"] * hd["c_loc"]

    _interleave(pair_chain(b, p) for b in range(batch) for p in range(2))


def _mlstm(ml, ifc, ift, conv_w, conv_b, batch, seq):
    nc = seq // CHUNK
    n_chain = batch * N_HEADS
    step = lambda n: pl.BlockSpec((batch, CHUNK, n), lambda c: (0, c, 0))
    return pl.pallas_call(
        _ml_kernel,
        grid=(nc,),
        in_specs=[step(4 * W_BRANCH), step(2 * LANES), pl.BlockSpec((batch, 16, CHUNK), lambda c: (0, 0, c)),
                  _resident(conv_w), _resident(conv_b)],
        out_specs=step(W_BRANCH),
        out_shape=jax.ShapeDtypeStruct((batch, seq, W_BRANCH), BF16),
        scratch_shapes=[pltpu.VMEM((n_chain, CHUNK + 8, LANES), F32),
                        pltpu.VMEM((n_chain, LANES, LANES), F32), pltpu.VMEM((n_chain, 8, LANES), F32)],
        compiler_params=_params(("arbitrary",)),
        name="mlstm",
    )(ml.reshape(batch, seq, -1), ifc.reshape(batch, seq, -1), ift, conv_w, conv_b)


def _ret_kernel(ret_ref, cos_ref, sin_ref, lg_ref, ng_ref, out_ref, st):
    L = CHUNK
    batch = ret_ref.shape[0]

    @pl.when(pl.program_id(0) == 0)
    def _init():
        st[...] = jnp.zeros(st.shape, F32)

    lane = _iota((L, LANES), 1)
    row = _iota((L, LANES), 0)
    rowf = row.astype(F32)
    relf = jnp.maximum(row - lane, 0).astype(F32)
    first = (lane % HEAD_DIM) < (HEAD_DIM // 2)
    same_head = (row // HEAD_DIM) == (lane // HEAD_DIM)

    def rot(t, cos, sin):
        partner = jnp.where(first, -pltpu.roll(t, LANES - HEAD_DIM // 2, 1), pltpu.roll(t, HEAD_DIM // 2, 1))
        return t * cos + partner * sin

    decays = []
    for p in range(2):
        lg = lg_ref[p, 0:1, :]
        decays.append(dict(
            q=jnp.exp((rowf + 1.0) * lg), k=jnp.exp((L - 1.0 - rowf) * lg), c=jnp.exp(float(L) * lg),
            intra=[jnp.where(row >= lane, jnp.exp(relf * lg[:, hh * HEAD_DIM:hh * HEAD_DIM + 1]), 0.0)
                   for hh in range(2)]))

    def pair_chain(b, p):
        pc = p * LANES
        dec = decays[p]
        cos = cos_ref[b]
        sin = sin_ref[b]
        q = rot(ret_ref[b, :, pc:pc + LANES], cos, sin)
        k = rot(ret_ref[b, :, W_BRANCH + pc:W_BRANCH + pc + LANES], cos, sin) * (HEAD_DIM ** -0.5)
        v_bf = ret_ref[b, :, 2 * W_BRANCH + pc:2 * W_BRANCH + pc + LANES].astype(BF16)
        k_bf = k.astype(BF16)
        state = st[b * 2 + p]
        s_qk = [_dot_nt(jnp.where((lane >= HEAD_DIM) if hh else (lane < HEAD_DIM), q, 0.0).astype(BF16), k_bf)
                for hh in range(2)]
        inter = _dot(q.astype(BF16), state.astype(BF16))
        kv = _dot_tn((k * dec["k"]).astype(BF16), v_bf)
        yield

        intra = [_dot((s_qk[hh] * dec["intra"][hh]).astype(BF16), v_bf) for hh in range(2)]
        st[b * 2 + p] = dec["c"] * state + jnp.where(same_head, kv, 0.0)
        yield

        o = jnp.where(lane < HEAD_DIM, intra[0], intra[1]) + inter * dec["q"]
        y = o * lax.rsqrt(_head_mean_sq(o) + EPS) * ng_ref[:, pc:pc + LANES]
        gate = ret_ref[b, :, 3 * W_BRANCH + pc:3 * W_BRANCH + pc + LANES]
        out_ref[b, :, pc:pc + LANES] = (y * _silu(gate)).astype(BF16)

    _interleave(pair_chain(b, p) for b in range(batch) for p in range(2))


def _retention(ret, cos_t, sin_t, lg_tab, ret_g, batch, seq):
    nc = seq // CHUNK
    step = lambda n: pl.BlockSpec((batch, CHUNK, n), lambda c: (0, c, 0))
    return pl.pallas_call(
        _ret_kernel,
        grid=(nc,),
        in_specs=[step(4 * W_BRANCH), step(LANES), step(LANES), _resident(lg_tab), _resident(ret_g)],
        out_specs=step(W_BRANCH),
        out_shape=jax.ShapeDtypeStruct((batch, seq, W_BRANCH), BF16),
        scratch_shapes=[pltpu.VMEM((batch * 2, LANES, LANES), F32)],
        compiler_params=_params(("arbitrary",)),
        name="retention",
    )(ret.reshape(batch, seq, -1), cos_t.reshape(batch, seq, LANES), sin_t.reshape(batch, seq, LANES), lg_tab, ret_g)


def _sb_kernel(q_ref, k_ref, v_ref, out_ref, qh_s, carry_s, acc_s):
    L = CHUNK
    batch = q_ref.shape[0]
    i = pl.program_id(0)
    lane = _iota((L, LANES), 1)
    row = _iota((L, LANES), 0)
    chains = [(b, p, hh) for b in range(batch) for p in range(2) for hh in range(2)]

    r2 = _iota((2 * L, 2 * L), 0) % L
    c2 = _iota((2 * L, 2 * L), 1)
    suffix_op = jnp.where((c2 >= L) | (r2 >= c2), 1.0, 0.0).astype(BF16)

    def tile_chain(c, b, p, tiles, fresh):
        pc = p * LANES
        zs = [_dot_nt(qh_s[c], k_ref[b, pl.ds(off, L), pc:pc + LANES]) for off, _ in tiles]
        yield
        rrs = []
        for z, (_, mask) in zip(zs, tiles):
            lk = -(jnp.maximum(z, 0.0) + jnp.log(1.0 + jnp.exp(-jnp.abs(z))))
            if mask is not None:
                lk = jnp.where(mask, lk, 0.0)
            hi, lo = _split2(lk)
            rrs.append(_dot(jnp.concatenate([hi, lo], axis=1), suffix_op))
        yield
        carry = None if fresh else carry_s[c]
        parts = []
        for z, rr, (off, mask) in zip(zs, rrs, tiles):
            e = z + rr[:, :L]
            w = jnp.exp(e if carry is None else e + carry)
            if mask is not None:
                w = jnp.where(mask, w, 0.0)
            parts.append(_dot(w.astype(BF16), v_ref[b, pl.ds(off, L), pc:pc + LANES]))
            carry = rr[:, L:] if carry is None else carry + rr[:, L:]
        carry_s[c] = carry
        yield
        acc = None if fresh else acc_s[c]
        for part in parts:
            acc = part if acc is None else acc + part
        acc_s[c] = acc

    def walk(tiles, fresh):
        _interleave(tile_chain(c, b, p, tiles, fresh) for c, (b, p, hh) in enumerate(chains))

    for c, (b, p, hh) in enumerate(chains):
        hm = (lane >= HEAD_DIM) if hh else (lane < HEAD_DIM)
        qh_s[c] = jnp.where(hm, q_ref[b, :, p * LANES:(p + 1) * LANES] * (HEAD_DIM ** -0.5), 0.0).astype(BF16)

    FUSED_BLOCKS = 3
    strict = lane < row

    def head_tiles(n):
        return [(pl.multiple_of((i - d) * L, L), strict if d == 0 else None) for d in range(n)]

    for n in range(1, FUSED_BLOCKS):
        pl.when(i == n - 1)(functools.partial(walk, head_tiles(n), True))
    pl.when(i >= FUSED_BLOCKS - 1)(functools.partial(walk, head_tiles(FUSED_BLOCKS), True))

    def all_underflow():
        return (jnp.max(jnp.max(carry_s[...], axis=0)) <= F32_EXP_UNDERFLOW).astype(jnp.int32)

    def cond(st):
        return jnp.logical_and(st[0] >= 0, st[1] == 0)

    def body(st):
        walk([(pl.multiple_of(st[0] * L, L), None)], False)
        return (st[0] - 1, all_underflow())

    lax.while_loop(cond, body, (i - FUSED_BLOCKS, all_underflow()))
    for b in range(batch):
        for p in range(2):
            c0 = (b * 2 + p) * 2
            out_ref[b, :, p * LANES:(p + 1) * LANES] = jnp.where(lane < HEAD_DIM, acc_s[c0], acc_s[c0 + 1]).astype(BF16)


def _stick_breaking(sb, batch, seq):
    nb = seq // CHUNK
    n_chain = batch * N_HEADS
    sb3 = sb.reshape(batch, seq, -1)
    kv = lambda blk: pl.BlockSpec((batch, seq, W_BRANCH), lambda i: (0, 0, blk), pipeline_mode=pl.Buffered(1))
    return pl.pallas_call(
        _sb_kernel,
        grid=(nb,),
        in_specs=[pl.BlockSpec((batch, CHUNK, W_BRANCH), lambda i: (0, i, 0)), kv(1), kv(2)],
        out_specs=pl.BlockSpec((batch, CHUNK, W_BRANCH), lambda i: (0, i, 0)),
        out_shape=jax.ShapeDtypeStruct((batch, seq, W_BRANCH), BF16),
        scratch_shapes=[pltpu.VMEM((n_chain, CHUNK, LANES), BF16), pltpu.VMEM((n_chain, CHUNK, LANES), F32),
                        pltpu.VMEM((n_chain, CHUNK, LANES), F32)],
        compiler_params=_params(("arbitrary",)),
        name="stick_breaking",
    )(sb3, sb3, sb3)


def _swa_kernel(sink_ref, q_ref, kc_ref, kp_ref, vc_ref, vp_ref, bias_ref, qg_ref, kg_ref, out_ref):
    L = CHUNK
    batch = q_ref.shape[0]
    i = pl.program_id(0)
    lane = _iota((L, LANES), 1)
    t = _iota((L, 2 * L), 0)
    j = _iota((L, 2 * L), 1)
    dist = t + L - j
    valid = (dist >= 0) & (dist < L) & ((j >= L) | (i > 0))

    def qk_norm(x, g_ref):
        return x * lax.rsqrt(_head_mean_sq(x) + EPS) * g_ref[...]

    def group_chain(b, g):
        sl = slice(g * LANES, (g + 1) * LANES)
        qn = qk_norm(q_ref[b, :, sl], qg_ref) * (HEAD_DIM ** -0.5)
        kcat = jnp.concatenate([qk_norm(kp_ref[b, :, sl], kg_ref), qk_norm(kc_ref[b, :, sl], kg_ref)],
                               axis=0).astype(BF16)
        qk = [_dot_nt(jnp.where((lane >= HEAD_DIM) if r else (lane < HEAD_DIM), qn, 0.0).astype(BF16), kcat)
              for r in range(2)]
        yield
        vcat = jnp.concatenate([vp_ref[b, :, sl], vc_ref[b, :, sl]], axis=0).astype(BF16)
        outs = []
        for r in range(2):
            hq = 2 * g + r
            logits = jnp.where(valid, qk[r] + bias_ref[b, 0, hq], NEG)
            sink = sink_ref[hq]
            m = jnp.maximum(jnp.max(logits, axis=-1, keepdims=True), sink)
            pr = jnp.exp(logits - m)
            den = jnp.sum(pr, axis=-1, keepdims=True) + jnp.exp(sink - m)
            outs.append(_dot((pr / den).astype(BF16), vcat))
        yield
        out_ref[b, :, sl] = jnp.where(lane < HEAD_DIM, outs[0], outs[1]).astype(BF16)

    _interleave(group_chain(b, g) for b in range(batch) for g in range(2))


def _swa(swa, bias, sinks, q_g, k_g, batch, seq):
    nb = seq // CHUNK
    swa3 = swa.reshape(batch, seq, -1)
    cur = lambda blk: pl.BlockSpec((batch, CHUNK, W_BRANCH), lambda i: (0, i, blk))
    prev = lambda blk: pl.BlockSpec((batch, CHUNK, W_BRANCH), lambda i: (0, jnp.maximum(i - 1, 0), blk))
    return pl.pallas_call(
        _swa_kernel,
        grid=(nb,),
        in_specs=[pl.BlockSpec(memory_space=pltpu.SMEM), cur(0), cur(1), prev(1), cur(2), prev(2),
                  pl.BlockSpec((batch, 1, N_HEADS, CHUNK, 2 * CHUNK), lambda i: (0, i, 0, 0, 0)),
                  _resident(q_g), _resident(k_g)],
        out_specs=pl.BlockSpec((batch, CHUNK, W_BRANCH), lambda i: (0, i, 0)),
        out_shape=jax.ShapeDtypeStruct((batch, seq, W_BRANCH), BF16),
        compiler_params=_params(("parallel",)),
        name="swa",
    )(sinks, swa3, swa3, swa3, swa3, swa3, bias.reshape(batch, nb, N_HEADS, CHUNK, 2 * CHUNK), q_g, k_g)


def _merge_kernel(x_ref, g_ref, sc_ref, sh_ref, gt_ref, b0_ref, b1_ref, b2_ref, b3_ref,
                  wg_ref, wu_ref, wo_ref, out_ref):
    x = x_ref[...]
    h = _norm_mod(x, g_ref[...], sc_ref[0], sh_ref[0]).astype(BF16)
    merged = None
    for n, b_ref in enumerate((b0_ref, b1_ref, b2_ref, b3_ref)):
        gate = _sigmoid(_dot(h, wg_ref[:, n * D_MODEL:(n + 1) * D_MODEL]))
        term = gate * _dot(b_ref[...], wu_ref[n])
        merged = term if merged is None else merged + term
    mix = _dot(merged.astype(BF16), wo_ref[...])
    out_ref[...] = x + gt_ref[0] * mix


def _merge(x, g, scale, shift, gate, branches, w_gate, w_up, w_out, layer, seq):
    t, d = x.shape
    tm = 256
    tpb = seq // tm
    row = lambda n: pl.BlockSpec((tm, n), lambda i: (i, 0))
    mod = pl.BlockSpec((1, 1, d), lambda i: (i // tpb, 0, 0))
    return pl.pallas_call(
        _merge_kernel,
        grid=(t // tm,),
        in_specs=[row(d), _resident(g), mod, mod, mod] + [row(W_BRANCH)] * 4
        + [_layer_of(w_gate, layer), _layer_of(w_up, layer), _layer_of(w_out, layer)],
        out_specs=row(d),
        out_shape=jax.ShapeDtypeStruct((t, d), F32),
        compiler_params=_params(("parallel",)),
        name="merge",
    )(x, g, scale, shift, gate, *branches, w_gate, w_up, w_out)


def _ffn_kernel(x_ref, g_ref, sc_ref, sh_ref, gt_ref, w1_ref, w2_ref, out_ref):
    x = x_ref[...]
    h = _norm_mod(x, g_ref[...], sc_ref[0], sh_ref[0]).astype(BF16)
    tf = 1024
    acc = None
    for n in range(D_FF // tf):
        a = jnp.maximum(_dot(h, w1_ref[:, n * tf:(n + 1) * tf]), 0.0)
        part = _dot((a * a).astype(BF16), w2_ref[n * tf:(n + 1) * tf, :])
        acc = part if acc is None else acc + part
    out_ref[...] = x + gt_ref[0] * acc


def _ffn(x, g, scale, shift, gate, w1, w2, layer, seq):
    t, d = x.shape
    tm = 256
    tpb = seq // tm
    row = pl.BlockSpec((tm, d), lambda i: (i, 0))
    mod = pl.BlockSpec((1, 1, d), lambda i: (i // tpb, 0, 0))
    return pl.pallas_call(
        _ffn_kernel,
        grid=(t // tm,),
        in_specs=[row, _resident(g), mod, mod, mod, _layer_of(w1, layer), _layer_of(w2, layer)],
        out_specs=row,
        out_shape=jax.ShapeDtypeStruct((t, d), F32),
        compiler_params=_params(("parallel",)),
        name="ffn",
    )(x, g, scale, shift, gate, w1, w2)


GATE_COL = 3 * W_BRANCH
SHIFT = 2 * N_HEADS
N_SHIFTED = 52


def _w_in_kernel(w_ref, tail_ref, perm_ref, gate_ref):
    tm = w_ref.shape[0]
    row = _iota((LANES, LANES), 0)
    lane = _iota((LANES, LANES), 1)
    one_hot = lambda hit: jnp.where(hit, 1.0, 0.0).astype(BF16)
    first = GATE_COL // LANES
    last_full = w_ref.shape[1] // LANES

    def blk(d):
        return (tail_ref[...] if d >= last_full else w_ref[:, d * LANES:(d + 1) * LANES]).astype(BF16)

    stack = lambda lo: jnp.concatenate([blk(lo + m) for m in range(N_SHIFTED)], axis=0)
    t = (_dot(stack(first), one_hot(row == lane + SHIFT))
         + _dot(stack(first + 1), one_hot(row == lane - (LANES - SHIFT)))).astype(BF16)
    shifted = lambda lo, n: jnp.concatenate([t[(lo + m) * tm:(lo + m + 1) * tm] for m in range(n)], axis=1)

    perm_ref[:, 0:GATE_COL] = w_ref[:, 0:GATE_COL].astype(BF16)
    perm_ref[:, GATE_COL:SEG_ML[1]] = shifted(0, 2)
    gates = blk(first)
    for p in range(2):
        src = jnp.where(lane < 2, lane + 2 * p, lane + 2 + 2 * p)
        pick = one_hot((row == src) & (lane < 4))
        perm_ref[:, SEG_IFC[0] + p * LANES:SEG_IFC[0] + (p + 1) * LANES] = _dot(gates, pick).astype(BF16)
    perm_ref[:, SEG_RET[0]:SEG_RET[1]] = shifted(2, 8)
    perm_ref[:, SEG_SB[0]:SEG_SB[1]] = shifted(10, 6)
    perm_ref[:, SEG_SWA[0]:SEG_SWA[0] + W_BRANCH] = shifted(16, 2)
    for n, m in enumerate((18, 19)):
        pair = shifted(m, 1)
        for h in range(2):
            dup = one_hot(row == h * HEAD_DIM + lane % HEAD_DIM)
            c0 = SEG_SWA[0] + W_BRANCH + (2 * n + h) * LANES
            perm_ref[:, c0:c0 + LANES] = _dot(pair, dup).astype(BF16)
    gate_ref[...] = shifted(20, N_SHIFTED - 20)


def _prep_w_in(w_in):
    depth, d, n_cols = w_in.shape
    tm = 128
    tail = jnp.pad(w_in[:, :, (n_cols // LANES) * LANES:], ((0, 0), (0, 0), (0, LANES - n_cols % LANES)))
    return pl.pallas_call(
        _w_in_kernel,
        grid=(depth, d // tm),
        in_specs=[pl.BlockSpec((None, tm, n_cols), lambda l, i: (l, i, 0)),
                  pl.BlockSpec((None, tm, LANES), lambda l, i: (l, i, 0))],
        out_specs=[pl.BlockSpec((None, tm, N_IN), lambda l, i: (l, i, 0)),
                   pl.BlockSpec((None, tm, N_HEADS * d), lambda l, i: (l, i, 0))],
        out_shape=[jax.ShapeDtypeStruct((depth, d, N_IN), BF16), jax.ShapeDtypeStruct((depth, d, N_HEADS * d), BF16)],
        compiler_params=_params(("parallel", "parallel")),
        name="prep_w_in",
    )(w_in, tail)


def _cast_kernel(x_ref, o_ref):
    o_ref[...] = x_ref[...].astype(BF16)


def _to_bf16(a):
    cols = a.shape[-1]
    rows = a.size // cols
    tm = 512
    spec = pl.BlockSpec((tm, cols), lambda i: (i, 0))
    out = pl.pallas_call(
        _cast_kernel,
        grid=(rows // tm,),
        in_specs=[spec],
        out_specs=spec,
        out_shape=jax.ShapeDtypeStruct((rows, cols), BF16),
        compiler_params=_params(("parallel",)),
        name="cast_bf16",
    )(a.reshape(rows, cols))
    return out.reshape(a.shape)


def _gate_rows(w_in):
    g = w_in[:, :, GATE_COL:GATE_COL + SHIFT]
    mi, mf = g[..., :N_HEADS], g[..., N_HEADS:]
    z4 = jnp.zeros_like(mi)
    ift = jnp.concatenate([mi[..., 0:2], mf[..., 0:2], z4, mi[..., 2:4], mf[..., 2:4], z4], axis=-1)
    return jnp.swapaxes(ift, 1, 2).astype(BF16)


def _gate_bias_layout(gate_b):
    ib, fb = gate_b[0], gate_b[1]
    z = jnp.zeros((LANES - 4,), F32)
    row = jnp.concatenate([ib[0:2], fb[0:2], z, ib[2:4], fb[2:4], z]).reshape(1, 2 * LANES)
    z4 = jnp.zeros((4,), F32)
    col = jnp.concatenate([ib[0:2], fb[0:2], z4, ib[2:4], fb[2:4], z4])
    return row, jnp.broadcast_to(col[:, None], (16, LANES))


def kernel(x, c, positions, w_ada, b_ada, norm_g, w_in, mlstm_conv_w, mlstm_conv_b, mlstm_gate_b,
           ret_norm_g, swa_q_norm_g, swa_k_norm_g, swa_sinks, rel_bias, w_up, w_out, w_ff1, w_ff2):
    batch, seq, d = x.shape
    depth = w_in.shape[0]
    t = batch * seq
    nb = seq // CHUNK

    c8 = jnp.concatenate([c, jnp.zeros((8 - batch, d), F32)], axis=0)
    mod = _ada(c8, w_ada.reshape(depth * 2, d, 3 * d), b_ada.reshape(depth * 2, 1, 3 * d))
    mod = mod[:, :batch].reshape(depth, 2, batch, 3, 1, d)

    pos_col = jnp.broadcast_to(positions.reshape(t, 1), (t, LANES))
    pos_row = positions.reshape(batch * nb, 1, CHUNK)
    half = HEAD_DIM // 2
    inv = ROPE_BASE ** (-(np.arange(LANES) % half).astype(np.float64) / half)
    cos_t, sin_t = _rope_tables(pos_col, jnp.asarray(inv, F32).reshape(1, LANES))
    bias_tab = jnp.concatenate([rel_bias.T, jnp.zeros((N_HEADS, LANES - N_BUCKETS), F32)], axis=1)
    bias = _bias_tiles(bias_tab, pos_col, pos_row, batch, nb)

    log_gamma = np.log(1.0 - np.exp2(-(RET_DECAY_BASE + np.arange(N_HEADS, dtype=np.float64))))
    lg_tab = jnp.asarray(np.broadcast_to(np.repeat(log_gamma, HEAD_DIM).reshape(2, 1, LANES), (2, 8, LANES)), F32)

    w_perm, w_gate = _prep_w_in(w_in)
    w_ift = _gate_rows(w_in)
    w_up_bf, w_out_bf, w_ff1_bf, w_ff2_bf = (_to_bf16(w) for w in (w_up, w_out, w_ff1, w_ff2))

    xt = x.reshape(t, d)
    for l in range(depth):
        gb_row, gbt = _gate_bias_layout(mlstm_gate_b[l])
        g1 = norm_g[l, 0].reshape(1, d)
        g2 = norm_g[l, 1].reshape(1, d)
        shift1, scale1, gate1 = mod[l, 0, :, 0], mod[l, 0, :, 1], mod[l, 0, :, 2]
        shift2, scale2, gate2 = mod[l, 1, :, 0], mod[l, 1, :, 1], mod[l, 1, :, 2]

        ml, ifc, ret, sb, swa, ift = _in_proj(xt, g1, scale1, shift1, w_perm, w_ift, gb_row, gbt, l, batch, seq)
        out_ml = _mlstm(ml, ifc, ift, mlstm_conv_w[l], mlstm_conv_b[l].reshape(1, 2 * W_BRANCH), batch, seq)
        out_ret = _retention(ret, cos_t, sin_t, lg_tab, ret_norm_g[l].reshape(1, W_BRANCH), batch, seq)
        out_sb = _stick_breaking(sb, batch, seq)
        qg = jnp.tile(swa_q_norm_g[l], 2).reshape(1, LANES)
        kg = jnp.tile(swa_k_norm_g[l], 2).reshape(1, LANES)
        out_swa = _swa(swa, bias, swa_sinks[l], qg, kg, batch, seq)

        branches = tuple(o.reshape(t, W_BRANCH) for o in (out_ml, out_ret, out_sb, out_swa))
        xt = _merge(xt, g1, scale1, shift1, gate1, branches, w_gate, w_up_bf, w_out_bf, l, seq)
        xt = _ffn(xt, g2, scale2, shift2, gate2, w_ff1_bf, w_ff2_bf, l, seq)
    return xt.reshape(batch, seq, d)
```

```python
import functools
import math

import numpy as np
import jax
import jax.numpy as jnp
from jax import lax
from jax.experimental import pallas as pl
from jax.experimental.pallas import tpu as pltpu

F32 = jnp.float32
BF16 = jnp.bfloat16

D_MODEL = 1024
HEAD_DIM = 64
N_HEADS = 4
W_BRANCH = N_HEADS * HEAD_DIM
LANES = 128
CHUNK = 128
CONV_K = 4
D_FF = 4 * D_MODEL
N_BUCKETS = 32
MAX_DIST = 128
ROPE_BASE = 10000.0
RET_DECAY_BASE = 5.0
EPS = 1e-6
NEG = -1e30
F32_EXP_UNDERFLOW = -104.0
VMEM_LIMIT = 56 * 1024 * 1024

SEG_ML = (0, 1024)
SEG_IFC = (1024, 1280)
SEG_RET = (1280, 2304)
SEG_SB = (2304, 3072)
SEG_SWA = (3072, 3840)
N_IN = 3840


def _dot(a, b):
    return jnp.dot(a, b, preferred_element_type=F32)


def _dot_nt(a, b):
    return lax.dot_general(a, b, (((1,), (1,)), ((), ())), preferred_element_type=F32)


def _dot_tn(a, b):
    return lax.dot_general(a, b, (((0,), (0,)), ((), ())), preferred_element_type=F32)


def _split2(x):
    hi = x.astype(BF16)
    lo = (x - hi.astype(F32)).astype(BF16)
    return hi, lo


def _split3(x):
    hi = x.astype(BF16)
    r = x - hi.astype(F32)
    mid = r.astype(BF16)
    lo = (r - mid.astype(F32)).astype(BF16)
    return hi, mid, lo


def _iota(shape, axis):
    return lax.broadcasted_iota(jnp.int32, shape, axis)


def _log_sigmoid(x):
    return jnp.minimum(x, 0.0) - jnp.log(1.0 + jnp.exp(-jnp.abs(x)))


def _sigmoid(x):
    return 1.0 / (1.0 + jnp.exp(-x))


def _silu(x):
    return x * _sigmoid(x)


def _norm_mod(x, g, scale, shift):
    ms = jnp.mean(x * x, axis=-1, keepdims=True)
    y = x * lax.rsqrt(ms + EPS)
    return (y * g) * (1.0 + scale) + shift


def _head_mean_sq(x):
    lane = _iota(x.shape, 1)
    sq = x * x
    s0 = jnp.sum(jnp.where(lane < HEAD_DIM, sq, 0.0), axis=-1, keepdims=True)
    s1 = jnp.sum(jnp.where(lane >= HEAD_DIM, sq, 0.0), axis=-1, keepdims=True)
    return jnp.where(lane < HEAD_DIM, s0, s1) * (1.0 / HEAD_DIM)


def _params(sem):
    return pltpu.CompilerParams(dimension_semantics=sem, vmem_limit_bytes=VMEM_LIMIT)


def _interleave(chains):
    live = list(chains)
    while live:
        still = []
        for ch in live:
            try:
                next(ch)
                still.append(ch)
            except StopIteration:
                pass
        live = still


def _resident(a):
    return pl.BlockSpec(a.shape, lambda *_: (0,) * a.ndim, pipeline_mode=pl.Buffered(1))


def _layer_of(a, layer):
    return pl.BlockSpec((None,) + a.shape[1:], lambda *_: (layer,) + (0,) * (a.ndim - 1),
                        pipeline_mode=pl.Buffered(1))


def _ada_kernel(c_ref, w_ref, b_ref, o_ref):
    c = c_ref[...]
    ch, cl = _split2(_silu(c))
    wh, wl = _split2(w_ref[0])
    o_ref[0] = _dot(ch, wh) + _dot(ch, wl) + _dot(cl, wh) + b_ref[0]


def _ada(c8, w_ada, b_ada):
    n_mod, d, n3 = w_ada.shape
    tn = 1024
    return pl.pallas_call(
        _ada_kernel,
        grid=(n_mod, n3 // tn),
        in_specs=[pl.BlockSpec((8, d), lambda m, n: (0, 0)),
                  pl.BlockSpec((1, d, tn), lambda m, n: (m, 0, n)),
                  pl.BlockSpec((1, 1, tn), lambda m, n: (m, 0, n))],
        out_specs=pl.BlockSpec((1, 8, tn), lambda m, n: (m, 0, n)),
        out_shape=jax.ShapeDtypeStruct((n_mod, 8, n3), F32),
        compiler_params=_params(("parallel", "parallel")),
        name="ada_mod",
    )(c8, w_ada, b_ada)


def _rope_kernel(pos_ref, inv_ref, cos_ref, sin_ref):
    ang = pos_ref[...].astype(F32) * inv_ref[...]
    cos_ref[...] = jnp.cos(ang)
    sin_ref[...] = jnp.sin(ang)


def _rope_tables(pos_b, inv_row):
    t = pos_b.shape[0]
    tm = 1024
    spec = pl.BlockSpec((tm, LANES), lambda i: (i, 0))
    return pl.pallas_call(
        _rope_kernel,
        grid=(t // tm,),
        in_specs=[spec, pl.BlockSpec((1, LANES), lambda i: (0, 0))],
        out_specs=[spec, spec],
        out_shape=[jax.ShapeDtypeStruct((t, LANES), F32)] * 2,
        compiler_params=_params(("parallel",)),
        name="rope_tables",
    )(pos_b, inv_row)


BIAS_BLOCKS = 4


def _bias_kernel(tab_ref, posq_ref, posc_ref, posp_ref, o_ref):
    max_exact = N_BUCKETS // 2
    rows = [jnp.broadcast_to(tab_ref[h:h + 1, :], (CHUNK, LANES)) for h in range(N_HEADS)]
    for r in range(BIAS_BLOCKS):
        pos_q = posq_ref[r * CHUNK:(r + 1) * CHUNK, :]
        pos_prev = posp_ref[0] if r == 0 else posc_ref[r - 1]
        rel = jnp.concatenate([pos_q - pos_prev, pos_q - posc_ref[r]], axis=1)
        n = jnp.maximum(rel, 0)
        nf = jnp.maximum(n, 1).astype(F32)
        large = max_exact + (jnp.log(nf / max_exact) / math.log(MAX_DIST / max_exact)
                             * (N_BUCKETS - max_exact)).astype(jnp.int32)
        large = jnp.minimum(large, N_BUCKETS - 1)
        bucket = jnp.where(n < max_exact, n, large)
        halves = (bucket[:, :CHUNK], bucket[:, CHUNK:])
        for h in range(N_HEADS):
            o_ref[r, h] = jnp.concatenate([jnp.take_along_axis(rows[h], idx, axis=1) for idx in halves], axis=1)


def _bias_tiles(rel_bias, pos_b, pos_r, batch, n_b):
    nblk = batch * n_b
    return pl.pallas_call(
        _bias_kernel,
        grid=(nblk // BIAS_BLOCKS,),
        in_specs=[pl.BlockSpec((N_HEADS, LANES), lambda i: (0, 0)),
                  pl.BlockSpec((BIAS_BLOCKS * CHUNK, LANES), lambda i: (i, 0)),
                  pl.BlockSpec((BIAS_BLOCKS, 1, CHUNK), lambda i: (i, 0, 0)),
                  pl.BlockSpec((1, 1, CHUNK), lambda i: (jnp.maximum(i * BIAS_BLOCKS - 1, 0), 0, 0))],
        out_specs=pl.BlockSpec((BIAS_BLOCKS, N_HEADS, CHUNK, 2 * CHUNK), lambda i: (i, 0, 0, 0)),
        out_shape=jax.ShapeDtypeStruct((nblk, N_HEADS, CHUNK, 2 * CHUNK), F32),
        compiler_params=_params(("parallel",)),
        name="t5_bias",
    )(rel_bias, pos_b, pos_r, pos_r)


def _in_kernel(x_ref, g_ref, sc_ref, sh_ref, w_ref, wt_ref, gb_ref, gbt_ref,
               ml_ref, ifc_ref, ret_ref, sb_ref, swa_ref, ift_ref):
    h = _norm_mod(x_ref[...], g_ref[...], sc_ref[0], sh_ref[0]).astype(BF16)

    def seg(s):
        return _dot(h, w_ref[:, s[0]:s[1]])

    ml_ref[...] = seg(SEG_ML)
    ifc_ref[...] = seg(SEG_IFC) + gb_ref[...]
    ret_ref[...] = seg(SEG_RET)
    sb_ref[...] = seg(SEG_SB).astype(BF16)
    swa_ref[...] = seg(SEG_SWA)
    ift_ref[0] = _dot_nt(wt_ref[...], h) + gbt_ref[:, 0:1]


def _in_proj(x, g, scale, shift, w_perm, w_ift, gb_row, gbt, layer, batch, seq):
    t, d = x.shape
    tm = 512
    tpb = seq // tm
    row = lambda n: pl.BlockSpec((tm, n), lambda i: (i, 0))
    mod = pl.BlockSpec((1, 1, d), lambda i: (i // tpb, 0, 0))
    widths = [s[1] - s[0] for s in (SEG_ML, SEG_IFC, SEG_RET, SEG_SB, SEG_SWA)]
    dts = [F32, F32, F32, BF16, F32]
    return pl.pallas_call(
        _in_kernel,
        grid=(t // tm,),
        in_specs=[row(d), _resident(g), mod, mod, _layer_of(w_perm, layer), _layer_of(w_ift, layer), _resident(gb_row),
                  _resident(gbt)],
        out_specs=[row(n) for n in widths] + [pl.BlockSpec((1, 16, tm), lambda i: (i // tpb, 0, i % tpb))],
        out_shape=[jax.ShapeDtypeStruct((t, n), dt) for n, dt in zip(widths, dts)]
        + [jax.ShapeDtypeStruct((batch, 16, seq), F32)],
        compiler_params=_params(("parallel",)),
        name="in_proj",
    )(x, g, scale, shift, w_perm, w_ift, gb_row, gbt)


def _ml_kernel(ml_ref, ifc_ref, ift_ref, cw_ref, cb_ref, out_ref, halo, c_st, m_st):
    L = CHUNK
    HALO = 8
    batch = ml_ref.shape[0]

    @pl.when(pl.program_id(0) == 0)
    def _init():
        halo[:, 0:HALO, :] = jnp.zeros((halo.shape[0], HALO, LANES), F32)
        c_st[...] = jnp.zeros(c_st.shape, F32)
        m_st[...] = jnp.zeros(m_st.shape, F32)

    lane = _iota((L, LANES), 1)
    row = _iota((L, LANES), 0)
    causal = row >= lane
    tri = jnp.where(causal, 1.0, 0.0).astype(BF16)
    upper = jnp.where(row <= lane, 1.0, 0.0).astype(BF16)
    eye = jnp.where(row == lane, 1.0, 0.0).astype(BF16)
    spread = jnp.where(_iota((LANES, 4 * LANES), 0) == _iota((LANES, 4 * LANES), 1) // LANES,
                       1.0, 0.0).astype(BF16)

    def conv_silu(b, col, slot):
        hl = halo.at[slot]
        hl[HALO:HALO + L, :] = ml_ref[b, :, col:col + LANES]
        acc = jnp.zeros((L, LANES), F32) + cb_ref[:, col:col + LANES]
        for j in range(CONV_K):
            off = HALO - (CONV_K - 1) + j
            acc = acc + hl[off:off + L, :] * cw_ref[j:j + 1, col:col + LANES]
        hl[0:HALO, :] = hl[L:L + HALO, :]
        return _silu(acc)

    def pair_chain(b, p):
        pc = p * LANES
        ift = ift_ref[b, p * 8:p * 8 + 8, :]
        lr_h, lr_m, lr_l = _split3(_log_sigmoid(ift))
        a_rows = _dot(lr_h, upper) + _dot(lr_m, upper) + _dot(lr_l, upper)
        g_h, g_m, g_l = _split3(ifc_ref[b, :, pc:pc + LANES])
        gates_b = _dot(g_h, spread) + _dot(g_m, spread) + _dot(g_l, spread)
        q = conv_silu(b, pc, (b * 2 + p) * 2)
        k = conv_silu(b, W_BRANCH + pc, (b * 2 + p) * 2 + 1)
        v = ml_ref[b, :, 2 * W_BRANCH + pc:2 * W_BRANCH + pc + LANES]
        k_bf = k.astype(BF16)
        yield

        lf_h, lf_m, lf_l = _split3(_log_sigmoid(gates_b[:, 2 * LANES:]))
        a_b = _dot(tri, lf_h) + _dot(tri, lf_m) + _dot(tri, lf_l)
        heads = []
        for hh in range(2):
            ch = (b * 2 + p) * 2 + hh
            hm = (lane >= HEAD_DIM) if hh else (lane < HEAD_DIM)
            qh = (jnp.where(hm, q, 0.0) * (HEAD_DIM ** -0.5)).astype(BF16)
            c_in = c_st[ch]
            heads.append(dict(ch=ch, hm=hm, c_in=c_in, s_qk=_dot_nt(qh, k_bf), q_c=_dot(qh, c_in.astype(BF16))))
        yield

        for hh, hd in enumerate(heads):
            i_col = gates_b[:, hh * LANES:(hh + 1) * LANES]
            a_col = a_b[:, hh * LANES:(hh + 1) * LANES]
            a_last = a_col[L - 1:L, :]
            m_in = m_st[hd["ch"], 0:1, :]
            v_aug = jnp.where(hd["hm"], v, 1.0).astype(BF16)

            d_intra = jnp.where(causal, a_col - (a_rows[2 + hh:3 + hh, :] - ift[hh:hh + 1, :]), NEG)
            m_intra = jnp.max(d_intra, axis=-1, keepdims=True)
            m_inter = a_col + m_in
            m_tot = jnp.maximum(m_inter, m_intra)
            pm = jnp.exp(d_intra - m_tot) * hd["s_qk"]
            hd["p_v"] = _dot(pm.astype(BF16), v_aug)
            hd["s_inter"] = jnp.exp(m_inter - m_tot)
            hd["floor"] = jnp.exp(-m_tot)

            g_col = a_last - a_col + i_col
            m_loc = jnp.max(g_col, axis=0, keepdims=True)
            kw = (jnp.where(hd["hm"], k, 0.0) * jnp.exp(g_col - m_loc)).astype(BF16)
            hd["kw_t"] = _dot_nt(eye, kw)
            hd["v_aug"] = v_aug
            m_new = jnp.maximum(a_last + m_in, m_loc)
            hd["w_old"] = jnp.exp(a_last + m_in - m_new)
            hd["w_new"] = jnp.exp(m_loc - m_new)
            m_st[hd["ch"]] = jnp.zeros((8, LANES), F32) + m_new
        yield

        for hd in heads:
            hd["c_loc"] = _dot(hd["kw_t"].astype(BF16), hd["v_aug"])
            hd["o_aug"] = hd["s_inter"] * hd["q_c"] + hd["p_v"]
        den = pltpu.roll(jnp.where(lane < HEAD_DIM, heads[1]["o_aug"], heads[0]["o_aug"]), HEAD_DIM, 1)
        num = jnp.where(lane < HEAD_DIM, heads[0]["o_aug"], heads[1]["o_aug"])
        floor = jnp.where(lane < HEAD_DIM, heads[0]["floor"], heads[1]["floor"])
        h_ml = num / jnp.maximum(jnp.abs(den), floor)
        o_gate = ml_ref[b, :, 3 * W_BRANCH + pc:3 * W_BRANCH + pc + LANES]
        out_ref[b, :, pc:pc + LANES] = (h_ml * _sigmoid(o_gate)).astype(BF16)
        yield

        for hd in heads:
            c_st[hd["ch"]] = hd["w_old"] * hd["c_in"] + hd["w_new"] * hd["c_loc"]

    _interleave(pair_chain(b, p) for b in range(batch) for p in range(2))


def _mlstm(ml, ifc, ift, conv_w, conv_b, batch, seq):
    nc = seq // CHUNK
    n_chain = batch * N_HEADS
    step = lambda n: pl.BlockSpec((batch, CHUNK, n), lambda c: (0, c, 0))
    return pl.pallas_call(
        _ml_kernel,
        grid=(nc,),
        in_specs=[step(4 * W_BRANCH), step(2 * LANES), pl.BlockSpec((batch, 16, CHUNK), lambda c: (0, 0, c)),
                  _resident(conv_w), _resident(conv_b)],
        out_specs=step(W_BRANCH),
        out_shape=jax.ShapeDtypeStruct((batch, seq, W_BRANCH), BF16),
        scratch_shapes=[pltpu.VMEM((n_chain, CHUNK + 8, LANES), F32),
                        pltpu.VMEM((n_chain, LANES, LANES), F32), pltpu.VMEM((n_chain, 8, LANES), F32)],
        compiler_params=_params(("arbitrary",)),
        name="mlstm",
    )(ml.reshape(batch, seq, -1), ifc.reshape(batch, seq, -1), ift, conv_w, conv_b)


def _ret_kernel(ret_ref, cos_ref, sin_ref, lg_ref, ng_ref, out_ref, st):
    L = CHUNK
    batch = ret_ref.shape[0]

    @pl.when(pl.program_id(0) == 0)
    def _init():
        st[...] = jnp.zeros(st.shape, F32)

    lane = _iota((L, LANES), 1)
    row = _iota((L, LANES), 0)
    rowf = row.astype(F32)
    relf = jnp.maximum(row - lane, 0).astype(F32)
    first = (lane % HEAD_DIM) < (HEAD_DIM // 2)
    same_head = (row // HEAD_DIM) == (lane // HEAD_DIM)

    def rot(t, cos, sin):
        partner = jnp.where(first, -pltpu.roll(t, LANES - HEAD_DIM // 2, 1), pltpu.roll(t, HEAD_DIM // 2, 1))
        return t * cos + partner * sin

    decays = []
    for p in range(2):
        lg = lg_ref[p, 0:1, :]
        decays.append(dict(
            q=jnp.exp((rowf + 1.0) * lg), k=jnp.exp((L - 1.0 - rowf) * lg), c=jnp.exp(float(L) * lg),
            intra=[jnp.where(row >= lane, jnp.exp(relf * lg[:, hh * HEAD_DIM:hh * HEAD_DIM + 1]), 0.0)
                   for hh in range(2)]))

    def pair_chain(b, p):
        pc = p * LANES
        dec = decays[p]
        cos = cos_ref[b]
        sin = sin_ref[b]
        q = rot(ret_ref[b, :, pc:pc + LANES], cos, sin)
        k = rot(ret_ref[b, :, W_BRANCH + pc:W_BRANCH + pc + LANES], cos, sin) * (HEAD_DIM ** -0.5)
        v_bf = ret_ref[b, :, 2 * W_BRANCH + pc:2 * W_BRANCH + pc + LANES].astype(BF16)
        k_bf = k.astype(BF16)
        state = st[b * 2 + p]
        s_qk = [_dot_nt(jnp.where((lane >= HEAD_DIM) if hh else (lane < HEAD_DIM), q, 0.0).astype(BF16), k_bf)
                for hh in range(2)]
        inter = _dot(q.astype(BF16), state.astype(BF16))
        kv = _dot_tn((k * dec["k"]).astype(BF16), v_bf)
        yield

        intra = [_dot((s_qk[hh] * dec["intra"][hh]).astype(BF16), v_bf) for hh in range(2)]
        st[b * 2 + p] = dec["c"] * state + jnp.where(same_head, kv, 0.0)
        yield

        o = jnp.where(lane < HEAD_DIM, intra[0], intra[1]) + inter * dec["q"]
        y = o * lax.rsqrt(_head_mean_sq(o) + EPS) * ng_ref[:, pc:pc + LANES]
        gate = ret_ref[b, :, 3 * W_BRANCH + pc:3 * W_BRANCH + pc + LANES]
        out_ref[b, :, pc:pc + LANES] = (y * _silu(gate)).astype(BF16)

    _interleave(pair_chain(b, p) for b in range(batch) for p in range(2))


def _retention(ret, cos_t, sin_t, lg_tab, ret_g, batch, seq):
    nc = seq // CHUNK
    step = lambda n: pl.BlockSpec((batch, CHUNK, n), lambda c: (0, c, 0))
    return pl.pallas_call(
        _ret_kernel,
        grid=(nc,),
        in_specs=[step(4 * W_BRANCH), step(LANES), step(LANES), _resident(lg_tab), _resident(ret_g)],
        out_specs=step(W_BRANCH),
        out_shape=jax.ShapeDtypeStruct((batch, seq, W_BRANCH), BF16),
        scratch_shapes=[pltpu.VMEM((batch * 2, LANES, LANES), F32)],
        compiler_params=_params(("arbitrary",)),
        name="retention",
    )(ret.reshape(batch, seq, -1), cos_t.reshape(batch, seq, LANES), sin_t.reshape(batch, seq, LANES), lg_tab, ret_g)


def _sb_kernel(q_ref, k_ref, v_ref, out_ref, qh_s, carry_s, acc_s):
    L = CHUNK
    batch = q_ref.shape[0]
    i = pl.program_id(0)
    lane = _iota((L, LANES), 1)
    row = _iota((L, LANES), 0)
    chains = [(b, p, hh) for b in range(batch) for p in range(2) for hh in range(2)]

    r2 = _iota((2 * L, 2 * L), 0) % L
    c2 = _iota((2 * L, 2 * L), 1)
    suffix_op = jnp.where((c2 >= L) | (r2 >= c2), 1.0, 0.0).astype(BF16)

    def tile_chain(c, b, p, tiles, fresh):
        pc = p * LANES
        zs = [_dot_nt(qh_s[c], k_ref[b, pl.ds(off, L), pc:pc + LANES]) for off, _ in tiles]
        yield
        rrs = []
        for z, (_, mask) in zip(zs, tiles):
            lk = -(jnp.maximum(z, 0.0) + jnp.log(1.0 + jnp.exp(-jnp.abs(z))))
            if mask is not None:
                lk = jnp.where(mask, lk, 0.0)
            hi, lo = _split2(lk)
            rrs.append(_dot(jnp.concatenate([hi, lo], axis=1), suffix_op))
        yield
        carry = None if fresh else carry_s[c]
        parts = []
        for z, rr, (off, mask) in zip(zs, rrs, tiles):
            e = z + rr[:, :L]
            w = jnp.exp(e if carry is None else e + carry)
            if mask is not None:
                w = jnp.where(mask, w, 0.0)
            parts.append(_dot(w.astype(BF16), v_ref[b, pl.ds(off, L), pc:pc + LANES]))
            carry = rr[:, L:] if carry is None else carry + rr[:, L:]
        carry_s[c] = carry
        yield
        acc = None if fresh else acc_s[c]
        for part in parts:
            acc = part if acc is None else acc + part
        acc_s[c] = acc

    def walk(tiles, fresh):
        _interleave(tile_chain(c, b, p, tiles, fresh) for c, (b, p, hh) in enumerate(chains))

    for c, (b, p, hh) in enumerate(chains):
        hm = (lane >= HEAD_DIM) if hh else (lane < HEAD_DIM)
        qh_s[c] = jnp.where(hm, q_ref[b, :, p * LANES:(p + 1) * LANES] * (HEAD_DIM ** -0.5), 0.0).astype(BF16)

    FUSED_BLOCKS = 3
    strict = lane < row

    def head_tiles(n):
        return [(pl.multiple_of((i - d) * L, L), strict if d == 0 else None) for d in range(n)]

    for n in range(1, FUSED_BLOCKS):
        pl.when(i == n - 1)(functools.partial(walk, head_tiles(n), True))
    pl.when(i >= FUSED_BLOCKS - 1)(functools.partial(walk, head_tiles(FUSED_BLOCKS), True))

    def all_underflow():
        return (jnp.max(jnp.max(carry_s[...], axis=0)) <= F32_EXP_UNDERFLOW).astype(jnp.int32)

    def cond(st):
        return jnp.logical_and(st[0] >= 0, st[1] == 0)

    def body(st):
        walk([(pl.multiple_of(st[0] * L, L), None)], False)
        return (st[0] - 1, all_underflow())

    lax.while_loop(cond, body, (i - FUSED_BLOCKS, all_underflow()))
    for b in range(batch):
        for p in range(2):
            c0 = (b * 2 + p) * 2
            out_ref[b, :, p * LANES:(p + 1) * LANES] = jnp.where(lane < HEAD_DIM, acc_s[c0], acc_s[c0 + 1]).astype(BF16)


def _stick_breaking(sb, batch, seq):
    nb = seq // CHUNK
    n_chain = batch * N_HEADS
    sb3 = sb.reshape(batch, seq, -1)
    kv = lambda blk: pl.BlockSpec((batch, seq, W_BRANCH), lambda i: (0, 0, blk), pipeline_mode=pl.Buffered(1))
    return pl.pallas_call(
        _sb_kernel,
        grid=(nb,),
        in_specs=[pl.BlockSpec((batch, CHUNK, W_BRANCH), lambda i: (0, i, 0)), kv(1), kv(2)],
        out_specs=pl.BlockSpec((batch, CHUNK, W_BRANCH), lambda i: (0, i, 0)),
        out_shape=jax.ShapeDtypeStruct((batch, seq, W_BRANCH), BF16),
        scratch_shapes=[pltpu.VMEM((n_chain, CHUNK, LANES), BF16), pltpu.VMEM((n_chain, CHUNK, LANES), F32),
                        pltpu.VMEM((n_chain, CHUNK, LANES), F32)],
        compiler_params=_params(("arbitrary",)),
        name="stick_breaking",
    )(sb3, sb3, sb3)


def _swa_kernel(sink_ref, q_ref, kc_ref, kp_ref, vc_ref, vp_ref, bias_ref, qg_ref, kg_ref, out_ref):
    L = CHUNK
    batch = q_ref.shape[0]
    i = pl.program_id(0)
    lane = _iota((L, LANES), 1)
    t = _iota((L, 2 * L), 0)
    j = _iota((L, 2 * L), 1)
    dist = t + L - j
    valid = (dist >= 0) & (dist < L) & ((j >= L) | (i > 0))

    def qk_norm(x, g_ref):
        return x * lax.rsqrt(_head_mean_sq(x) + EPS) * g_ref[...]

    def group_chain(b, g):
        sl = slice(g * LANES, (g + 1) * LANES)
        qn = qk_norm(q_ref[b, :, sl], qg_ref) * (HEAD_DIM ** -0.5)
        kcat = jnp.concatenate([qk_norm(kp_ref[b, :, sl], kg_ref), qk_norm(kc_ref[b, :, sl], kg_ref)],
                               axis=0).astype(BF16)
        qk = [_dot_nt(jnp.where((lane >= HEAD_DIM) if r else (lane < HEAD_DIM), qn, 0.0).astype(BF16), kcat)
              for r in range(2)]
        yield
        vcat = jnp.concatenate([vp_ref[b, :, sl], vc_ref[b, :, sl]], axis=0).astype(BF16)
        outs = []
        for r in range(2):
            hq = 2 * g + r
            logits = jnp.where(valid, qk[r] + bias_ref[b, 0, hq], NEG)
            sink = sink_ref[hq]
            m = jnp.maximum(jnp.max(logits, axis=-1, keepdims=True), sink)
            pr = jnp.exp(logits - m)
            den = jnp.sum(pr, axis=-1, keepdims=True) + jnp.exp(sink - m)
            outs.append(_dot((pr / den).astype(BF16), vcat))
        yield
        out_ref[b, :, sl] = jnp.where(lane < HEAD_DIM, outs[0], outs[1]).astype(BF16)

    _interleave(group_chain(b, g) for b in range(batch) for g in range(2))


def _swa(swa, bias, sinks, q_g, k_g, batch, seq):
    nb = seq // CHUNK
    swa3 = swa.reshape(batch, seq, -1)
    cur = lambda blk: pl.BlockSpec((batch, CHUNK, W_BRANCH), lambda i: (0, i, blk))
    prev = lambda blk: pl.BlockSpec((batch, CHUNK, W_BRANCH), lambda i: (0, jnp.maximum(i - 1, 0), blk))
    return pl.pallas_call(
        _swa_kernel,
        grid=(nb,),
        in_specs=[pl.BlockSpec(memory_space=pltpu.SMEM), cur(0), cur(1), prev(1), cur(2), prev(2),
                  pl.BlockSpec((batch, 1, N_HEADS, CHUNK, 2 * CHUNK), lambda i: (0, i, 0, 0, 0)),
                  _resident(q_g), _resident(k_g)],
        out_specs=pl.BlockSpec((batch, CHUNK, W_BRANCH), lambda i: (0, i, 0)),
        out_shape=jax.ShapeDtypeStruct((batch, seq, W_BRANCH), BF16),
        compiler_params=_params(("parallel",)),
        name="swa",
    )(sinks, swa3, swa3, swa3, swa3, swa3, bias.reshape(batch, nb, N_HEADS, CHUNK, 2 * CHUNK), q_g, k_g)


def _merge_kernel(x_ref, g_ref, sc_ref, sh_ref, gt_ref, b0_ref, b1_ref, b2_ref, b3_ref,
                  wg_ref, wu_ref, wo_ref, out_ref):
    x = x_ref[...]
    h = _norm_mod(x, g_ref[...], sc_ref[0], sh_ref[0]).astype(BF16)
    merged = None
    for n, b_ref in enumerate((b0_ref, b1_ref, b2_ref, b3_ref)):
        gate = _sigmoid(_dot(h, wg_ref[:, n * D_MODEL:(n + 1) * D_MODEL]))
        term = gate * _dot(b_ref[...], wu_ref[n])
        merged = term if merged is None else merged + term
    mix = _dot(merged.astype(BF16), wo_ref[...])
    out_ref[...] = x + gt_ref[0] * mix


def _merge(x, g, scale, shift, gate, branches, w_gate, w_up, w_out, layer, seq):
    t, d = x.shape
    tm = 256
    tpb = seq // tm
    row = lambda n: pl.BlockSpec((tm, n), lambda i: (i, 0))
    mod = pl.BlockSpec((1, 1, d), lambda i: (i // tpb, 0, 0))
    return pl.pallas_call(
        _merge_kernel,
        grid=(t // tm,),
        in_specs=[row(d), _resident(g), mod, mod, mod] + [row(W_BRANCH)] * 4
        + [_layer_of(w_gate, layer), _layer_of(w_up, layer), _layer_of(w_out, layer)],
        out_specs=row(d),
        out_shape=jax.ShapeDtypeStruct((t, d), F32),
        compiler_params=_params(("parallel",)),
        name="merge",
    )(x, g, scale, shift, gate, *branches, w_gate, w_up, w_out)


def _ffn_kernel(x_ref, g_ref, sc_ref, sh_ref, gt_ref, w1_ref, w2_ref, out_ref):
    x = x_ref[...]
    h = _norm_mod(x, g_ref[...], sc_ref[0], sh_ref[0]).astype(BF16)
    tf = 1024
    acc = None
    for n in range(D_FF // tf):
        a = jnp.maximum(_dot(h, w1_ref[:, n * tf:(n + 1) * tf]), 0.0)
        part = _dot((a * a).astype(BF16), w2_ref[n * tf:(n + 1) * tf, :])
        acc = part if acc is None else acc + part
    out_ref[...] = x + gt_ref[0] * acc


def _ffn(x, g, scale, shift, gate, w1, w2, layer, seq):
    t, d = x.shape
    tm = 256
    tpb = seq // tm
    row = pl.BlockSpec((tm, d), lambda i: (i, 0))
    mod = pl.BlockSpec((1, 1, d), lambda i: (i // tpb, 0, 0))
    return pl.pallas_call(
        _ffn_kernel,
        grid=(t // tm,),
        in_specs=[row, _resident(g), mod, mod, mod, _layer_of(w1, layer), _layer_of(w2, layer)],
        out_specs=row,
        out_shape=jax.ShapeDtypeStruct((t, d), F32),
        compiler_params=_params(("parallel",)),
        name="ffn",
    )(x, g, scale, shift, gate, w1, w2)


GATE_COL = 3 * W_BRANCH


def _w_in_kernel(wt_ref, perm_ref, gate_ref, ift_ref):
    one_hot = lambda hit: jnp.where(hit, 1.0, 0.0).astype(BF16)
    eye = one_hot(_iota((LANES, LANES), 0) == _iota((LANES, LANES), 1))
    transpose = lambda rows_bf: _dot_nt(eye, rows_bf).astype(BF16)
    cols = lambda lo, n: transpose(wt_ref[lo:lo + n, :].astype(BF16))
    o_out = GATE_COL + 2 * N_HEADS
    o_ret = o_out + W_BRANCH
    o_sb = o_ret + 4 * W_BRANCH
    o_swa = o_sb + 3 * W_BRANCH
    o_gate = o_swa + 2 * W_BRANCH

    perm_ref[:, 0:GATE_COL] = cols(0, GATE_COL)
    perm_ref[:, GATE_COL:SEG_ML[1]] = cols(o_out, W_BRANCH)
    perm_ref[:, SEG_RET[0]:SEG_RET[1]] = cols(o_ret, 4 * W_BRANCH)
    perm_ref[:, SEG_SB[0]:SEG_SB[1]] = cols(o_sb, 3 * W_BRANCH)
    perm_ref[:, SEG_SWA[0]:SEG_SWA[0] + W_BRANCH] = cols(o_swa, W_BRANCH)
    for n in range(2):
        for h in range(2):
            lo = o_swa + W_BRANCH + n * 2 * HEAD_DIM + h * HEAD_DIM
            head = wt_ref[lo:lo + HEAD_DIM, :].astype(BF16)
            c0 = SEG_SWA[0] + W_BRANCH + (2 * n + h) * LANES
            perm_ref[:, c0:c0 + LANES] = transpose(jnp.concatenate([head, head], axis=0))
    gate_ref[...] = cols(o_gate, gate_ref.shape[1])

    block = wt_ref[GATE_COL:GATE_COL + LANES, :].astype(BF16)
    def placement(n_dst, period):
        dst = _iota((n_dst, LANES), 0)
        j = dst % period
        src = jnp.where(j < 2, j, j + 2) + 2 * (dst // period)
        return one_hot((_iota((n_dst, LANES), 1) == src) & (j < 4))
    ift_ref[...] = _dot(placement(16, 8), block).astype(BF16)
    perm_ref[:, SEG_IFC[0]:SEG_IFC[1]] = transpose(_dot(placement(2 * LANES, LANES), block).astype(BF16))


def _prep_w_in(w_in):
    depth, d, n_cols = w_in.shape
    tk = LANES
    return pl.pallas_call(
        _w_in_kernel,
        grid=(depth, d // tk),
        in_specs=[pl.BlockSpec((None, n_cols, tk), lambda l, i: (l, 0, i))],
        out_specs=[pl.BlockSpec((None, tk, N_IN), lambda l, i: (l, i, 0)),
                   pl.BlockSpec((None, tk, N_HEADS * d), lambda l, i: (l, i, 0)),
                   pl.BlockSpec((None, 16, tk), lambda l, i: (l, 0, i))],
        out_shape=[jax.ShapeDtypeStruct((depth, d, N_IN), BF16), jax.ShapeDtypeStruct((depth, d, N_HEADS * d), BF16),
                   jax.ShapeDtypeStruct((depth, 16, d), BF16)],
        compiler_params=_params(("parallel", "parallel")),
        name="prep_w_in",
    )(jnp.swapaxes(w_in, 1, 2))


CAST_TILE_BYTES = 2 * 1024 * 1024


def _cast_kernel(x_ref, o_ref):
    o_ref[...] = x_ref[...].astype(BF16)


def _to_bf16(a):
    cols = a.shape[-1]
    rows = a.size // cols
    tm = CAST_TILE_BYTES // (4 * cols)
    spec = pl.BlockSpec((tm, cols), lambda i: (i, 0))
    out = pl.pallas_call(
        _cast_kernel,
        grid=(rows // tm,),
        in_specs=[spec],
        out_specs=spec,
        out_shape=jax.ShapeDtypeStruct((rows, cols), BF16),
        compiler_params=_params(("parallel",)),
        name="cast_bf16",
    )(a.reshape(rows, cols))
    return out.reshape(a.shape)


def _gate_bias_layout(gate_b):
    ib, fb = gate_b[0], gate_b[1]
    z = jnp.zeros((LANES - 4,), F32)
    row = jnp.concatenate([ib[0:2], fb[0:2], z, ib[2:4], fb[2:4], z]).reshape(1, 2 * LANES)
    z4 = jnp.zeros((4,), F32)
    col = jnp.concatenate([ib[0:2], fb[0:2], z4, ib[2:4], fb[2:4], z4])
    return row, jnp.broadcast_to(col[:, None], (16, LANES))


def kernel(x, c, positions, w_ada, b_ada, norm_g, w_in, mlstm_conv_w, mlstm_conv_b, mlstm_gate_b,
           ret_norm_g, swa_q_norm_g, swa_k_norm_g, swa_sinks, rel_bias, w_up, w_out, w_ff1, w_ff2):
    batch, seq, d = x.shape
    depth = w_in.shape[0]
    t = batch * seq
    nb = seq // CHUNK

    c8 = jnp.concatenate([c, jnp.zeros((8 - batch, d), F32)], axis=0)
    mod = _ada(c8, w_ada.reshape(depth * 2, d, 3 * d), b_ada.reshape(depth * 2, 1, 3 * d))
    mod = mod[:, :batch].reshape(depth, 2, batch, 3, 1, d)

    pos_col = jnp.broadcast_to(positions.reshape(t, 1), (t, LANES))
    pos_row = positions.reshape(batch * nb, 1, CHUNK)
    half = HEAD_DIM // 2
    inv = ROPE_BASE ** (-(np.arange(LANES) % half).astype(np.float64) / half)
    cos_t, sin_t = _rope_tables(pos_col, jnp.asarray(inv, F32).reshape(1, LANES))
    bias_tab = jnp.concatenate([rel_bias.T, jnp.zeros((N_HEADS, LANES - N_BUCKETS), F32)], axis=1)
    bias = _bias_tiles(bias_tab, pos_col, pos_row, batch, nb)

    log_gamma = np.log(1.0 - np.exp2(-(RET_DECAY_BASE + np.arange(N_HEADS, dtype=np.float64))))
    lg_tab = jnp.asarray(np.broadcast_to(np.repeat(log_gamma, HEAD_DIM).reshape(2, 1, LANES), (2, 8, LANES)), F32)

    w_perm, w_gate, w_ift = _prep_w_in(w_in)
    w_up_bf, w_out_bf, w_ff1_bf, w_ff2_bf = (_to_bf16(w) for w in (w_up, w_out, w_ff1, w_ff2))

    xt = x.reshape(t, d)
    for l in range(depth):
        gb_row, gbt = _gate_bias_layout(mlstm_gate_b[l])
        g1 = norm_g[l, 0].reshape(1, d)
        g2 = norm_g[l, 1].reshape(1, d)
        shift1, scale1, gate1 = mod[l, 0, :, 0], mod[l, 0, :, 1], mod[l, 0, :, 2]
        shift2, scale2, gate2 = mod[l, 1, :, 0], mod[l, 1, :, 1], mod[l, 1, :, 2]

        ml, ifc, ret, sb, swa, ift = _in_proj(xt, g1, scale1, shift1, w_perm, w_ift, gb_row, gbt, l, batch, seq)
        out_ml = _mlstm(ml, ifc, ift, mlstm_conv_w[l], mlstm_conv_b[l].reshape(1, 2 * W_BRANCH), batch, seq)
        out_ret = _retention(ret, cos_t, sin_t, lg_tab, ret_norm_g[l].reshape(1, W_BRANCH), batch, seq)
        out_sb = _stick_breaking(sb, batch, seq)
        qg = jnp.tile(swa_q_norm_g[l], 2).reshape(1, LANES)
        kg = jnp.tile(swa_k_norm_g[l], 2).reshape(1, LANES)
        out_swa = _swa(swa, bias, swa_sinks[l], qg, kg, batch, seq)

        branches = tuple(o.reshape(t, W_BRANCH) for o in (out_ml, out_ret, out_sb, out_swa))
        xt = _merge(xt, g1, scale1, shift1, gate1, branches, w_gate, w_up_bf, w_out_bf, l, seq)
        xt = _ffn(xt, g2, scale2, shift2, gate2, w_ff1_bf, w_ff2_bf, l, seq)
    return xt.reshape(batch, seq, d)
```

```python
import functools
import math

import numpy as np
import jax
import jax.numpy as jnp
from jax import lax
from jax.experimental import pallas as pl
from jax.experimental.pallas import tpu as pltpu

F32 = jnp.float32
BF16 = jnp.bfloat16

D_MODEL = 1024
HEAD_DIM = 64
N_HEADS = 4
W_BRANCH = N_HEADS * HEAD_DIM
LANES = 128
CHUNK = 128
CONV_K = 4
D_FF = 4 * D_MODEL
N_BUCKETS = 32
MAX_DIST = 128
ROPE_BASE = 10000.0
RET_DECAY_BASE = 5.0
EPS = 1e-6
NEG = -1e30
F32_EXP_UNDERFLOW = -104.0
VMEM_LIMIT = 56 * 1024 * 1024
LAYER_VMEM_LIMIT = 60 * 1024 * 1024

SEG_ML = (0, 1024)
SEG_IFC = (1024, 1280)
SEG_RET = (1280, 2304)
SEG_SB = (2304, 3072)
SEG_SWA = (3072, 3840)
N_IN = 3840


def _dot(a, b):
    return jnp.dot(a, b, preferred_element_type=F32)


def _dot_nt(a, b):
    return lax.dot_general(a, b, (((1,), (1,)), ((), ())), preferred_element_type=F32)


def _dot_tn(a, b):
    return lax.dot_general(a, b, (((0,), (0,)), ((), ())), preferred_element_type=F32)


def _split2(x):
    hi = x.astype(BF16)
    lo = (x - hi.astype(F32)).astype(BF16)
    return hi, lo


def _split3(x):
    hi = x.astype(BF16)
    r = x - hi.astype(F32)
    mid = r.astype(BF16)
    lo = (r - mid.astype(F32)).astype(BF16)
    return hi, mid, lo


def _iota(shape, axis):
    return lax.broadcasted_iota(jnp.int32, shape, axis)


def _log_sigmoid(x):
    return jnp.minimum(x, 0.0) - jnp.log(1.0 + jnp.exp(-jnp.abs(x)))


def _sigmoid(x):
    return 1.0 / (1.0 + jnp.exp(-x))


def _silu(x):
    return x * _sigmoid(x)


def _norm_mod(x, g, scale, shift):
    ms = jnp.mean(x * x, axis=-1, keepdims=True)
    y = x * lax.rsqrt(ms + EPS)
    return (y * g) * (1.0 + scale) + shift


def _head_mean_sq(x):
    lane = _iota(x.shape, 1)
    sq = x * x
    s0 = jnp.sum(jnp.where(lane < HEAD_DIM, sq, 0.0), axis=-1, keepdims=True)
    s1 = jnp.sum(jnp.where(lane >= HEAD_DIM, sq, 0.0), axis=-1, keepdims=True)
    return jnp.where(lane < HEAD_DIM, s0, s1) * (1.0 / HEAD_DIM)


def _params(sem):
    return pltpu.CompilerParams(dimension_semantics=sem, vmem_limit_bytes=VMEM_LIMIT)


def _interleave(chains, filler=None):
    def advance(gen):
        try:
            next(gen)
            return True
        except StopIteration:
            return False

    live = list(chains)
    filling = filler is not None
    while live:
        still = []
        for ch in live:
            if advance(ch):
                still.append(ch)
            if filling:
                filling = advance(filler)
        live = still
    while filling:
        filling = advance(filler)


def _resident(a):
    return pl.BlockSpec(a.shape, lambda *_: (0,) * a.ndim, pipeline_mode=pl.Buffered(1))


def _layer_of(a, layer):
    return pl.BlockSpec((None,) + a.shape[1:], lambda *_: (layer,) + (0,) * (a.ndim - 1),
                        pipeline_mode=pl.Buffered(1))


def _ada_kernel(c_ref, w_ref, b_ref, o_ref):
    c = c_ref[...]
    ch, cl = _split2(_silu(c))
    wh, wl = _split2(w_ref[0])
    o_ref[0] = _dot(ch, wh) + _dot(ch, wl) + _dot(cl, wh) + b_ref[0]


def _ada(c8, w_ada, b_ada):
    n_mod, d, n3 = w_ada.shape
    tn = 1024
    return pl.pallas_call(
        _ada_kernel,
        grid=(n_mod, n3 // tn),
        in_specs=[pl.BlockSpec((8, d), lambda m, n: (0, 0)),
                  pl.BlockSpec((1, d, tn), lambda m, n: (m, 0, n)),
                  pl.BlockSpec((1, 1, tn), lambda m, n: (m, 0, n))],
        out_specs=pl.BlockSpec((1, 8, tn), lambda m, n: (m, 0, n)),
        out_shape=jax.ShapeDtypeStruct((n_mod, 8, n3), F32),
        compiler_params=_params(("parallel", "parallel")),
        name="ada_mod",
    )(c8, w_ada, b_ada)


def _rope_kernel(pos_ref, inv_ref, cos_ref, sin_ref):
    ang = pos_ref[...].astype(F32) * inv_ref[...]
    cos_ref[...] = jnp.cos(ang)
    sin_ref[...] = jnp.sin(ang)


def _rope_tables(pos_b, inv_row):
    t = pos_b.shape[0]
    tm = 1024
    spec = pl.BlockSpec((tm, LANES), lambda i: (i, 0))
    return pl.pallas_call(
        _rope_kernel,
        grid=(t // tm,),
        in_specs=[spec, pl.BlockSpec((1, LANES), lambda i: (0, 0))],
        out_specs=[spec, spec],
        out_shape=[jax.ShapeDtypeStruct((t, LANES), F32)] * 2,
        compiler_params=_params(("parallel",)),
        name="rope_tables",
    )(pos_b, inv_row)


BIAS_BLOCKS = 4


def _bias_kernel(tab_ref, posq_ref, posc_ref, posp_ref, o_ref):
    max_exact = N_BUCKETS // 2
    rows = [jnp.broadcast_to(tab_ref[h:h + 1, :], (CHUNK, LANES)) for h in range(N_HEADS)]
    for r in range(BIAS_BLOCKS):
        pos_q = posq_ref[r * CHUNK:(r + 1) * CHUNK, :]
        pos_prev = posp_ref[0] if r == 0 else posc_ref[r - 1]
        rel = jnp.concatenate([pos_q - pos_prev, pos_q - posc_ref[r]], axis=1)
        n = jnp.maximum(rel, 0)
        nf = jnp.maximum(n, 1).astype(F32)
        large = max_exact + (jnp.log(nf / max_exact) / math.log(MAX_DIST / max_exact)
                             * (N_BUCKETS - max_exact)).astype(jnp.int32)
        large = jnp.minimum(large, N_BUCKETS - 1)
        bucket = jnp.where(n < max_exact, n, large)
        halves = (bucket[:, :CHUNK], bucket[:, CHUNK:])
        for h in range(N_HEADS):
            o_ref[r, h] = jnp.concatenate([jnp.take_along_axis(rows[h], idx, axis=1) for idx in halves], axis=1)


def _bias_tiles(rel_bias, pos_b, pos_r, batch, n_b):
    nblk = batch * n_b
    return pl.pallas_call(
        _bias_kernel,
        grid=(nblk // BIAS_BLOCKS,),
        in_specs=[pl.BlockSpec((N_HEADS, LANES), lambda i: (0, 0)),
                  pl.BlockSpec((BIAS_BLOCKS * CHUNK, LANES), lambda i: (i, 0)),
                  pl.BlockSpec((BIAS_BLOCKS, 1, CHUNK), lambda i: (i, 0, 0)),
                  pl.BlockSpec((1, 1, CHUNK), lambda i: (jnp.maximum(i * BIAS_BLOCKS - 1, 0), 0, 0))],
        out_specs=pl.BlockSpec((BIAS_BLOCKS, N_HEADS, CHUNK, 2 * CHUNK), lambda i: (i, 0, 0, 0)),
        out_shape=jax.ShapeDtypeStruct((nblk, N_HEADS, CHUNK, 2 * CHUNK), F32),
        compiler_params=_params(("parallel",)),
        name="t5_bias",
    )(rel_bias, pos_b, pos_r, pos_r)


def _in_kernel(x_ref, g_ref, sc_ref, sh_ref, w_ref, wt_ref, gb_ref, gbt_ref,
               ml_ref, ifc_ref, ret_ref, sb_ref, swa_ref, ift_ref):
    h = _norm_mod(x_ref[...], g_ref[...], sc_ref[0], sh_ref[0]).astype(BF16)

    def seg(s):
        return _dot(h, w_ref[:, s[0]:s[1]])

    ml_ref[...] = seg(SEG_ML)
    ifc_ref[...] = seg(SEG_IFC) + gb_ref[...]
    ret_ref[...] = seg(SEG_RET)
    sb_ref[...] = seg(SEG_SB).astype(BF16)
    swa_ref[...] = seg(SEG_SWA)
    ift_ref[0] = _dot_nt(wt_ref[...], h) + gbt_ref[:, 0:1]


def _in_proj(x, g, scale, shift, w_perm, w_ift, gb_row, gbt, layer, batch, seq):
    t, d = x.shape
    tm = 512
    tpb = seq // tm
    row = lambda n: pl.BlockSpec((tm, n), lambda i: (i, 0))
    mod = pl.BlockSpec((1, 1, d), lambda i: (i // tpb, 0, 0))
    widths = [s[1] - s[0] for s in (SEG_ML, SEG_IFC, SEG_RET, SEG_SB, SEG_SWA)]
    dts = [F32, F32, F32, BF16, F32]
    return pl.pallas_call(
        _in_kernel,
        grid=(t // tm,),
        in_specs=[row(d), _resident(g), mod, mod, _layer_of(w_perm, layer), _layer_of(w_ift, layer), _resident(gb_row),
                  _resident(gbt)],
        out_specs=[row(n) for n in widths] + [pl.BlockSpec((1, 16, tm), lambda i: (i // tpb, 0, i % tpb))],
        out_shape=[jax.ShapeDtypeStruct((t, n), dt) for n, dt in zip(widths, dts)]
        + [jax.ShapeDtypeStruct((batch, 16, seq), F32)],
        compiler_params=_params(("parallel",)),
        name="in_proj",
    )(x, g, scale, shift, w_perm, w_ift, gb_row, gbt)


def _mlstm_chains(ml_ref, ifc_ref, ift_ref, cw_ref, cb_ref, put, halo, c_st, m_st):
    L = CHUNK
    HALO = 8
    batch = ml_ref.shape[0]
    lane = _iota((L, LANES), 1)
    row = _iota((L, LANES), 0)
    causal = row >= lane
    tri = jnp.where(causal, 1.0, 0.0).astype(BF16)
    upper = jnp.where(row <= lane, 1.0, 0.0).astype(BF16)
    eye = jnp.where(row == lane, 1.0, 0.0).astype(BF16)
    spread = jnp.where(_iota((LANES, 4 * LANES), 0) == _iota((LANES, 4 * LANES), 1) // LANES,
                       1.0, 0.0).astype(BF16)

    def conv_silu(b, col, slot):
        hl = halo.at[slot]
        hl[HALO:HALO + L, :] = ml_ref[b, :, col:col + LANES]
        acc = jnp.zeros((L, LANES), F32) + cb_ref[:, col:col + LANES]
        for j in range(CONV_K):
            off = HALO - (CONV_K - 1) + j
            acc = acc + hl[off:off + L, :] * cw_ref[j:j + 1, col:col + LANES]
        hl[0:HALO, :] = hl[L:L + HALO, :]
        return _silu(acc)

    def pair_chain(b, p):
        pc = p * LANES
        ift = ift_ref[b, p * 8:p * 8 + 8, :]
        lr_h, lr_m, lr_l = _split3(_log_sigmoid(ift))
        a_rows = _dot(lr_h, upper) + _dot(lr_m, upper) + _dot(lr_l, upper)
        g_h, g_m, g_l = _split3(ifc_ref[b, :, pc:pc + LANES])
        gates_b = _dot(g_h, spread) + _dot(g_m, spread) + _dot(g_l, spread)
        q = conv_silu(b, pc, (b * 2 + p) * 2)
        k = conv_silu(b, W_BRANCH + pc, (b * 2 + p) * 2 + 1)
        v = ml_ref[b, :, 2 * W_BRANCH + pc:2 * W_BRANCH + pc + LANES]
        k_bf = k.astype(BF16)
        yield

        lf_h, lf_m, lf_l = _split3(_log_sigmoid(gates_b[:, 2 * LANES:]))
        a_b = _dot(tri, lf_h) + _dot(tri, lf_m) + _dot(tri, lf_l)
        heads = []
        for hh in range(2):
            ch = (b * 2 + p) * 2 + hh
            hm = (lane >= HEAD_DIM) if hh else (lane < HEAD_DIM)
            qh = (jnp.where(hm, q, 0.0) * (HEAD_DIM ** -0.5)).astype(BF16)
            c_in = c_st[ch]
            heads.append(dict(ch=ch, hm=hm, c_in=c_in, s_qk=_dot_nt(qh, k_bf), q_c=_dot(qh, c_in.astype(BF16))))
        yield

        for hh, hd in enumerate(heads):
            i_col = gates_b[:, hh * LANES:(hh + 1) * LANES]
            a_col = a_b[:, hh * LANES:(hh + 1) * LANES]
            a_last = a_col[L - 1:L, :]
            m_in = m_st[hd["ch"], 0:1, :]
            v_aug = jnp.where(hd["hm"], v, 1.0).astype(BF16)

            d_intra = jnp.where(causal, a_col - (a_rows[2 + hh:3 + hh, :] - ift[hh:hh + 1, :]), NEG)
            m_intra = jnp.max(d_intra, axis=-1, keepdims=True)
            m_inter = a_col + m_in
            m_tot = jnp.maximum(m_inter, m_intra)
            pm = jnp.exp(d_intra - m_tot) * hd["s_qk"]
            hd["p_v"] = _dot(pm.astype(BF16), v_aug)
            hd["s_inter"] = jnp.exp(m_inter - m_tot)
            hd["floor"] = jnp.exp(-m_tot)

            g_col = a_last - a_col + i_col
            m_loc = jnp.max(g_col, axis=0, keepdims=True)
            kw = (jnp.where(hd["hm"], k, 0.0) * jnp.exp(g_col - m_loc)).astype(BF16)
            hd["kw_t"] = _dot_nt(eye, kw)
            hd["v_aug"] = v_aug
            m_new = jnp.maximum(a_last + m_in, m_loc)
            hd["w_old"] = jnp.exp(a_last + m_in - m_new)
            hd["w_new"] = jnp.exp(m_loc - m_new)
            m_st[hd["ch"]] = jnp.zeros((8, LANES), F32) + m_new
        yield

        for hd in heads:
            hd["c_loc"] = _dot(hd["kw_t"].astype(BF16), hd["v_aug"])
            hd["o_aug"] = hd["s_inter"] * hd["q_c"] + hd["p_v"]
        den = pltpu.roll(jnp.where(lane < HEAD_DIM, heads[1]["o_aug"], heads[0]["o_aug"]), HEAD_DIM, 1)
        num = jnp.where(lane < HEAD_DIM, heads[0]["o_aug"], heads[1]["o_aug"])
        floor = jnp.where(lane < HEAD_DIM, heads[0]["floor"], heads[1]["floor"])
        h_ml = num / jnp.maximum(jnp.abs(den), floor)
        o_gate = ml_ref[b, :, 3 * W_BRANCH + pc:3 * W_BRANCH + pc + LANES]
        put(b, pc, h_ml * _sigmoid(o_gate))
        yield

        for hd in heads:
            c_st[hd["ch"]] = hd["w_old"] * hd["c_in"] + hd["w_new"] * hd["c_loc"]

    return [pair_chain(b, p) for b in range(batch) for p in range(2)]


def _retention_chains(ret_ref, cos_ref, sin_ref, lg_ref, ng_ref, put, st):
    L = CHUNK
    batch = ret_ref.shape[0]
    lane = _iota((L, LANES), 1)
    row = _iota((L, LANES), 0)
    rowf = row.astype(F32)
    relf = jnp.maximum(row - lane, 0).astype(F32)
    first = (lane % HEAD_DIM) < (HEAD_DIM // 2)
    same_head = (row // HEAD_DIM) == (lane // HEAD_DIM)

    def rot(t, cos, sin):
        partner = jnp.where(first, -pltpu.roll(t, LANES - HEAD_DIM // 2, 1), pltpu.roll(t, HEAD_DIM // 2, 1))
        return t * cos + partner * sin

    decays = []
    for p in range(2):
        lg = lg_ref[p, 0:1, :]
        decays.append(dict(
            q=jnp.exp((rowf + 1.0) * lg), k=jnp.exp((L - 1.0 - rowf) * lg), c=jnp.exp(float(L) * lg),
            intra=[jnp.where(row >= lane, jnp.exp(relf * lg[:, hh * HEAD_DIM:hh * HEAD_DIM + 1]), 0.0)
                   for hh in range(2)]))

    def pair_chain(b, p):
        pc = p * LANES
        dec = decays[p]
        cos = cos_ref[b]
        sin = sin_ref[b]
        q = rot(ret_ref[b, :, pc:pc + LANES], cos, sin)
        k = rot(ret_ref[b, :, W_BRANCH + pc:W_BRANCH + pc + LANES], cos, sin) * (HEAD_DIM ** -0.5)
        v_bf = ret_ref[b, :, 2 * W_BRANCH + pc:2 * W_BRANCH + pc + LANES].astype(BF16)
        k_bf = k.astype(BF16)
        state = st[b * 2 + p]
        s_qk = [_dot_nt(jnp.where((lane >= HEAD_DIM) if hh else (lane < HEAD_DIM), q, 0.0).astype(BF16), k_bf)
                for hh in range(2)]
        inter = _dot(q.astype(BF16), state.astype(BF16))
        kv = _dot_tn((k * dec["k"]).astype(BF16), v_bf)
        yield

        intra = [_dot((s_qk[hh] * dec["intra"][hh]).astype(BF16), v_bf) for hh in range(2)]
        st[b * 2 + p] = dec["c"] * state + jnp.where(same_head, kv, 0.0)
        yield

        o = jnp.where(lane < HEAD_DIM, intra[0], intra[1]) + inter * dec["q"]
        y = o * lax.rsqrt(_head_mean_sq(o) + EPS) * ng_ref[:, pc:pc + LANES]
        gate = ret_ref[b, :, 3 * W_BRANCH + pc:3 * W_BRANCH + pc + LANES]
        put(b, pc, y * _silu(gate))

    return [pair_chain(b, p) for b in range(batch) for p in range(2)]


SB_FUSED_BLOCKS = 3


def _sb_suffix_op():
    L = CHUNK
    r2 = _iota((2 * L, 2 * L), 0) % L
    c2 = _iota((2 * L, 2 * L), 1)
    return jnp.where((c2 >= L) | (r2 >= c2), 1.0, 0.0).astype(BF16)


def _sb_tile_chain(c, b, p, tiles, fresh, qh_s, carry_s, acc_s, suffix_op):
    L = CHUNK
    pc = p * LANES
    zs = [_dot_nt(qh_s[c], k_ref[b, :, pc:pc + LANES]) for k_ref, _, _ in tiles]
    yield
    rrs = []
    for z, (_, _, mask) in zip(zs, tiles):
        lk = -(jnp.maximum(z, 0.0) + jnp.log(1.0 + jnp.exp(-jnp.abs(z))))
        if mask is not None:
            lk = jnp.where(mask, lk, 0.0)
        hi, lo = _split2(lk)
        rrs.append(_dot(jnp.concatenate([hi, lo], axis=1), suffix_op))
    yield
    carry = None if fresh else carry_s[c]
    parts = []
    for z, rr, (_, v_ref, mask) in zip(zs, rrs, tiles):
        e = z + rr[:, :L]
        w = jnp.exp(e if carry is None else e + carry)
        if mask is not None:
            w = jnp.where(mask, w, 0.0)
        parts.append(_dot(w.astype(BF16), v_ref[b, :, pc:pc + LANES]))
        carry = rr[:, L:] if carry is None else carry + rr[:, L:]
    carry_s[c] = carry
    yield
    acc = None if fresh else acc_s[c]
    for part in parts:
        acc = part if acc is None else acc + part
    acc_s[c] = acc


def _sb_chains(i, q_ref, k_refs, v_refs, qh_s, carry_s, acc_s):
    L = CHUNK
    batch = q_ref.shape[0]
    lane = _iota((L, LANES), 1)
    row = _iota((L, LANES), 0)
    heads = [(b, p, hh) for b in range(batch) for p in range(2) for hh in range(2)]
    for c, (b, p, hh) in enumerate(heads):
        hm = (lane >= HEAD_DIM) if hh else (lane < HEAD_DIM)
        qh_s[c] = jnp.where(hm, q_ref[b, :, p * LANES:(p + 1) * LANES] * (HEAD_DIM ** -0.5), 0.0).astype(BF16)
    block_no = jnp.zeros((L, LANES), jnp.int32) + i
    masks = [lane < row] + [block_no >= d for d in range(1, SB_FUSED_BLOCKS)]
    tiles = [(k_refs[d], v_refs[d], masks[d]) for d in range(SB_FUSED_BLOCKS)]
    suffix_op = _sb_suffix_op()
    return [_sb_tile_chain(c, b, p, tiles, True, qh_s, carry_s, acc_s, suffix_op) for c, (b, p, hh) in enumerate(heads)]


def _sb_older_blocks(i, sb_hbm, kbuf, vbuf, sem, qh_s, carry_s, acc_s):
    L = CHUNK
    batch = kbuf.shape[0]
    heads = [(b, p, hh) for b in range(batch) for p in range(2) for hh in range(2)]

    def all_underflow():
        return (jnp.max(jnp.max(carry_s[...], axis=0)) <= F32_EXP_UNDERFLOW).astype(jnp.int32)

    def fetch(j, col, buf, slot):
        return pltpu.make_async_copy(
            sb_hbm.at[:, pl.ds(pl.multiple_of(j * L, L), L), pl.ds(col * W_BRANCH, W_BRANCH)], buf, sem.at[slot])

    def cond(st):
        return jnp.logical_and(st[0] >= 0, st[1] == 0)

    def body(st):
        copies = [fetch(st[0], 1, kbuf, 0), fetch(st[0], 2, vbuf, 1)]
        for cp in copies:
            cp.start()
        for cp in copies:
            cp.wait()
        suffix_op = _sb_suffix_op()
        _interleave([_sb_tile_chain(c, b, p, [(kbuf, vbuf, None)], False, qh_s, carry_s, acc_s, suffix_op)
                     for c, (b, p, hh) in enumerate(heads)])
        return (st[0] - 1, all_underflow())

    lax.while_loop(cond, body, (i - SB_FUSED_BLOCKS, all_underflow()))


def _swa_chains(i, sink_ref, q_ref, kc_ref, kp_ref, vc_ref, vp_ref, bias_ref, qg_ref, kg_ref, put):
    L = CHUNK
    batch = q_ref.shape[0]
    lane = _iota((L, LANES), 1)
    t = _iota((L, 2 * L), 0)
    j = _iota((L, 2 * L), 1)
    dist = t + L - j
    block_no = jnp.zeros((L, 2 * L), jnp.int32) + i
    valid = (dist >= 0) & (dist < L) & ((j >= L) | (block_no > 0))

    def qk_norm(x, g_ref):
        return x * lax.rsqrt(_head_mean_sq(x) + EPS) * g_ref[...]

    def group_chain(b, g):
        sl = slice(g * LANES, (g + 1) * LANES)
        qn = qk_norm(q_ref[b, :, sl], qg_ref) * (HEAD_DIM ** -0.5)
        kcat = jnp.concatenate([qk_norm(kp_ref[b, :, sl], kg_ref), qk_norm(kc_ref[b, :, sl], kg_ref)],
                               axis=0).astype(BF16)
        qk = [_dot_nt(jnp.where((lane >= HEAD_DIM) if r else (lane < HEAD_DIM), qn, 0.0).astype(BF16), kcat)
              for r in range(2)]
        yield
        vcat = jnp.concatenate([vp_ref[b, :, sl], vc_ref[b, :, sl]], axis=0).astype(BF16)
        outs = []
        for r in range(2):
            hq = 2 * g + r
            logits = jnp.where(valid, qk[r] + bias_ref[b, 0, hq], NEG)
            sink = sink_ref[hq]
            m = jnp.maximum(jnp.max(logits, axis=-1, keepdims=True), sink)
            pr = jnp.exp(logits - m)
            den = jnp.sum(pr, axis=-1, keepdims=True) + jnp.exp(sink - m)
            outs.append(_dot((pr / den).astype(BF16), vcat))
        yield
        put(b, g * LANES, jnp.where(lane < HEAD_DIM, outs[0], outs[1]))

    return [group_chain(b, g) for b in range(batch) for g in range(2)]


DENSE_COLS = 256


def _dense_chain(x_ref, g1_ref, mod1, g2_ref, mod2, branch, wg_ref, wu_ref, wo_ref, w1_ref, w2_ref, out_ref,
                 h_s, merged_s, x1_s, ff_s):
    L = CHUNK
    batch = x_ref.shape[0]
    rows = lambda b: slice(b * L, (b + 1) * L)
    sh1, sc1, gt1 = mod1
    sh2, sc2, gt2 = mod2
    for b in range(batch):
        h_s[rows(b), :] = _norm_mod(x_ref[b], g1_ref[...], sc1[b], sh1[b]).astype(BF16)
    yield
    for j in range(D_MODEL // DENSE_COLS):
        cs = slice(j * DENSE_COLS, (j + 1) * DENSE_COLS)
        acc = None
        for n in range(N_HEADS):
            gate = _sigmoid(_dot(h_s[...], wg_ref[:, n * D_MODEL + j * DENSE_COLS:n * D_MODEL + (j + 1) * DENSE_COLS]))
            term = gate * _dot(branch(n), wu_ref[n, :, cs])
            acc = term if acc is None else acc + term
            yield
        merged_s[:, cs] = acc.astype(BF16)
    for j in range(D_MODEL // DENSE_COLS):
        cs = slice(j * DENSE_COLS, (j + 1) * DENSE_COLS)
        mix = _dot(merged_s[...], wo_ref[:, cs])
        for b in range(batch):
            x1_s[rows(b), cs] = x_ref[b, :, cs] + gt1[b][:, cs] * mix[rows(b)]
        yield
    for b in range(batch):
        h_s[rows(b), :] = _norm_mod(x1_s[rows(b), :], g2_ref[...], sc2[b], sh2[b]).astype(BF16)
    yield
    for n in range(D_FF // DENSE_COLS):
        fs = slice(n * DENSE_COLS, (n + 1) * DENSE_COLS)
        a = jnp.maximum(_dot(h_s[...], w1_ref[:, fs]), 0.0)
        part = _dot((a * a).astype(BF16), w2_ref[fs, :])
        ff_s[...] = part if n == 0 else ff_s[...] + part
        yield
    for b in range(batch):
        out_ref[b] = x1_s[rows(b), :] + gt2[b] * ff_s[rows(b), :]


def _layer_kernel(sink_ref, ml_ref, ifc_ref, ift_ref, cw_ref, cb_ref,
                  ret_ref, cos_ref, sin_ref, lg_ref, ng_ref,
                  sbq_ref, sbk0_ref, sbk1_ref, sbk2_ref, sbv0_ref, sbv1_ref, sbv2_ref, sb_hbm,
                  swq_ref, swkc_ref, swkp_ref, swvc_ref, swvp_ref, bias_ref, qg_ref, kg_ref,
                  x_ref, g1_ref, sh1_ref, sc1_ref, gt1_ref, g2_ref, sh2_ref, sc2_ref, gt2_ref,
                  wg_ref, wu_ref, wo_ref, w1_ref, w2_ref,
                  out_ref,
                  halo, c_st, m_st, ret_st, qh_s, carry_s, acc_s, kbuf, vbuf, sem,
                  branch_s, h_s, merged_s, x1_s, ff_s, *, n_chunks):
    L = CHUNK
    batch = ml_ref.shape[0]
    step = pl.program_id(0)
    i = jnp.minimum(step, n_chunks - 1)
    slot = step % 2

    @pl.when(step == 0)
    def _init():
        halo[:, 0:8, :] = jnp.zeros((halo.shape[0], 8, LANES), F32)
        c_st[...] = jnp.zeros(c_st.shape, F32)
        m_st[...] = jnp.zeros(m_st.shape, F32)
        ret_st[...] = jnp.zeros(ret_st.shape, F32)
        branch_s[1] = jnp.zeros(branch_s.shape[1:], BF16)

    def put(n):
        def write(b, col, value):
            branch_s[slot, n, b * L:(b + 1) * L, col:col + LANES] = value.astype(BF16)
        return write

    chains = (_mlstm_chains(ml_ref, ifc_ref, ift_ref, cw_ref, cb_ref, put(0), halo, c_st, m_st)
              + _retention_chains(ret_ref, cos_ref, sin_ref, lg_ref, ng_ref, put(1), ret_st)
              + _sb_chains(i, sbq_ref, (sbk0_ref, sbk1_ref, sbk2_ref), (sbv0_ref, sbv1_ref, sbv2_ref),
                           qh_s, carry_s, acc_s)
              + _swa_chains(i, sink_ref, swq_ref, swkc_ref, swkp_ref, swvc_ref, swvp_ref, bias_ref, qg_ref, kg_ref,
                            put(3)))
    dense = _dense_chain(x_ref, g1_ref, (sh1_ref, sc1_ref, gt1_ref), g2_ref, (sh2_ref, sc2_ref, gt2_ref),
                         lambda n: branch_s[1 - slot, n], wg_ref, wu_ref, wo_ref, w1_ref, w2_ref, out_ref,
                         h_s, merged_s, x1_s, ff_s)
    _interleave(chains, filler=dense)

    _sb_older_blocks(i, sb_hbm, kbuf, vbuf, sem, qh_s, carry_s, acc_s)
    lane = _iota((L, LANES), 1)
    for b in range(batch):
        for p in range(2):
            c0 = (b * 2 + p) * 2
            put(2)(b, p * LANES, jnp.where(lane < HEAD_DIM, acc_s[c0], acc_s[c0 + 1]))


def _layer(xt, mods, g1, g2, ml, ifc, ift, ret, sb, swa, cos_t, sin_t, bias, small, weights, layer, batch, seq):
    d = xt.shape[1]
    nc = seq // CHUNK
    n_chain = batch * N_HEADS
    conv_w, conv_b, lg_tab, ret_g, sinks, q_g, k_g = small
    w_gate, w_up, w_out, w_ff1, w_ff2 = weights
    mix = lambda c: jnp.minimum(c, nc - 1)
    lag = lambda c: jnp.maximum(c - 1, 0)
    chunk = lambda n, blk=0, back=0: pl.BlockSpec(
        (batch, CHUNK, n), lambda c: (0, jnp.maximum(mix(c) - back, 0), blk))
    sb3 = sb.reshape(batch, seq, -1)
    swa3 = swa.reshape(batch, seq, -1)
    x3 = xt.reshape(batch, seq, d)
    rowblk = pl.BlockSpec((batch, CHUNK, d), lambda c: (0, lag(c), 0))
    in_specs = (
        [pl.BlockSpec(memory_space=pltpu.SMEM),
         chunk(4 * W_BRANCH), chunk(2 * LANES), pl.BlockSpec((batch, 16, CHUNK), lambda c: (0, 0, mix(c))),
         _resident(conv_w), _resident(conv_b),
         chunk(4 * W_BRANCH), chunk(LANES), chunk(LANES), _resident(lg_tab), _resident(ret_g),
         chunk(W_BRANCH, 0)]
        + [chunk(W_BRANCH, 1, back) for back in range(SB_FUSED_BLOCKS)]
        + [chunk(W_BRANCH, 2, back) for back in range(SB_FUSED_BLOCKS)]
        + [pl.BlockSpec(memory_space=pl.ANY),
           chunk(W_BRANCH, 0), chunk(W_BRANCH, 1), chunk(W_BRANCH, 1, 1), chunk(W_BRANCH, 2), chunk(W_BRANCH, 2, 1),
           pl.BlockSpec((batch, 1, N_HEADS, CHUNK, 2 * CHUNK), lambda c: (0, mix(c), 0, 0, 0)),
           _resident(q_g), _resident(k_g),
           rowblk, _resident(g1)] + [_resident(m) for m in mods[0]] + [_resident(g2)] + [_resident(m) for m in mods[1]]
        + [_layer_of(w, layer) for w in (w_gate, w_up, w_out, w_ff1, w_ff2)])
    rows = batch * CHUNK
    out = pl.pallas_call(
        functools.partial(_layer_kernel, n_chunks=nc),
        grid=(nc + 1,),
        in_specs=in_specs,
        out_specs=rowblk,
        out_shape=jax.ShapeDtypeStruct((batch, seq, d), F32),
        scratch_shapes=[pltpu.VMEM((n_chain, CHUNK + 8, LANES), F32), pltpu.VMEM((n_chain, LANES, LANES), F32),
                        pltpu.VMEM((n_chain, 8, LANES), F32), pltpu.VMEM((batch * 2, LANES, LANES), F32),
                        pltpu.VMEM((n_chain, CHUNK, LANES), BF16), pltpu.VMEM((n_chain, CHUNK, LANES), F32),
                        pltpu.VMEM((n_chain, CHUNK, LANES), F32),
                        pltpu.VMEM((batch, CHUNK, W_BRANCH), BF16), pltpu.VMEM((batch, CHUNK, W_BRANCH), BF16),
                        pltpu.SemaphoreType.DMA((2,)),
                        pltpu.VMEM((2, N_HEADS, rows, W_BRANCH), BF16), pltpu.VMEM((rows, d), BF16),
                        pltpu.VMEM((rows, d), BF16), pltpu.VMEM((rows, d), F32), pltpu.VMEM((rows, d), F32)],
        compiler_params=pltpu.CompilerParams(dimension_semantics=("arbitrary",), vmem_limit_bytes=LAYER_VMEM_LIMIT),
        name="layer",
    )(sinks, ml.reshape(batch, seq, -1), ifc.reshape(batch, seq, -1), ift, conv_w, conv_b,
      ret.reshape(batch, seq, -1), cos_t.reshape(batch, seq, LANES), sin_t.reshape(batch, seq, LANES), lg_tab, ret_g,
      sb3, sb3, sb3, sb3, sb3, sb3, sb3, sb3,
      swa3, swa3, swa3, swa3, swa3, bias.reshape(batch, nc, N_HEADS, CHUNK, 2 * CHUNK), q_g, k_g,
      x3, g1, *mods[0], g2, *mods[1], w_gate, w_up, w_out, w_ff1, w_ff2)
    return out.reshape(batch * seq, d)


GATE_COL = 3 * W_BRANCH


def _w_in_kernel(wt_ref, perm_ref, gate_ref, ift_ref):
    one_hot = lambda hit: jnp.where(hit, 1.0, 0.0).astype(BF16)
    eye = one_hot(_iota((LANES, LANES), 0) == _iota((LANES, LANES), 1))
    transpose = lambda rows_bf: _dot_nt(eye, rows_bf).astype(BF16)
    cols = lambda lo, n: transpose(wt_ref[lo:lo + n, :].astype(BF16))
    o_out = GATE_COL + 2 * N_HEADS
    o_ret = o_out + W_BRANCH
    o_sb = o_ret + 4 * W_BRANCH
    o_swa = o_sb + 3 * W_BRANCH
    o_gate = o_swa + 2 * W_BRANCH

    perm_ref[:, 0:GATE_COL] = cols(0, GATE_COL)
    perm_ref[:, GATE_COL:SEG_ML[1]] = cols(o_out, W_BRANCH)
    perm_ref[:, SEG_RET[0]:SEG_RET[1]] = cols(o_ret, 4 * W_BRANCH)
    perm_ref[:, SEG_SB[0]:SEG_SB[1]] = cols(o_sb, 3 * W_BRANCH)
    perm_ref[:, SEG_SWA[0]:SEG_SWA[0] + W_BRANCH] = cols(o_swa, W_BRANCH)
    for n in range(2):
        for h in range(2):
            lo = o_swa + W_BRANCH + n * 2 * HEAD_DIM + h * HEAD_DIM
            head = wt_ref[lo:lo + HEAD_DIM, :].astype(BF16)
            c0 = SEG_SWA[0] + W_BRANCH + (2 * n + h) * LANES
            perm_ref[:, c0:c0 + LANES] = transpose(jnp.concatenate([head, head], axis=0))
    gate_ref[...] = cols(o_gate, gate_ref.shape[1])

    block = wt_ref[GATE_COL:GATE_COL + LANES, :].astype(BF16)
    def placement(n_dst, period):
        dst = _iota((n_dst, LANES), 0)
        j = dst % period
        src = jnp.where(j < 2, j, j + 2) + 2 * (dst // period)
        return one_hot((_iota((n_dst, LANES), 1) == src) & (j < 4))
    ift_ref[...] = _dot(placement(16, 8), block).astype(BF16)
    perm_ref[:, SEG_IFC[0]:SEG_IFC[1]] = transpose(_dot(placement(2 * LANES, LANES), block).astype(BF16))


def _prep_w_in(w_in):
    depth, d, n_cols = w_in.shape
    tk = LANES
    return pl.pallas_call(
        _w_in_kernel,
        grid=(depth, d // tk),
        in_specs=[pl.BlockSpec((None, n_cols, tk), lambda l, i: (l, 0, i))],
        out_specs=[pl.BlockSpec((None, tk, N_IN), lambda l, i: (l, i, 0)),
                   pl.BlockSpec((None, tk, N_HEADS * d), lambda l, i: (l, i, 0)),
                   pl.BlockSpec((None, 16, tk), lambda l, i: (l, 0, i))],
        out_shape=[jax.ShapeDtypeStruct((depth, d, N_IN), BF16), jax.ShapeDtypeStruct((depth, d, N_HEADS * d), BF16),
                   jax.ShapeDtypeStruct((depth, 16, d), BF16)],
        compiler_params=_params(("parallel", "parallel")),
        name="prep_w_in",
    )(jnp.swapaxes(w_in, 1, 2))


CAST_TILE_BYTES = 2 * 1024 * 1024


def _cast_kernel(x_ref, o_ref):
    o_ref[...] = x_ref[...].astype(BF16)


def _to_bf16(a):
    cols = a.shape[-1]
    rows = a.size // cols
    tm = CAST_TILE_BYTES // (4 * cols)
    spec = pl.BlockSpec((tm, cols), lambda i: (i, 0))
    out = pl.pallas_call(
        _cast_kernel,
        grid=(rows // tm,),
        in_specs=[spec],
        out_specs=spec,
        out_shape=jax.ShapeDtypeStruct((rows, cols), BF16),
        compiler_params=_params(("parallel",)),
        name="cast_bf16",
    )(a.reshape(rows, cols))
    return out.reshape(a.shape)


def _gate_bias_layout(gate_b):
    ib, fb = gate_b[0], gate_b[1]
    z = jnp.zeros((LANES - 4,), F32)
    row = jnp.concatenate([ib[0:2], fb[0:2], z, ib[2:4], fb[2:4], z]).reshape(1, 2 * LANES)
    z4 = jnp.zeros((4,), F32)
    col = jnp.concatenate([ib[0:2], fb[0:2], z4, ib[2:4], fb[2:4], z4])
    return row, jnp.broadcast_to(col[:, None], (16, LANES))


def kernel(x, c, positions, w_ada, b_ada, norm_g, w_in, mlstm_conv_w, mlstm_conv_b, mlstm_gate_b,
           ret_norm_g, swa_q_norm_g, swa_k_norm_g, swa_sinks, rel_bias, w_up, w_out, w_ff1, w_ff2):
    batch, seq, d = x.shape
    depth = w_in.shape[0]
    t = batch * seq
    nb = seq // CHUNK

    c8 = jnp.concatenate([c, jnp.zeros((8 - batch, d), F32)], axis=0)
    mod = _ada(c8, w_ada.reshape(depth * 2, d, 3 * d), b_ada.reshape(depth * 2, 1, 3 * d))
    mod = mod[:, :batch].reshape(depth, 2, batch, 3, 1, d)

    pos_col = jnp.broadcast_to(positions.reshape(t, 1), (t, LANES))
    pos_row = positions.reshape(batch * nb, 1, CHUNK)
    half = HEAD_DIM // 2
    inv = ROPE_BASE ** (-(np.arange(LANES) % half).astype(np.float64) / half)
    cos_t, sin_t = _rope_tables(pos_col, jnp.asarray(inv, F32).reshape(1, LANES))
    bias_tab = jnp.concatenate([rel_bias.T, jnp.zeros((N_HEADS, LANES - N_BUCKETS), F32)], axis=1)
    bias = _bias_tiles(bias_tab, pos_col, pos_row, batch, nb)

    log_gamma = np.log(1.0 - np.exp2(-(RET_DECAY_BASE + np.arange(N_HEADS, dtype=np.float64))))
    lg_tab = jnp.asarray(np.broadcast_to(np.repeat(log_gamma, HEAD_DIM).reshape(2, 1, LANES), (2, 8, LANES)), F32)

    w_perm, w_gate, w_ift = _prep_w_in(w_in)
    w_up_bf, w_out_bf, w_ff1_bf, w_ff2_bf = (_to_bf16(w) for w in (w_up, w_out, w_ff1, w_ff2))

    xt = x.reshape(t, d)
    for l in range(depth):
        gb_row, gbt = _gate_bias_layout(mlstm_gate_b[l])
        g1 = norm_g[l, 0].reshape(1, d)
        g2 = norm_g[l, 1].reshape(1, d)
        shift1, scale1, gate1 = mod[l, 0, :, 0], mod[l, 0, :, 1], mod[l, 0, :, 2]
        shift2, scale2, gate2 = mod[l, 1, :, 0], mod[l, 1, :, 1], mod[l, 1, :, 2]

        ml, ifc, ret, sb, swa, ift = _in_proj(xt, g1, scale1, shift1, w_perm, w_ift, gb_row, gbt, l, batch, seq)
        small = (mlstm_conv_w[l], mlstm_conv_b[l].reshape(1, 2 * W_BRANCH), lg_tab, ret_norm_g[l].reshape(1, W_BRANCH),
                 swa_sinks[l], jnp.tile(swa_q_norm_g[l], 2).reshape(1, LANES), jnp.tile(swa_k_norm_g[l], 2).reshape(1, LANES))
        xt = _layer(xt, ((shift1, scale1, gate1), (shift2, scale2, gate2)), g1, g2, ml, ifc, ift, ret, sb, swa,
                    cos_t, sin_t, bias, small, (w_gate, w_up_bf, w_out_bf, w_ff1_bf, w_ff2_bf), l, batch, seq)
    return xt.reshape(batch, seq, d)
```

```python
import functools
import math

import numpy as np
import jax
import jax.numpy as jnp
from jax import lax
from jax.experimental import pallas as pl
from jax.experimental.pallas import tpu as pltpu

F32 = jnp.float32
BF16 = jnp.bfloat16

D_MODEL = 1024
HEAD_DIM = 64
N_HEADS = 4
W_BRANCH = N_HEADS * HEAD_DIM
LANES = 128
CHUNK = 128
CONV_K = 4
D_FF = 4 * D_MODEL
N_BUCKETS = 32
MAX_DIST = 128
ROPE_BASE = 10000.0
RET_DECAY_BASE = 5.0
EPS = 1e-6
NEG = -1e30
F32_EXP_UNDERFLOW = -104.0
VMEM_LIMIT = 56 * 1024 * 1024
LAYER_VMEM_LIMIT = 60 * 1024 * 1024

SEG_ML = (0, 1024)
SEG_IFC = (1024, 1280)
SEG_RET = (1280, 2304)
SEG_SB = (2304, 3072)
SEG_SWA = (3072, 3840)
N_IN = 3840


def _dot(a, b):
    return jnp.dot(a, b, preferred_element_type=F32)


def _dot_nt(a, b):
    return lax.dot_general(a, b, (((1,), (1,)), ((), ())), preferred_element_type=F32)


def _dot_tn(a, b):
    return lax.dot_general(a, b, (((0,), (0,)), ((), ())), preferred_element_type=F32)


def _split2(x):
    hi = x.astype(BF16)
    lo = (x - hi.astype(F32)).astype(BF16)
    return hi, lo


def _split3(x):
    hi = x.astype(BF16)
    r = x - hi.astype(F32)
    mid = r.astype(BF16)
    lo = (r - mid.astype(F32)).astype(BF16)
    return hi, mid, lo


def _iota(shape, axis):
    return lax.broadcasted_iota(jnp.int32, shape, axis)


def _log_sigmoid(x):
    return jnp.minimum(x, 0.0) - jnp.log(1.0 + jnp.exp(-jnp.abs(x)))


def _sigmoid(x):
    return 1.0 / (1.0 + jnp.exp(-x))


def _silu(x):
    return x * _sigmoid(x)


def _norm_mod(x, g, scale, shift):
    ms = jnp.mean(x * x, axis=-1, keepdims=True)
    y = x * lax.rsqrt(ms + EPS)
    return (y * g) * (1.0 + scale) + shift


def _head_mean_sq(x):
    lane = _iota(x.shape, 1)
    sq = x * x
    s0 = jnp.sum(jnp.where(lane < HEAD_DIM, sq, 0.0), axis=-1, keepdims=True)
    s1 = jnp.sum(jnp.where(lane >= HEAD_DIM, sq, 0.0), axis=-1, keepdims=True)
    return jnp.where(lane < HEAD_DIM, s0, s1) * (1.0 / HEAD_DIM)


def _params(sem):
    return pltpu.CompilerParams(dimension_semantics=sem, vmem_limit_bytes=VMEM_LIMIT)


def _interleave(chains, filler=None, every=1):
    def advance(gen):
        try:
            next(gen)
            return True
        except StopIteration:
            return False

    live = list(chains)
    filling = filler is not None
    steps = 0
    while live:
        still = []
        for ch in live:
            if advance(ch):
                still.append(ch)
            steps += 1
            if filling and steps % every == 0:
                filling = advance(filler)
        live = still
    while filling:
        filling = advance(filler)


def _resident(a):
    return pl.BlockSpec(a.shape, lambda *_: (0,) * a.ndim, pipeline_mode=pl.Buffered(1))


def _layer_of(a, layer):
    return pl.BlockSpec((None,) + a.shape[1:], lambda *_: (layer,) + (0,) * (a.ndim - 1),
                        pipeline_mode=pl.Buffered(1))


def _ada_kernel(c_ref, w_ref, b_ref, o_ref):
    c = c_ref[...]
    ch, cl = _split2(_silu(c))
    wh, wl = _split2(w_ref[0])
    o_ref[0] = _dot(ch, wh) + _dot(ch, wl) + _dot(cl, wh) + b_ref[0]


def _ada(c8, w_ada, b_ada):
    n_mod, d, n3 = w_ada.shape
    tn = 1024
    return pl.pallas_call(
        _ada_kernel,
        grid=(n_mod, n3 // tn),
        in_specs=[pl.BlockSpec((8, d), lambda m, n: (0, 0)),
                  pl.BlockSpec((1, d, tn), lambda m, n: (m, 0, n)),
                  pl.BlockSpec((1, 1, tn), lambda m, n: (m, 0, n))],
        out_specs=pl.BlockSpec((1, 8, tn), lambda m, n: (m, 0, n)),
        out_shape=jax.ShapeDtypeStruct((n_mod, 8, n3), F32),
        compiler_params=_params(("parallel", "parallel")),
        name="ada_mod",
    )(c8, w_ada, b_ada)


def _rope_kernel(pos_ref, inv_ref, cos_ref, sin_ref):
    ang = pos_ref[...].astype(F32) * inv_ref[...]
    cos_ref[...] = jnp.cos(ang)
    sin_ref[...] = jnp.sin(ang)


def _rope_tables(pos_b, inv_row):
    t = pos_b.shape[0]
    tm = 1024
    spec = pl.BlockSpec((tm, LANES), lambda i: (i, 0))
    return pl.pallas_call(
        _rope_kernel,
        grid=(t // tm,),
        in_specs=[spec, pl.BlockSpec((1, LANES), lambda i: (0, 0))],
        out_specs=[spec, spec],
        out_shape=[jax.ShapeDtypeStruct((t, LANES), F32)] * 2,
        compiler_params=_params(("parallel",)),
        name="rope_tables",
    )(pos_b, inv_row)


BIAS_BLOCKS = 4


def _bias_kernel(tab_ref, posq_ref, posc_ref, posp_ref, o_ref):
    max_exact = N_BUCKETS // 2
    rows = [jnp.broadcast_to(tab_ref[h:h + 1, :], (CHUNK, LANES)) for h in range(N_HEADS)]
    for r in range(BIAS_BLOCKS):
        pos_q = posq_ref[r * CHUNK:(r + 1) * CHUNK, :]
        pos_prev = posp_ref[0] if r == 0 else posc_ref[r - 1]
        rel = jnp.concatenate([pos_q - pos_prev, pos_q - posc_ref[r]], axis=1)
        n = jnp.maximum(rel, 0)
        nf = jnp.maximum(n, 1).astype(F32)
        large = max_exact + (jnp.log(nf / max_exact) / math.log(MAX_DIST / max_exact)
                             * (N_BUCKETS - max_exact)).astype(jnp.int32)
        large = jnp.minimum(large, N_BUCKETS - 1)
        bucket = jnp.where(n < max_exact, n, large)
        halves = (bucket[:, :CHUNK], bucket[:, CHUNK:])
        for h in range(N_HEADS):
            o_ref[r, h] = jnp.concatenate([jnp.take_along_axis(rows[h], idx, axis=1) for idx in halves], axis=1)


def _bias_tiles(rel_bias, pos_b, pos_r, batch, n_b):
    nblk = batch * n_b
    return pl.pallas_call(
        _bias_kernel,
        grid=(nblk // BIAS_BLOCKS,),
        in_specs=[pl.BlockSpec((N_HEADS, LANES), lambda i: (0, 0)),
                  pl.BlockSpec((BIAS_BLOCKS * CHUNK, LANES), lambda i: (i, 0)),
                  pl.BlockSpec((BIAS_BLOCKS, 1, CHUNK), lambda i: (i, 0, 0)),
                  pl.BlockSpec((1, 1, CHUNK), lambda i: (jnp.maximum(i * BIAS_BLOCKS - 1, 0), 0, 0))],
        out_specs=pl.BlockSpec((BIAS_BLOCKS, N_HEADS, CHUNK, 2 * CHUNK), lambda i: (i, 0, 0, 0)),
        out_shape=jax.ShapeDtypeStruct((nblk, N_HEADS, CHUNK, 2 * CHUNK), F32),
        compiler_params=_params(("parallel",)),
        name="t5_bias",
    )(rel_bias, pos_b, pos_r, pos_r)


def _in_kernel(x_ref, g_ref, sc_ref, sh_ref, w_ref, wt_ref, gb_ref, gbt_ref,
               ml_ref, ifc_ref, ret_ref, sb_ref, swa_ref, ift_ref):
    h = _norm_mod(x_ref[...], g_ref[...], sc_ref[0], sh_ref[0]).astype(BF16)

    def seg(s):
        return _dot(h, w_ref[:, s[0]:s[1]])

    ml_ref[...] = seg(SEG_ML)
    ifc_ref[...] = seg(SEG_IFC) + gb_ref[...]
    ret_ref[...] = seg(SEG_RET)
    sb_ref[...] = seg(SEG_SB).astype(BF16)
    swa_ref[...] = seg(SEG_SWA)
    ift_ref[0] = _dot_nt(wt_ref[...], h) + gbt_ref[:, 0:1]


def _in_proj(x, g, scale, shift, w_perm, w_ift, gb_row, gbt, layer, batch, seq):
    t, d = x.shape
    tm = 512
    tpb = seq // tm
    row = lambda n: pl.BlockSpec((tm, n), lambda i: (i, 0))
    mod = pl.BlockSpec((1, 1, d), lambda i: (i // tpb, 0, 0))
    widths = [s[1] - s[0] for s in (SEG_ML, SEG_IFC, SEG_RET, SEG_SB, SEG_SWA)]
    dts = [F32, F32, F32, BF16, F32]
    return pl.pallas_call(
        _in_kernel,
        grid=(t // tm,),
        in_specs=[row(d), _resident(g), mod, mod, _layer_of(w_perm, layer), _layer_of(w_ift, layer), _resident(gb_row),
                  _resident(gbt)],
        out_specs=[row(n) for n in widths] + [pl.BlockSpec((1, 16, tm), lambda i: (i // tpb, 0, i % tpb))],
        out_shape=[jax.ShapeDtypeStruct((t, n), dt) for n, dt in zip(widths, dts)]
        + [jax.ShapeDtypeStruct((batch, 16, seq), F32)],
        compiler_params=_params(("parallel",)),
        name="in_proj",
    )(x, g, scale, shift, w_perm, w_ift, gb_row, gbt)


def _mlstm_chains(ml_ref, ifc_ref, ift_ref, cw_ref, cb_ref, put, halo, c_st, m_st):
    L = CHUNK
    HALO = 8
    batch = ml_ref.shape[0]
    lane = _iota((L, LANES), 1)
    row = _iota((L, LANES), 0)
    causal = row >= lane
    tri = jnp.where(causal, 1.0, 0.0).astype(BF16)
    upper = jnp.where(row <= lane, 1.0, 0.0).astype(BF16)
    eye = jnp.where(row == lane, 1.0, 0.0).astype(BF16)
    spread = jnp.where(_iota((LANES, 4 * LANES), 0) == _iota((LANES, 4 * LANES), 1) // LANES,
                       1.0, 0.0).astype(BF16)

    def conv_silu(b, col, slot):
        hl = halo.at[slot]
        hl[HALO:HALO + L, :] = ml_ref[b, :, col:col + LANES]
        acc = jnp.zeros((L, LANES), F32) + cb_ref[:, col:col + LANES]
        for j in range(CONV_K):
            off = HALO - (CONV_K - 1) + j
            acc = acc + hl[off:off + L, :] * cw_ref[j:j + 1, col:col + LANES]
        hl[0:HALO, :] = hl[L:L + HALO, :]
        return _silu(acc)

    def pair_chain(b, p):
        pc = p * LANES
        ift = ift_ref[b, p * 8:p * 8 + 8, :]
        lr_h, lr_m, lr_l = _split3(_log_sigmoid(ift))
        a_rows = _dot(lr_h, upper) + _dot(lr_m, upper) + _dot(lr_l, upper)
        g_h, g_m, g_l = _split3(ifc_ref[b, :, pc:pc + LANES])
        gates_b = _dot(g_h, spread) + _dot(g_m, spread) + _dot(g_l, spread)
        q = conv_silu(b, pc, (b * 2 + p) * 2)
        k = conv_silu(b, W_BRANCH + pc, (b * 2 + p) * 2 + 1)
        v = ml_ref[b, :, 2 * W_BRANCH + pc:2 * W_BRANCH + pc + LANES]
        k_bf = k.astype(BF16)
        yield

        lf_h, lf_m, lf_l = _split3(_log_sigmoid(gates_b[:, 2 * LANES:]))
        a_b = _dot(tri, lf_h) + _dot(tri, lf_m) + _dot(tri, lf_l)
        heads = []
        for hh in range(2):
            ch = (b * 2 + p) * 2 + hh
            hm = (lane >= HEAD_DIM) if hh else (lane < HEAD_DIM)
            qh = (jnp.where(hm, q, 0.0) * (HEAD_DIM ** -0.5)).astype(BF16)
            c_in = c_st[ch]
            heads.append(dict(ch=ch, hm=hm, c_in=c_in, s_qk=_dot_nt(qh, k_bf), q_c=_dot(qh, c_in.astype(BF16))))
        yield

        for hh, hd in enumerate(heads):
            i_col = gates_b[:, hh * LANES:(hh + 1) * LANES]
            a_col = a_b[:, hh * LANES:(hh + 1) * LANES]
            a_last = a_col[L - 1:L, :]
            m_in = m_st[hd["ch"], 0:1, :]
            v_aug = jnp.where(hd["hm"], v, 1.0).astype(BF16)

            d_intra = jnp.where(causal, a_col - (a_rows[2 + hh:3 + hh, :] - ift[hh:hh + 1, :]), NEG)
            m_intra = jnp.max(d_intra, axis=-1, keepdims=True)
            m_inter = a_col + m_in
            m_tot = jnp.maximum(m_inter, m_intra)
            pm = jnp.exp(d_intra - m_tot) * hd["s_qk"]
            hd["p_v"] = _dot(pm.astype(BF16), v_aug)
            hd["s_inter"] = jnp.exp(m_inter - m_tot)
            hd["floor"] = jnp.exp(-m_tot)

            g_col = a_last - a_col + i_col
            m_loc = jnp.max(g_col, axis=0, keepdims=True)
            kw = (jnp.where(hd["hm"], k, 0.0) * jnp.exp(g_col - m_loc)).astype(BF16)
            hd["kw_t"] = _dot_nt(eye, kw)
            hd["v_aug"] = v_aug
            m_new = jnp.maximum(a_last + m_in, m_loc)
            hd["w_old"] = jnp.exp(a_last + m_in - m_new)
            hd["w_new"] = jnp.exp(m_loc - m_new)
            m_st[hd["ch"]] = jnp.zeros((8, LANES), F32) + m_new
        yield

        for hd in heads:
            hd["c_loc"] = _dot(hd["kw_t"].astype(BF16), hd["v_aug"])
            hd["o_aug"] = hd["s_inter"] * hd["q_c"] + hd["p_v"]
        den = pltpu.roll(jnp.where(lane < HEAD_DIM, heads[1]["o_aug"], heads[0]["o_aug"]), HEAD_DIM, 1)
        num = jnp.where(lane < HEAD_DIM, heads[0]["o_aug"], heads[1]["o_aug"])
        floor = jnp.where(lane < HEAD_DIM, heads[0]["floor"], heads[1]["floor"])
        h_ml = num / jnp.maximum(jnp.abs(den), floor)
        o_gate = ml_ref[b, :, 3 * W_BRANCH + pc:3 * W_BRANCH + pc + LANES]
        put(b, pc, h_ml * _sigmoid(o_gate))
        yield

        for hd in heads:
            c_st[hd["ch"]] = hd["w_old"] * hd["c_in"] + hd["w_new"] * hd["c_loc"]

    return [pair_chain(b, p) for b in range(batch) for p in range(2)]


def _retention_chains(ret_ref, cos_ref, sin_ref, lg_ref, ng_ref, put, st):
    L = CHUNK
    batch = ret_ref.shape[0]
    lane = _iota((L, LANES), 1)
    row = _iota((L, LANES), 0)
    rowf = row.astype(F32)
    relf = jnp.maximum(row - lane, 0).astype(F32)
    first = (lane % HEAD_DIM) < (HEAD_DIM // 2)
    same_head = (row // HEAD_DIM) == (lane // HEAD_DIM)

    def rot(t, cos, sin):
        partner = jnp.where(first, -pltpu.roll(t, LANES - HEAD_DIM // 2, 1), pltpu.roll(t, HEAD_DIM // 2, 1))
        return t * cos + partner * sin

    decays = []
    for p in range(2):
        lg = lg_ref[p, 0:1, :]
        decays.append(dict(
            q=jnp.exp((rowf + 1.0) * lg), k=jnp.exp((L - 1.0 - rowf) * lg), c=jnp.exp(float(L) * lg),
            intra=[jnp.where(row >= lane, jnp.exp(relf * lg[:, hh * HEAD_DIM:hh * HEAD_DIM + 1]), 0.0)
                   for hh in range(2)]))

    def pair_chain(b, p):
        pc = p * LANES
        dec = decays[p]
        cos = cos_ref[b]
        sin = sin_ref[b]
        q = rot(ret_ref[b, :, pc:pc + LANES], cos, sin)
        k = rot(ret_ref[b, :, W_BRANCH + pc:W_BRANCH + pc + LANES], cos, sin) * (HEAD_DIM ** -0.5)
        v_bf = ret_ref[b, :, 2 * W_BRANCH + pc:2 * W_BRANCH + pc + LANES].astype(BF16)
        k_bf = k.astype(BF16)
        state = st[b * 2 + p]
        s_qk = [_dot_nt(jnp.where((lane >= HEAD_DIM) if hh else (lane < HEAD_DIM), q, 0.0).astype(BF16), k_bf)
                for hh in range(2)]
        inter = _dot(q.astype(BF16), state.astype(BF16))
        kv = _dot_tn((k * dec["k"]).astype(BF16), v_bf)
        yield

        intra = [_dot((s_qk[hh] * dec["intra"][hh]).astype(BF16), v_bf) for hh in range(2)]
        st[b * 2 + p] = dec["c"] * state + jnp.where(same_head, kv, 0.0)
        yield

        o = jnp.where(lane < HEAD_DIM, intra[0], intra[1]) + inter * dec["q"]
        y = o * lax.rsqrt(_head_mean_sq(o) + EPS) * ng_ref[:, pc:pc + LANES]
        gate = ret_ref[b, :, 3 * W_BRANCH + pc:3 * W_BRANCH + pc + LANES]
        put(b, pc, y * _silu(gate))

    return [pair_chain(b, p) for b in range(batch) for p in range(2)]


SB_FUSED_BLOCKS = 3


def _sb_suffix_op():
    L = CHUNK
    r2 = _iota((2 * L, 2 * L), 0) % L
    c2 = _iota((2 * L, 2 * L), 1)
    return jnp.where((c2 >= L) | (r2 >= c2), 1.0, 0.0).astype(BF16)


def _sb_tile_chain(c, b, p, tiles, fresh, qh_s, carry_s, acc_s, suffix_op):
    L = CHUNK
    pc = p * LANES
    zs = [_dot_nt(qh_s[c], k_ref[b, :, pc:pc + LANES]) for k_ref, _, _ in tiles]
    yield
    rrs = []
    for z, (_, _, mask) in zip(zs, tiles):
        lk = -(jnp.maximum(z, 0.0) + jnp.log(1.0 + jnp.exp(-jnp.abs(z))))
        if mask is not None:
            lk = jnp.where(mask, lk, 0.0)
        hi, lo = _split2(lk)
        rrs.append(_dot(jnp.concatenate([hi, lo], axis=1), suffix_op))
    yield
    carry = None if fresh else carry_s[c]
    parts = []
    for z, rr, (_, v_ref, mask) in zip(zs, rrs, tiles):
        e = z + rr[:, :L]
        w = jnp.exp(e if carry is None else e + carry)
        if mask is not None:
            w = jnp.where(mask, w, 0.0)
        parts.append(_dot(w.astype(BF16), v_ref[b, :, pc:pc + LANES]))
        carry = rr[:, L:] if carry is None else carry + rr[:, L:]
    carry_s[c] = carry
    yield
    acc = None if fresh else acc_s[c]
    for part in parts:
        acc = part if acc is None else acc + part
    acc_s[c] = acc


def _sb_chains(i, q_ref, k_refs, v_refs, qh_s, carry_s, acc_s):
    L = CHUNK
    batch = q_ref.shape[0]
    lane = _iota((L, LANES), 1)
    row = _iota((L, LANES), 0)
    heads = [(b, p, hh) for b in range(batch) for p in range(2) for hh in range(2)]
    for c, (b, p, hh) in enumerate(heads):
        hm = (lane >= HEAD_DIM) if hh else (lane < HEAD_DIM)
        qh_s[c] = jnp.where(hm, q_ref[b, :, p * LANES:(p + 1) * LANES] * (HEAD_DIM ** -0.5), 0.0).astype(BF16)
    block_no = jnp.zeros((L, LANES), jnp.int32) + i
    masks = [lane < row] + [block_no >= d for d in range(1, SB_FUSED_BLOCKS)]
    tiles = [(k_refs[d], v_refs[d], masks[d]) for d in range(SB_FUSED_BLOCKS)]
    suffix_op = _sb_suffix_op()
    return [_sb_tile_chain(c, b, p, tiles, True, qh_s, carry_s, acc_s, suffix_op) for c, (b, p, hh) in enumerate(heads)]


def _sb_older_blocks(i, sb_hbm, kbuf, vbuf, sem, qh_s, carry_s, acc_s):
    L = CHUNK
    batch = kbuf.shape[0]
    heads = [(b, p, hh) for b in range(batch) for p in range(2) for hh in range(2)]

    def all_underflow():
        return (jnp.max(jnp.max(carry_s[...], axis=0)) <= F32_EXP_UNDERFLOW).astype(jnp.int32)

    def fetch(j, col, buf, slot):
        return pltpu.make_async_copy(
            sb_hbm.at[:, pl.ds(pl.multiple_of(j * L, L), L), pl.ds(col * W_BRANCH, W_BRANCH)], buf, sem.at[slot])

    def cond(st):
        return jnp.logical_and(st[0] >= 0, st[1] == 0)

    def body(st):
        copies = [fetch(st[0], 1, kbuf, 0), fetch(st[0], 2, vbuf, 1)]
        for cp in copies:
            cp.start()
        for cp in copies:
            cp.wait()
        suffix_op = _sb_suffix_op()
        _interleave([_sb_tile_chain(c, b, p, [(kbuf, vbuf, None)], False, qh_s, carry_s, acc_s, suffix_op)
                     for c, (b, p, hh) in enumerate(heads)])
        return (st[0] - 1, all_underflow())

    lax.while_loop(cond, body, (i - SB_FUSED_BLOCKS, all_underflow()))


def _swa_chains(i, sink_ref, q_ref, kc_ref, kp_ref, vc_ref, vp_ref, bias_ref, qg_ref, kg_ref, put):
    L = CHUNK
    batch = q_ref.shape[0]
    lane = _iota((L, LANES), 1)
    t = _iota((L, 2 * L), 0)
    j = _iota((L, 2 * L), 1)
    dist = t + L - j
    block_no = jnp.zeros((L, 2 * L), jnp.int32) + i
    valid = (dist >= 0) & (dist < L) & ((j >= L) | (block_no > 0))

    def qk_norm(x, g_ref):
        return x * lax.rsqrt(_head_mean_sq(x) + EPS) * g_ref[...]

    def group_chain(b, g):
        sl = slice(g * LANES, (g + 1) * LANES)
        qn = qk_norm(q_ref[b, :, sl], qg_ref) * (HEAD_DIM ** -0.5)
        kcat = jnp.concatenate([qk_norm(kp_ref[b, :, sl], kg_ref), qk_norm(kc_ref[b, :, sl], kg_ref)],
                               axis=0).astype(BF16)
        qk = [_dot_nt(jnp.where((lane >= HEAD_DIM) if r else (lane < HEAD_DIM), qn, 0.0).astype(BF16), kcat)
              for r in range(2)]
        yield
        vcat = jnp.concatenate([vp_ref[b, :, sl], vc_ref[b, :, sl]], axis=0).astype(BF16)
        outs = []
        for r in range(2):
            hq = 2 * g + r
            logits = jnp.where(valid, qk[r] + bias_ref[b, 0, hq], NEG)
            sink = sink_ref[hq]
            m = jnp.maximum(jnp.max(logits, axis=-1, keepdims=True), sink)
            pr = jnp.exp(logits - m)
            den = jnp.sum(pr, axis=-1, keepdims=True) + jnp.exp(sink - m)
            outs.append(_dot((pr / den).astype(BF16), vcat))
        yield
        put(b, g * LANES, jnp.where(lane < HEAD_DIM, outs[0], outs[1]))

    return [group_chain(b, g) for b in range(batch) for g in range(2)]


DENSE_COLS = 512


def _dense_chain(x_ref, g1_ref, mod1, g2_ref, mod2, branch, wg_ref, wu_ref, wo_ref, w1_ref, w2_ref, out_ref,
                 h_s, merged_s, x1_s, ff_s):
    L = CHUNK
    batch = x_ref.shape[0]
    rows = lambda b: slice(b * L, (b + 1) * L)
    sh1, sc1, gt1 = mod1
    sh2, sc2, gt2 = mod2
    for b in range(batch):
        h_s[rows(b), :] = _norm_mod(x_ref[b], g1_ref[...], sc1[b], sh1[b]).astype(BF16)
    yield
    n_col = D_MODEL // DENSE_COLS
    units = [(j, n) for j in range(n_col) for n in range(N_HEADS)]
    logits = lambda j, n: _dot(h_s[...], wg_ref[:, n * D_MODEL + j * DENSE_COLS:n * D_MODEL + (j + 1) * DENSE_COLS])
    pending = logits(*units[0])
    acc = None
    for u, (j, n) in enumerate(units):
        cs = slice(j * DENSE_COLS, (j + 1) * DENSE_COLS)
        gate_logits = pending
        if u + 1 < len(units):
            pending = logits(*units[u + 1])
        term = _sigmoid(gate_logits) * _dot(branch(n), wu_ref[n, :, cs])
        acc = term if n == 0 else acc + term
        if n == N_HEADS - 1:
            merged_s[:, cs] = acc.astype(BF16)
        yield
    proj = lambda j: _dot(merged_s[...], wo_ref[:, j * DENSE_COLS:(j + 1) * DENSE_COLS])
    pending = proj(0)
    for j in range(n_col):
        cs = slice(j * DENSE_COLS, (j + 1) * DENSE_COLS)
        mix = pending
        if j + 1 < n_col:
            pending = proj(j + 1)
        for b in range(batch):
            x1_s[rows(b), cs] = x_ref[b, :, cs] + gt1[b][:, cs] * mix[rows(b)]
        yield
    for b in range(batch):
        h_s[rows(b), :] = _norm_mod(x1_s[rows(b), :], g2_ref[...], sc2[b], sh2[b]).astype(BF16)
    yield
    n_ff = D_FF // DENSE_COLS
    up = lambda n: _dot(h_s[...], w1_ref[:, n * DENSE_COLS:(n + 1) * DENSE_COLS])
    pending = up(0)
    for n in range(n_ff):
        a = jnp.maximum(pending, 0.0)
        if n + 1 < n_ff:
            pending = up(n + 1)
        part = _dot((a * a).astype(BF16), w2_ref[n * DENSE_COLS:(n + 1) * DENSE_COLS, :])
        ff_s[...] = part if n == 0 else ff_s[...] + part
        yield
    for b in range(batch):
        out_ref[b] = x1_s[rows(b), :] + gt2[b] * ff_s[rows(b), :]


def _layer_kernel(sink_ref, ml_ref, ifc_ref, ift_ref, cw_ref, cb_ref,
                  ret_ref, cos_ref, sin_ref, lg_ref, ng_ref,
                  sbq_ref, sbk0_ref, sbk1_ref, sbk2_ref, sbv0_ref, sbv1_ref, sbv2_ref, sb_hbm,
                  swq_ref, swkc_ref, swkp_ref, swvc_ref, swvp_ref, bias_ref, qg_ref, kg_ref,
                  x_ref, g1_ref, sh1_ref, sc1_ref, gt1_ref, g2_ref, sh2_ref, sc2_ref, gt2_ref,
                  wg_ref, wu_ref, wo_ref, w1_ref, w2_ref,
                  out_ref,
                  halo, c_st, m_st, ret_st, qh_s, carry_s, acc_s, kbuf, vbuf, sem,
                  branch_s, h_s, merged_s, x1_s, ff_s, *, n_chunks):
    L = CHUNK
    batch = ml_ref.shape[0]
    step = pl.program_id(0)
    i = jnp.minimum(step, n_chunks - 1)
    slot = step % 2

    @pl.when(step == 0)
    def _init():
        halo[:, 0:8, :] = jnp.zeros((halo.shape[0], 8, LANES), F32)
        c_st[...] = jnp.zeros(c_st.shape, F32)
        m_st[...] = jnp.zeros(m_st.shape, F32)
        ret_st[...] = jnp.zeros(ret_st.shape, F32)
        branch_s[1] = jnp.zeros(branch_s.shape[1:], BF16)

    def put(n):
        def write(b, col, value):
            branch_s[slot, n, b * L:(b + 1) * L, col:col + LANES] = value.astype(BF16)
        return write

    chains = (_mlstm_chains(ml_ref, ifc_ref, ift_ref, cw_ref, cb_ref, put(0), halo, c_st, m_st)
              + _retention_chains(ret_ref, cos_ref, sin_ref, lg_ref, ng_ref, put(1), ret_st)
              + _sb_chains(i, sbq_ref, (sbk0_ref, sbk1_ref, sbk2_ref), (sbv0_ref, sbv1_ref, sbv2_ref),
                           qh_s, carry_s, acc_s)
              + _swa_chains(i, sink_ref, swq_ref, swkc_ref, swkp_ref, swvc_ref, swvp_ref, bias_ref, qg_ref, kg_ref,
                            put(3)))
    dense = _dense_chain(x_ref, g1_ref, (sh1_ref, sc1_ref, gt1_ref), g2_ref, (sh2_ref, sc2_ref, gt2_ref),
                         lambda n: branch_s[1 - slot, n], wg_ref, wu_ref, wo_ref, w1_ref, w2_ref, out_ref,
                         h_s, merged_s, x1_s, ff_s)
    _interleave(chains, filler=dense)

    _sb_older_blocks(i, sb_hbm, kbuf, vbuf, sem, qh_s, carry_s, acc_s)
    lane = _iota((L, LANES), 1)
    for b in range(batch):
        for p in range(2):
            c0 = (b * 2 + p) * 2
            put(2)(b, p * LANES, jnp.where(lane < HEAD_DIM, acc_s[c0], acc_s[c0 + 1]))


def _layer(xt, mods, g1, g2, ml, ifc, ift, ret, sb, swa, cos_t, sin_t, bias, small, weights, layer, batch, seq):
    d = xt.shape[1]
    nc = seq // CHUNK
    n_chain = batch * N_HEADS
    conv_w, conv_b, lg_tab, ret_g, sinks, q_g, k_g = small
    w_gate, w_up, w_out, w_ff1, w_ff2 = weights
    mix = lambda c: jnp.minimum(c, nc - 1)
    lag = lambda c: jnp.maximum(c - 1, 0)
    chunk = lambda n, blk=0, back=0: pl.BlockSpec(
        (batch, CHUNK, n), lambda c: (0, jnp.maximum(mix(c) - back, 0), blk))
    sb3 = sb.reshape(batch, seq, -1)
    swa3 = swa.reshape(batch, seq, -1)
    x3 = xt.reshape(batch, seq, d)
    rowblk = pl.BlockSpec((batch, CHUNK, d), lambda c: (0, lag(c), 0))
    in_specs = (
        [pl.BlockSpec(memory_space=pltpu.SMEM),
         chunk(4 * W_BRANCH), chunk(2 * LANES), pl.BlockSpec((batch, 16, CHUNK), lambda c: (0, 0, mix(c))),
         _resident(conv_w), _resident(conv_b),
         chunk(4 * W_BRANCH), chunk(LANES), chunk(LANES), _resident(lg_tab), _resident(ret_g),
         chunk(W_BRANCH, 0)]
        + [chunk(W_BRANCH, 1, back) for back in range(SB_FUSED_BLOCKS)]
        + [chunk(W_BRANCH, 2, back) for back in range(SB_FUSED_BLOCKS)]
        + [pl.BlockSpec(memory_space=pl.ANY),
           chunk(W_BRANCH, 0), chunk(W_BRANCH, 1), chunk(W_BRANCH, 1, 1), chunk(W_BRANCH, 2), chunk(W_BRANCH, 2, 1),
           pl.BlockSpec((batch, 1, N_HEADS, CHUNK, 2 * CHUNK), lambda c: (0, mix(c), 0, 0, 0)),
           _resident(q_g), _resident(k_g),
           rowblk, _resident(g1)] + [_resident(m) for m in mods[0]] + [_resident(g2)] + [_resident(m) for m in mods[1]]
        + [_layer_of(w, layer) for w in (w_gate, w_up, w_out, w_ff1, w_ff2)])
    rows = batch * CHUNK
    out = pl.pallas_call(
        functools.partial(_layer_kernel, n_chunks=nc),
        grid=(nc + 1,),
        in_specs=in_specs,
        out_specs=rowblk,
        out_shape=jax.ShapeDtypeStruct((batch, seq, d), F32),
        scratch_shapes=[pltpu.VMEM((n_chain, CHUNK + 8, LANES), F32), pltpu.VMEM((n_chain, LANES, LANES), F32),
                        pltpu.VMEM((n_chain, 8, LANES), F32), pltpu.VMEM((batch * 2, LANES, LANES), F32),
                        pltpu.VMEM((n_chain, CHUNK, LANES), BF16), pltpu.VMEM((n_chain, CHUNK, LANES), F32),
                        pltpu.VMEM((n_chain, CHUNK, LANES), F32),
                        pltpu.VMEM((batch, CHUNK, W_BRANCH), BF16), pltpu.VMEM((batch, CHUNK, W_BRANCH), BF16),
                        pltpu.SemaphoreType.DMA((2,)),
                        pltpu.VMEM((2, N_HEADS, rows, W_BRANCH), BF16), pltpu.VMEM((rows, d), BF16),
                        pltpu.VMEM((rows, d), BF16), pltpu.VMEM((rows, d), F32), pltpu.VMEM((rows, d), F32)],
        compiler_params=pltpu.CompilerParams(dimension_semantics=("arbitrary",), vmem_limit_bytes=LAYER_VMEM_LIMIT),
        name="layer",
    )(sinks, ml.reshape(batch, seq, -1), ifc.reshape(batch, seq, -1), ift, conv_w, conv_b,
      ret.reshape(batch, seq, -1), cos_t.reshape(batch, seq, LANES), sin_t.reshape(batch, seq, LANES), lg_tab, ret_g,
      sb3, sb3, sb3, sb3, sb3, sb3, sb3, sb3,
      swa3, swa3, swa3, swa3, swa3, bias.reshape(batch, nc, N_HEADS, CHUNK, 2 * CHUNK), q_g, k_g,
      x3, g1, *mods[0], g2, *mods[1], w_gate, w_up, w_out, w_ff1, w_ff2)
    return out.reshape(batch * seq, d)


GATE_COL = 3 * W_BRANCH


def _w_in_kernel(wt_ref, perm_ref, gate_ref, ift_ref):
    one_hot = lambda hit: jnp.where(hit, 1.0, 0.0).astype(BF16)
    eye = one_hot(_iota((LANES, LANES), 0) == _iota((LANES, LANES), 1))
    transpose = lambda rows_bf: _dot_nt(eye, rows_bf).astype(BF16)
    cols = lambda lo, n: transpose(wt_ref[lo:lo + n, :].astype(BF16))
    o_out = GATE_COL + 2 * N_HEADS
    o_ret = o_out + W_BRANCH
    o_sb = o_ret + 4 * W_BRANCH
    o_swa = o_sb + 3 * W_BRANCH
    o_gate = o_swa + 2 * W_BRANCH

    perm_ref[:, 0:GATE_COL] = cols(0, GATE_COL)
    perm_ref[:, GATE_COL:SEG_ML[1]] = cols(o_out, W_BRANCH)
    perm_ref[:, SEG_RET[0]:SEG_RET[1]] = cols(o_ret, 4 * W_BRANCH)
    perm_ref[:, SEG_SB[0]:SEG_SB[1]] = cols(o_sb, 3 * W_BRANCH)
    perm_ref[:, SEG_SWA[0]:SEG_SWA[0] + W_BRANCH] = cols(o_swa, W_BRANCH)
    for n in range(2):
        for h in range(2):
            lo = o_swa + W_BRANCH + n * 2 * HEAD_DIM + h * HEAD_DIM
            head = wt_ref[lo:lo + HEAD_DIM, :].astype(BF16)
            c0 = SEG_SWA[0] + W_BRANCH + (2 * n + h) * LANES
            perm_ref[:, c0:c0 + LANES] = transpose(jnp.concatenate([head, head], axis=0))
    gate_ref[...] = cols(o_gate, gate_ref.shape[1])

    block = wt_ref[GATE_COL:GATE_COL + LANES, :].astype(BF16)
    def placement(n_dst, period):
        dst = _iota((n_dst, LANES), 0)
        j = dst % period
        src = jnp.where(j < 2, j, j + 2) + 2 * (dst // period)
        return one_hot((_iota((n_dst, LANES), 1) == src) & (j < 4))
    ift_ref[...] = _dot(placement(16, 8), block).astype(BF16)
    perm_ref[:, SEG_IFC[0]:SEG_IFC[1]] = transpose(_dot(placement(2 * LANES, LANES), block).astype(BF16))


def _prep_w_in(w_in):
    depth, d, n_cols = w_in.shape
    tk = LANES
    return pl.pallas_call(
        _w_in_kernel,
        grid=(depth, d // tk),
        in_specs=[pl.BlockSpec((None, n_cols, tk), lambda l, i: (l, 0, i))],
        out_specs=[pl.BlockSpec((None, tk, N_IN), lambda l, i: (l, i, 0)),
                   pl.BlockSpec((None, tk, N_HEADS * d), lambda l, i: (l, i, 0)),
                   pl.BlockSpec((None, 16, tk), lambda l, i: (l, 0, i))],
        out_shape=[jax.ShapeDtypeStruct((depth, d, N_IN), BF16), jax.ShapeDtypeStruct((depth, d, N_HEADS * d), BF16),
                   jax.ShapeDtypeStruct((depth, 16, d), BF16)],
        compiler_params=_params(("parallel", "parallel")),
        name="prep_w_in",
    )(jnp.swapaxes(w_in, 1, 2))


CAST_TILE_BYTES = 2 * 1024 * 1024


def _cast_kernel(x_ref, o_ref):
    o_ref[...] = x_ref[...].astype(BF16)


def _to_bf16(a):
    cols = a.shape[-1]
    rows = a.size // cols
    tm = CAST_TILE_BYTES // (4 * cols)
    spec = pl.BlockSpec((tm, cols), lambda i: (i, 0))
    out = pl.pallas_call(
        _cast_kernel,
        grid=(rows // tm,),
        in_specs=[spec],
        out_specs=spec,
        out_shape=jax.ShapeDtypeStruct((rows, cols), BF16),
        compiler_params=_params(("parallel",)),
        name="cast_bf16",
    )(a.reshape(rows, cols))
    return out.reshape(a.shape)


def _gate_bias_layout(gate_b):
    ib, fb = gate_b[0], gate_b[1]
    z = jnp.zeros((LANES - 4,), F32)
    row = jnp.concatenate([ib[0:2], fb[0:2], z, ib[2:4], fb[2:4], z]).reshape(1, 2 * LANES)
    z4 = jnp.zeros((4,), F32)
    col = jnp.concatenate([ib[0:2], fb[0:2], z4, ib[2:4], fb[2:4], z4])
    return row, jnp.broadcast_to(col[:, None], (16, LANES))


def kernel(x, c, positions, w_ada, b_ada, norm_g, w_in, mlstm_conv_w, mlstm_conv_b, mlstm_gate_b,
           ret_norm_g, swa_q_norm_g, swa_k_norm_g, swa_sinks, rel_bias, w_up, w_out, w_ff1, w_ff2):
    batch, seq, d = x.shape
    depth = w_in.shape[0]
    t = batch * seq
    nb = seq // CHUNK

    c8 = jnp.concatenate([c, jnp.zeros((8 - batch, d), F32)], axis=0)
    mod = _ada(c8, w_ada.reshape(depth * 2, d, 3 * d), b_ada.reshape(depth * 2, 1, 3 * d))
    mod = mod[:, :batch].reshape(depth, 2, batch, 3, 1, d)

    pos_col = jnp.broadcast_to(positions.reshape(t, 1), (t, LANES))
    pos_row = positions.reshape(batch * nb, 1, CHUNK)
    half = HEAD_DIM // 2
    inv = ROPE_BASE ** (-(np.arange(LANES) % half).astype(np.float64) / half)
    cos_t, sin_t = _rope_tables(pos_col, jnp.asarray(inv, F32).reshape(1, LANES))
    bias_tab = jnp.concatenate([rel_bias.T, jnp.zeros((N_HEADS, LANES - N_BUCKETS), F32)], axis=1)
    bias = _bias_tiles(bias_tab, pos_col, pos_row, batch, nb)

    log_gamma = np.log(1.0 - np.exp2(-(RET_DECAY_BASE + np.arange(N_HEADS, dtype=np.float64))))
    lg_tab = jnp.asarray(np.broadcast_to(np.repeat(log_gamma, HEAD_DIM).reshape(2, 1, LANES), (2, 8, LANES)), F32)

    w_perm, w_gate, w_ift = _prep_w_in(w_in)
    w_up_bf, w_out_bf, w_ff1_bf, w_ff2_bf = (_to_bf16(w) for w in (w_up, w_out, w_ff1, w_ff2))

    xt = x.reshape(t, d)
    for l in range(depth):
        gb_row, gbt = _gate_bias_layout(mlstm_gate_b[l])
        g1 = norm_g[l, 0].reshape(1, d)
        g2 = norm_g[l, 1].reshape(1, d)
        shift1, scale1, gate1 = mod[l, 0, :, 0], mod[l, 0, :, 1], mod[l, 0, :, 2]
        shift2, scale2, gate2 = mod[l, 1, :, 0], mod[l, 1, :, 1], mod[l, 1, :, 2]

        ml, ifc, ret, sb, swa, ift = _in_proj(xt, g1, scale1, shift1, w_perm, w_ift, gb_row, gbt, l, batch, seq)
        small = (mlstm_conv_w[l], mlstm_conv_b[l].reshape(1, 2 * W_BRANCH), lg_tab, ret_norm_g[l].reshape(1, W_BRANCH),
                 swa_sinks[l], jnp.tile(swa_q_norm_g[l], 2).reshape(1, LANES), jnp.tile(swa_k_norm_g[l], 2).reshape(1, LANES))
        xt = _layer(xt, ((shift1, scale1, gate1), (shift2, scale2, gate2)), g1, g2, ml, ifc, ift, ret, sb, swa,
                    cos_t, sin_t, bias, small, (w_gate, w_up_bf, w_out_bf, w_ff1_bf, w_ff2_bf), l, batch, seq)
    return xt.reshape(batch, seq, d)
```

```python
import functools
import math

import numpy as np
import jax
import jax.numpy as jnp
from jax import lax
from jax.experimental import pallas as pl
from jax.experimental.pallas import tpu as pltpu

F32 = jnp.float32
BF16 = jnp.bfloat16

D_MODEL = 1024
HEAD_DIM = 64
N_HEADS = 4
W_BRANCH = N_HEADS * HEAD_DIM
LANES = 128
CHUNK = 128
CONV_K = 4
D_FF = 4 * D_MODEL
N_BUCKETS = 32
MAX_DIST = 128
ROPE_BASE = 10000.0
RET_DECAY_BASE = 5.0
EPS = 1e-6
NEG = -1e30
F32_EXP_UNDERFLOW = -104.0
VMEM_LIMIT = 56 * 1024 * 1024
LAYER_VMEM_LIMIT = 60 * 1024 * 1024

SEG_ML = (0, 1024)
SEG_IFC = (1024, 1280)
SEG_RET = (1280, 2304)
SEG_SB = (2304, 3072)
SEG_SWA = (3072, 3840)
N_IN = 3840


def _dot(a, b):
    return jnp.dot(a, b, preferred_element_type=F32)


def _dot_nt(a, b):
    return lax.dot_general(a, b, (((1,), (1,)), ((), ())), preferred_element_type=F32)


def _dot_tn(a, b):
    return lax.dot_general(a, b, (((0,), (0,)), ((), ())), preferred_element_type=F32)


def _split2(x):
    hi = x.astype(BF16)
    lo = (x - hi.astype(F32)).astype(BF16)
    return hi, lo


def _split3(x):
    hi = x.astype(BF16)
    r = x - hi.astype(F32)
    mid = r.astype(BF16)
    lo = (r - mid.astype(F32)).astype(BF16)
    return hi, mid, lo


def _iota(shape, axis):
    return lax.broadcasted_iota(jnp.int32, shape, axis)


def _log_sigmoid(x):
    return jnp.minimum(x, 0.0) - jnp.log(1.0 + jnp.exp(-jnp.abs(x)))


def _sigmoid(x):
    return 1.0 / (1.0 + jnp.exp(-x))


def _silu(x):
    return x * _sigmoid(x)


def _norm_mod(x, g, scale, shift):
    ms = jnp.mean(x * x, axis=-1, keepdims=True)
    y = x * lax.rsqrt(ms + EPS)
    return (y * g) * (1.0 + scale) + shift


def _head_mean_sq(x):
    lane = _iota(x.shape, 1)
    sq = x * x
    s0 = jnp.sum(jnp.where(lane < HEAD_DIM, sq, 0.0), axis=-1, keepdims=True)
    s1 = jnp.sum(jnp.where(lane >= HEAD_DIM, sq, 0.0), axis=-1, keepdims=True)
    return jnp.where(lane < HEAD_DIM, s0, s1) * (1.0 / HEAD_DIM)


def _params(sem):
    return pltpu.CompilerParams(dimension_semantics=sem, vmem_limit_bytes=VMEM_LIMIT)


def _interleave(chains, filler=None, every=1):
    def advance(gen):
        try:
            next(gen)
            return True
        except StopIteration:
            return False

    live = list(chains)
    filling = filler is not None
    steps = 0
    while live:
        still = []
        for ch in live:
            if advance(ch):
                still.append(ch)
            steps += 1
            if filling and steps % every == 0:
                filling = advance(filler)
        live = still
    while filling:
        filling = advance(filler)


def _resident(a):
    return pl.BlockSpec(a.shape, lambda *_: (0,) * a.ndim, pipeline_mode=pl.Buffered(1))


def _layer_of(a, layer):
    return pl.BlockSpec((None,) + a.shape[1:], lambda *_: (layer,) + (0,) * (a.ndim - 1),
                        pipeline_mode=pl.Buffered(1))


def _ada_kernel(c_ref, w_ref, b_ref, o_ref):
    c = c_ref[...]
    ch, cl = _split2(_silu(c))
    wh, wl = _split2(w_ref[0])
    o_ref[0] = _dot(ch, wh) + _dot(ch, wl) + _dot(cl, wh) + b_ref[0]


def _ada(c8, w_ada, b_ada):
    n_mod, d, n3 = w_ada.shape
    tn = 1024
    return pl.pallas_call(
        _ada_kernel,
        grid=(n_mod, n3 // tn),
        in_specs=[pl.BlockSpec((8, d), lambda m, n: (0, 0)),
                  pl.BlockSpec((1, d, tn), lambda m, n: (m, 0, n)),
                  pl.BlockSpec((1, 1, tn), lambda m, n: (m, 0, n))],
        out_specs=pl.BlockSpec((1, 8, tn), lambda m, n: (m, 0, n)),
        out_shape=jax.ShapeDtypeStruct((n_mod, 8, n3), F32),
        compiler_params=_params(("parallel", "parallel")),
        name="ada_mod",
    )(c8, w_ada, b_ada)


def _t5_bias(tab_ref, pos_q, pos_prev, pos_cur):
    max_exact = N_BUCKETS // 2
    rel = jnp.concatenate([pos_q - pos_prev, pos_q - pos_cur], axis=1)
    n = jnp.maximum(rel, 0)
    nf = jnp.maximum(n, 1).astype(F32)
    large = max_exact + (jnp.log(nf / max_exact) / math.log(MAX_DIST / max_exact)
                         * (N_BUCKETS - max_exact)).astype(jnp.int32)
    large = jnp.minimum(large, N_BUCKETS - 1)
    bucket = jnp.where(n < max_exact, n, large)
    halves = (bucket[:, :CHUNK], bucket[:, CHUNK:])
    tiles = []
    for h in range(N_HEADS):
        row = jnp.broadcast_to(tab_ref[h:h + 1, :], (CHUNK, LANES))
        tiles.append(jnp.concatenate([jnp.take_along_axis(row, idx, axis=1) for idx in halves], axis=1))
    return tiles


def _in_kernel(x_ref, g_ref, sc_ref, sh_ref, w_ref, wt_ref, gb_ref, gbt_ref,
               ml_ref, ifc_ref, ret_ref, sb_ref, swa_ref, ift_ref):
    h = _norm_mod(x_ref[...], g_ref[...], sc_ref[0], sh_ref[0]).astype(BF16)

    def seg(s):
        return _dot(h, w_ref[:, s[0]:s[1]])

    ml_ref[...] = seg(SEG_ML)
    ifc_ref[...] = seg(SEG_IFC) + gb_ref[...]
    ret_ref[...] = seg(SEG_RET)
    sb_ref[...] = seg(SEG_SB).astype(BF16)
    swa_ref[...] = seg(SEG_SWA)
    ift_ref[0] = _dot_nt(wt_ref[...], h) + gbt_ref[:, 0:1]


def _in_proj(x, g, scale, shift, w_perm, w_ift, gb_row, gbt, layer, batch, seq):
    t, d = x.shape
    tm = 512
    tpb = seq // tm
    row = lambda n: pl.BlockSpec((tm, n), lambda i: (i, 0))
    mod = pl.BlockSpec((1, 1, d), lambda i: (i // tpb, 0, 0))
    widths = [s[1] - s[0] for s in (SEG_ML, SEG_IFC, SEG_RET, SEG_SB, SEG_SWA)]
    dts = [F32, F32, F32, BF16, F32]
    return pl.pallas_call(
        _in_kernel,
        grid=(t // tm,),
        in_specs=[row(d), _resident(g), mod, mod, _layer_of(w_perm, layer), _layer_of(w_ift, layer), _resident(gb_row),
                  _resident(gbt)],
        out_specs=[row(n) for n in widths] + [pl.BlockSpec((1, 16, tm), lambda i: (i // tpb, 0, i % tpb))],
        out_shape=[jax.ShapeDtypeStruct((t, n), dt) for n, dt in zip(widths, dts)]
        + [jax.ShapeDtypeStruct((batch, 16, seq), F32)],
        compiler_params=_params(("parallel",)),
        name="in_proj",
    )(x, g, scale, shift, w_perm, w_ift, gb_row, gbt)


def _mlstm_chains(ml_ref, ifc_ref, ift_ref, cw_ref, cb_ref, put, halo, c_st, m_st):
    L = CHUNK
    HALO = 8
    batch = ml_ref.shape[0]
    lane = _iota((L, LANES), 1)
    row = _iota((L, LANES), 0)
    causal = row >= lane
    tri = jnp.where(causal, 1.0, 0.0).astype(BF16)
    upper = jnp.where(row <= lane, 1.0, 0.0).astype(BF16)

    def conv_silu(b, col, slot):
        hl = halo.at[slot]
        hl[HALO:HALO + L, :] = ml_ref[b, :, col:col + LANES]
        acc = jnp.zeros((L, LANES), F32) + cb_ref[:, col:col + LANES]
        for j in range(CONV_K):
            off = HALO - (CONV_K - 1) + j
            acc = acc + hl[off:off + L, :] * cw_ref[j:j + 1, col:col + LANES]
        hl[0:HALO, :] = hl[L:L + HALO, :]
        return _silu(acc)

    def pair_chain(b, p):
        pc = p * LANES
        ift = ift_ref[b, p * 8:p * 8 + 8, :]
        lr_h, lr_m, lr_l = _split3(_log_sigmoid(ift))
        a_rows = _dot(lr_h, upper) + _dot(lr_m, upper) + _dot(lr_l, upper)
        ifc = ifc_ref[b, :, pc:pc + LANES]
        gates_b = jnp.concatenate([jnp.broadcast_to(ifc[:, j:j + 1], (L, LANES)) for j in range(4)], axis=1)
        q = conv_silu(b, pc, (b * 2 + p) * 2)
        k = conv_silu(b, W_BRANCH + pc, (b * 2 + p) * 2 + 1)
        v = ml_ref[b, :, 2 * W_BRANCH + pc:2 * W_BRANCH + pc + LANES]
        k_bf = k.astype(BF16)
        yield

        lf_h, lf_m, lf_l = _split3(_log_sigmoid(gates_b[:, 2 * LANES:]))
        a_b = _dot(tri, lf_h) + _dot(tri, lf_m) + _dot(tri, lf_l)
        heads = []
        for hh in range(2):
            ch = (b * 2 + p) * 2 + hh
            hm = (lane >= HEAD_DIM) if hh else (lane < HEAD_DIM)
            qh = (jnp.where(hm, q, 0.0) * (HEAD_DIM ** -0.5)).astype(BF16)
            c_in = c_st[ch]
            heads.append(dict(ch=ch, hm=hm, c_in=c_in, s_qk=_dot_nt(qh, k_bf), q_c=_dot(qh, c_in.astype(BF16))))
        yield

        for hh, hd in enumerate(heads):
            i_col = gates_b[:, hh * LANES:(hh + 1) * LANES]
            a_col = a_b[:, hh * LANES:(hh + 1) * LANES]
            a_last = a_col[L - 1:L, :]
            m_in = m_st[hd["ch"], 0:1, :]
            v_aug = jnp.where(hd["hm"], v, 1.0).astype(BF16)

            d_intra = jnp.where(causal, a_col - (a_rows[2 + hh:3 + hh, :] - ift[hh:hh + 1, :]), NEG)
            m_intra = jnp.max(d_intra, axis=-1, keepdims=True)
            m_inter = a_col + m_in
            m_tot = jnp.maximum(m_inter, m_intra)
            pm = jnp.exp(d_intra - m_tot) * hd["s_qk"]
            hd["p_v"] = _dot(pm.astype(BF16), v_aug)
            hd["s_inter"] = jnp.exp(m_inter - m_tot)
            hd["floor"] = jnp.exp(-m_tot)

            g_col = a_last - a_col + i_col
            m_loc = jnp.max(g_col, axis=0, keepdims=True)
            kw = (jnp.where(hd["hm"], k, 0.0) * jnp.exp(g_col - m_loc)).astype(BF16)
            hd["c_loc"] = _dot_tn(kw, v_aug)
            m_new = jnp.maximum(a_last + m_in, m_loc)
            hd["w_old"] = jnp.exp(a_last + m_in - m_new)
            hd["w_new"] = jnp.exp(m_loc - m_new)
            m_st[hd["ch"]] = jnp.zeros((8, LANES), F32) + m_new
        yield

        for hd in heads:
            hd["o_aug"] = hd["s_inter"] * hd["q_c"] + hd["p_v"]
        den = pltpu.roll(jnp.where(lane < HEAD_DIM, heads[1]["o_aug"], heads[0]["o_aug"]), HEAD_DIM, 1)
        num = jnp.where(lane < HEAD_DIM, heads[0]["o_aug"], heads[1]["o_aug"])
        floor = jnp.where(lane < HEAD_DIM, heads[0]["floor"], heads[1]["floor"])
        h_ml = num / jnp.maximum(jnp.abs(den), floor)
        o_gate = ml_ref[b, :, 3 * W_BRANCH + pc:3 * W_BRANCH + pc + LANES]
        put(b, pc, h_ml * _sigmoid(o_gate))
        yield

        for hd in heads:
            c_st[hd["ch"]] = hd["w_old"] * hd["c_in"] + hd["w_new"] * hd["c_loc"]

    return [pair_chain(b, p) for b in range(batch) for p in range(2)]


def _retention_chains(ret_ref, posq_ref, inv_ref, lg_ref, ng_ref, put, st):
    L = CHUNK
    batch = ret_ref.shape[0]
    lane = _iota((L, LANES), 1)
    row = _iota((L, LANES), 0)
    rowf = row.astype(F32)
    relf = jnp.maximum(row - lane, 0).astype(F32)
    first = (lane % HEAD_DIM) < (HEAD_DIM // 2)
    same_head = (row // HEAD_DIM) == (lane // HEAD_DIM)

    def rot(t, cos, sin):
        partner = jnp.where(first, -pltpu.roll(t, LANES - HEAD_DIM // 2, 1), pltpu.roll(t, HEAD_DIM // 2, 1))
        return t * cos + partner * sin

    decays = []
    for p in range(2):
        lg = lg_ref[p, 0:1, :]
        decays.append(dict(
            q=jnp.exp((rowf + 1.0) * lg), k=jnp.exp((L - 1.0 - rowf) * lg), c=jnp.exp(float(L) * lg),
            intra=[jnp.where(row >= lane, jnp.exp(relf * lg[:, hh * HEAD_DIM:hh * HEAD_DIM + 1]), 0.0)
                   for hh in range(2)]))

    tables = {}

    def pair_chain(b, p):
        pc = p * LANES
        dec = decays[p]
        if p == 0:
            ang = posq_ref[b].astype(F32) * inv_ref[...]
            tables[b] = (jnp.cos(ang), jnp.sin(ang))
        cos, sin = tables[b]
        q = rot(ret_ref[b, :, pc:pc + LANES], cos, sin)
        k = rot(ret_ref[b, :, W_BRANCH + pc:W_BRANCH + pc + LANES], cos, sin) * (HEAD_DIM ** -0.5)
        v_bf = ret_ref[b, :, 2 * W_BRANCH + pc:2 * W_BRANCH + pc + LANES].astype(BF16)
        k_bf = k.astype(BF16)
        state = st[b * 2 + p]
        s_qk = [_dot_nt(jnp.where((lane >= HEAD_DIM) if hh else (lane < HEAD_DIM), q, 0.0).astype(BF16), k_bf)
                for hh in range(2)]
        inter = _dot(q.astype(BF16), state.astype(BF16))
        kv = _dot_tn((k * dec["k"]).astype(BF16), v_bf)
        yield

        intra = [_dot((s_qk[hh] * dec["intra"][hh]).astype(BF16), v_bf) for hh in range(2)]
        st[b * 2 + p] = dec["c"] * state + jnp.where(same_head, kv, 0.0)
        yield

        o = jnp.where(lane < HEAD_DIM, intra[0], intra[1]) + inter * dec["q"]
        y = o * lax.rsqrt(_head_mean_sq(o) + EPS) * ng_ref[:, pc:pc + LANES]
        gate = ret_ref[b, :, 3 * W_BRANCH + pc:3 * W_BRANCH + pc + LANES]
        put(b, pc, y * _silu(gate))

    return [pair_chain(b, p) for b in range(batch) for p in range(2)]


SB_FUSED_BLOCKS = 3


def _sb_suffix_op():
    L = CHUNK
    r2 = _iota((2 * L, 2 * L), 0) % L
    c2 = _iota((2 * L, 2 * L), 1)
    return jnp.where((c2 >= L) | (r2 >= c2), 1.0, 0.0).astype(BF16)


def _sb_tile_chain(c, b, p, tiles, fresh, qh_s, carry_s, acc_s, suffix_op):
    L = CHUNK
    pc = p * LANES
    zs = [_dot_nt(qh_s[c], k_ref[b, :, pc:pc + LANES]) for k_ref, _, _ in tiles]
    yield
    rrs = []
    for z, (_, _, mask) in zip(zs, tiles):
        lk = -(jnp.maximum(z, 0.0) + jnp.log(1.0 + jnp.exp(-jnp.abs(z))))
        if mask is not None:
            lk = jnp.where(mask, lk, 0.0)
        hi, lo = _split2(lk)
        rrs.append(_dot(jnp.concatenate([hi, lo], axis=1), suffix_op))
    yield
    carry = None if fresh else carry_s[c]
    parts = []
    for z, rr, (_, v_ref, mask) in zip(zs, rrs, tiles):
        e = z + rr[:, :L]
        w = jnp.exp(e if carry is None else e + carry)
        if mask is not None:
            w = jnp.where(mask, w, 0.0)
        parts.append(_dot(w.astype(BF16), v_ref[b, :, pc:pc + LANES]))
        carry = rr[:, L:] if carry is None else carry + rr[:, L:]
    carry_s[c] = carry
    yield
    acc = None if fresh else acc_s[c]
    for part in parts:
        acc = part if acc is None else acc + part
    acc_s[c] = acc


def _sb_chains(i, q_ref, k_refs, v_refs, qh_s, carry_s, acc_s):
    L = CHUNK
    batch = q_ref.shape[0]
    lane = _iota((L, LANES), 1)
    row = _iota((L, LANES), 0)
    heads = [(b, p, hh) for b in range(batch) for p in range(2) for hh in range(2)]
    for c, (b, p, hh) in enumerate(heads):
        hm = (lane >= HEAD_DIM) if hh else (lane < HEAD_DIM)
        qh_s[c] = jnp.where(hm, q_ref[b, :, p * LANES:(p + 1) * LANES] * (HEAD_DIM ** -0.5), 0.0).astype(BF16)
    block_no = jnp.zeros((L, LANES), jnp.int32) + i
    masks = [lane < row] + [block_no >= d for d in range(1, SB_FUSED_BLOCKS)]
    tiles = [(k_refs[d], v_refs[d], masks[d]) for d in range(SB_FUSED_BLOCKS)]
    suffix_op = _sb_suffix_op()
    return [_sb_tile_chain(c, b, p, tiles, True, qh_s, carry_s, acc_s, suffix_op) for c, (b, p, hh) in enumerate(heads)]


def _sb_older_blocks(i, sb_hbm, kbuf, vbuf, sem, qh_s, carry_s, acc_s):
    L = CHUNK
    batch = kbuf.shape[0]
    heads = [(b, p, hh) for b in range(batch) for p in range(2) for hh in range(2)]

    def all_underflow():
        return (jnp.max(jnp.max(carry_s[...], axis=0)) <= F32_EXP_UNDERFLOW).astype(jnp.int32)

    def fetch(j, col, buf, slot):
        return pltpu.make_async_copy(
            sb_hbm.at[:, pl.ds(pl.multiple_of(j * L, L), L), pl.ds(col * W_BRANCH, W_BRANCH)], buf, sem.at[slot])

    def cond(st):
        return jnp.logical_and(st[0] >= 0, st[1] == 0)

    def body(st):
        copies = [fetch(st[0], 1, kbuf, 0), fetch(st[0], 2, vbuf, 1)]
        for cp in copies:
            cp.start()
        for cp in copies:
            cp.wait()
        suffix_op = _sb_suffix_op()
        _interleave([_sb_tile_chain(c, b, p, [(kbuf, vbuf, None)], False, qh_s, carry_s, acc_s, suffix_op)
                     for c, (b, p, hh) in enumerate(heads)])
        return (st[0] - 1, all_underflow())

    lax.while_loop(cond, body, (i - SB_FUSED_BLOCKS, all_underflow()))


def _swa_chains(i, sink_ref, q_ref, kc_ref, kp_ref, vc_ref, vp_ref, tab_ref, posq_ref, posc_ref, posp_ref,
                qg_ref, kg_ref, put):
    L = CHUNK
    batch = q_ref.shape[0]
    lane = _iota((L, LANES), 1)
    t = _iota((L, 2 * L), 0)
    j = _iota((L, 2 * L), 1)
    dist = t + L - j
    block_no = jnp.zeros((L, 2 * L), jnp.int32) + i
    valid = (dist >= 0) & (dist < L) & ((j >= L) | (block_no > 0))

    def qk_norm(x, g_ref):
        return x * lax.rsqrt(_head_mean_sq(x) + EPS) * g_ref[...]

    bias = {}

    def group_chain(b, g):
        sl = slice(g * LANES, (g + 1) * LANES)
        if g == 0:
            bias[b] = _t5_bias(tab_ref, posq_ref[b], posp_ref[b, 0], posc_ref[b, 0])
        qn = qk_norm(q_ref[b, :, sl], qg_ref) * (HEAD_DIM ** -0.5)
        kcat = jnp.concatenate([qk_norm(kp_ref[b, :, sl], kg_ref), qk_norm(kc_ref[b, :, sl], kg_ref)],
                               axis=0).astype(BF16)
        qk = [_dot_nt(jnp.where((lane >= HEAD_DIM) if r else (lane < HEAD_DIM), qn, 0.0).astype(BF16), kcat)
              for r in range(2)]
        yield
        vcat = jnp.concatenate([vp_ref[b, :, sl], vc_ref[b, :, sl]], axis=0).astype(BF16)
        outs = []
        for r in range(2):
            hq = 2 * g + r
            logits = jnp.where(valid, qk[r] + bias[b][hq], NEG)
            sink = sink_ref[hq]
            m = jnp.maximum(jnp.max(logits, axis=-1, keepdims=True), sink)
            pr = jnp.exp(logits - m)
            den = jnp.sum(pr, axis=-1, keepdims=True) + jnp.exp(sink - m)
            outs.append(_dot((pr / den).astype(BF16), vcat))
        yield
        put(b, g * LANES, jnp.where(lane < HEAD_DIM, outs[0], outs[1]))

    return [group_chain(b, g) for b in range(batch) for g in range(2)]


DENSE_COLS = 512


def _dense_chain(x_ref, g1_ref, mod1, g2_ref, mod2, branch, wg_ref, wu_ref, wo_ref, w1_ref, w2_ref, out_ref,
                 h_s, merged_s, x1_s, ff_s):
    L = CHUNK
    batch = x_ref.shape[0]
    rows = lambda b: slice(b * L, (b + 1) * L)
    sh1, sc1, gt1 = mod1
    sh2, sc2, gt2 = mod2
    for b in range(batch):
        h_s[rows(b), :] = _norm_mod(x_ref[b], g1_ref[...], sc1[b], sh1[b]).astype(BF16)
    yield
    n_col = D_MODEL // DENSE_COLS
    units = [(j, n) for j in range(n_col) for n in range(N_HEADS)]
    logits = lambda j, n: _dot(h_s[...], wg_ref[:, n * D_MODEL + j * DENSE_COLS:n * D_MODEL + (j + 1) * DENSE_COLS])
    pending = logits(*units[0])
    acc = None
    for u, (j, n) in enumerate(units):
        cs = slice(j * DENSE_COLS, (j + 1) * DENSE_COLS)
        gate_logits = pending
        if u + 1 < len(units):
            pending = logits(*units[u + 1])
        term = _sigmoid(gate_logits) * _dot(branch(n), wu_ref[n, :, cs])
        acc = term if n == 0 else acc + term
        if n == N_HEADS - 1:
            merged_s[:, cs] = acc.astype(BF16)
        yield
    proj = lambda j: _dot(merged_s[...], wo_ref[:, j * DENSE_COLS:(j + 1) * DENSE_COLS])
    pending = proj(0)
    for j in range(n_col):
        cs = slice(j * DENSE_COLS, (j + 1) * DENSE_COLS)
        mix = pending
        if j + 1 < n_col:
            pending = proj(j + 1)
        for b in range(batch):
            x1_s[rows(b), cs] = x_ref[b, :, cs] + gt1[b][:, cs] * mix[rows(b)]
        yield
    for b in range(batch):
        h_s[rows(b), :] = _norm_mod(x1_s[rows(b), :], g2_ref[...], sc2[b], sh2[b]).astype(BF16)
    yield
    n_ff = D_FF // DENSE_COLS
    up = lambda n: _dot(h_s[...], w1_ref[:, n * DENSE_COLS:(n + 1) * DENSE_COLS])
    pending = up(0)
    for n in range(n_ff):
        a = jnp.maximum(pending, 0.0)
        if n + 1 < n_ff:
            pending = up(n + 1)
        part = _dot((a * a).astype(BF16), w2_ref[n * DENSE_COLS:(n + 1) * DENSE_COLS, :])
        ff_s[...] = part if n == 0 else ff_s[...] + part
        yield
    for b in range(batch):
        out_ref[b] = x1_s[rows(b), :] + gt2[b] * ff_s[rows(b), :]


def _layer_kernel(sink_ref, ml_ref, ifc_ref, ift_ref, cw_ref, cb_ref,
                  ret_ref, inv_ref, lg_ref, ng_ref,
                  sbq_ref, sbk0_ref, sbk1_ref, sbk2_ref, sbv0_ref, sbv1_ref, sbv2_ref, sb_hbm,
                  swq_ref, swkc_ref, swkp_ref, swvc_ref, swvp_ref, tab_ref, posq_ref, posc_ref, posp_ref, qg_ref, kg_ref,
                  x_ref, g1_ref, sh1_ref, sc1_ref, gt1_ref, g2_ref, sh2_ref, sc2_ref, gt2_ref,
                  wg_ref, wu_ref, wo_ref, w1_ref, w2_ref,
                  out_ref,
                  halo, c_st, m_st, ret_st, qh_s, carry_s, acc_s, kbuf, vbuf, sem,
                  branch_s, h_s, merged_s, x1_s, ff_s, *, n_chunks):
    L = CHUNK
    batch = ml_ref.shape[0]
    step = pl.program_id(0)
    i = jnp.minimum(step, n_chunks - 1)
    slot = step % 2

    @pl.when(step == 0)
    def _init():
        halo[:, 0:8, :] = jnp.zeros((halo.shape[0], 8, LANES), F32)
        c_st[...] = jnp.zeros(c_st.shape, F32)
        m_st[...] = jnp.zeros(m_st.shape, F32)
        ret_st[...] = jnp.zeros(ret_st.shape, F32)
        branch_s[1] = jnp.zeros(branch_s.shape[1:], BF16)

    def put(n):
        def write(b, col, value):
            branch_s[slot, n, b * L:(b + 1) * L, col:col + LANES] = value.astype(BF16)
        return write

    chains = (_mlstm_chains(ml_ref, ifc_ref, ift_ref, cw_ref, cb_ref, put(0), halo, c_st, m_st)
              + _retention_chains(ret_ref, posq_ref, inv_ref, lg_ref, ng_ref, put(1), ret_st)
              + _sb_chains(i, sbq_ref, (sbk0_ref, sbk1_ref, sbk2_ref), (sbv0_ref, sbv1_ref, sbv2_ref),
                           qh_s, carry_s, acc_s)
              + _swa_chains(i, sink_ref, swq_ref, swkc_ref, swkp_ref, swvc_ref, swvp_ref, tab_ref, posq_ref, posc_ref,
                            posp_ref, qg_ref, kg_ref, put(3)))
    dense = _dense_chain(x_ref, g1_ref, (sh1_ref, sc1_ref, gt1_ref), g2_ref, (sh2_ref, sc2_ref, gt2_ref),
                         lambda n: branch_s[1 - slot, n], wg_ref, wu_ref, wo_ref, w1_ref, w2_ref, out_ref,
                         h_s, merged_s, x1_s, ff_s)
    _interleave(chains, filler=dense)

    _sb_older_blocks(i, sb_hbm, kbuf, vbuf, sem, qh_s, carry_s, acc_s)
    lane = _iota((L, LANES), 1)
    for b in range(batch):
        for p in range(2):
            c0 = (b * 2 + p) * 2
            put(2)(b, p * LANES, jnp.where(lane < HEAD_DIM, acc_s[c0], acc_s[c0 + 1]))


def _layer(xt, mods, g1, g2, ml, ifc, ift, ret, sb, swa, pos, small, weights, layer, batch, seq):
    d = xt.shape[1]
    nc = seq // CHUNK
    n_chain = batch * N_HEADS
    conv_w, conv_b, lg_tab, ret_g, sinks, q_g, k_g = small
    bias_tab, inv_row, pos_col, pos_row = pos
    w_gate, w_up, w_out, w_ff1, w_ff2 = weights
    mix = lambda c: jnp.minimum(c, nc - 1)
    lag = lambda c: jnp.maximum(c - 1, 0)
    chunk = lambda n, blk=0, back=0: pl.BlockSpec(
        (batch, CHUNK, n), lambda c: (0, jnp.maximum(mix(c) - back, 0), blk))
    sb3 = sb.reshape(batch, seq, -1)
    swa3 = swa.reshape(batch, seq, -1)
    x3 = xt.reshape(batch, seq, d)
    rowblk = pl.BlockSpec((batch, CHUNK, d), lambda c: (0, lag(c), 0))
    in_specs = (
        [pl.BlockSpec(memory_space=pltpu.SMEM),
         chunk(4 * W_BRANCH), chunk(2 * LANES), pl.BlockSpec((batch, 16, CHUNK), lambda c: (0, 0, mix(c))),
         _resident(conv_w), _resident(conv_b),
         chunk(4 * W_BRANCH), _resident(inv_row), _resident(lg_tab), _resident(ret_g),
         chunk(W_BRANCH, 0)]
        + [chunk(W_BRANCH, 1, back) for back in range(SB_FUSED_BLOCKS)]
        + [chunk(W_BRANCH, 2, back) for back in range(SB_FUSED_BLOCKS)]
        + [pl.BlockSpec(memory_space=pl.ANY),
           chunk(W_BRANCH, 0), chunk(W_BRANCH, 1), chunk(W_BRANCH, 1, 1), chunk(W_BRANCH, 2), chunk(W_BRANCH, 2, 1),
           _resident(bias_tab), chunk(LANES),
           pl.BlockSpec((batch, 1, 1, CHUNK), lambda c: (0, mix(c), 0, 0)),
           pl.BlockSpec((batch, 1, 1, CHUNK), lambda c: (0, jnp.maximum(mix(c) - 1, 0), 0, 0)),
           _resident(q_g), _resident(k_g),
           rowblk, _resident(g1)] + [_resident(m) for m in mods[0]] + [_resident(g2)] + [_resident(m) for m in mods[1]]
        + [_layer_of(w, layer) for w in (w_gate, w_up, w_out, w_ff1, w_ff2)])
    rows = batch * CHUNK
    out = pl.pallas_call(
        functools.partial(_layer_kernel, n_chunks=nc),
        grid=(nc + 1,),
        in_specs=in_specs,
        out_specs=rowblk,
        out_shape=jax.ShapeDtypeStruct((batch, seq, d), F32),
        scratch_shapes=[pltpu.VMEM((n_chain, CHUNK + 8, LANES), F32), pltpu.VMEM((n_chain, LANES, LANES), F32),
                        pltpu.VMEM((n_chain, 8, LANES), F32), pltpu.VMEM((batch * 2, LANES, LANES), F32),
                        pltpu.VMEM((n_chain, CHUNK, LANES), BF16), pltpu.VMEM((n_chain, CHUNK, LANES), F32),
                        pltpu.VMEM((n_chain, CHUNK, LANES), F32),
                        pltpu.VMEM((batch, CHUNK, W_BRANCH), BF16), pltpu.VMEM((batch, CHUNK, W_BRANCH), BF16),
                        pltpu.SemaphoreType.DMA((2,)),
                        pltpu.VMEM((2, N_HEADS, rows, W_BRANCH), BF16), pltpu.VMEM((rows, d), BF16),
                        pltpu.VMEM((rows, d), BF16), pltpu.VMEM((rows, d), F32), pltpu.VMEM((rows, d), F32)],
        compiler_params=pltpu.CompilerParams(dimension_semantics=("arbitrary",), vmem_limit_bytes=LAYER_VMEM_LIMIT),
        name="layer",
    )(sinks, ml.reshape(batch, seq, -1), ifc.reshape(batch, seq, -1), ift, conv_w, conv_b,
      ret.reshape(batch, seq, -1), inv_row, lg_tab, ret_g,
      sb3, sb3, sb3, sb3, sb3, sb3, sb3, sb3,
      swa3, swa3, swa3, swa3, swa3, bias_tab, pos_col.reshape(batch, seq, LANES), pos_row, pos_row, q_g, k_g,
      x3, g1, *mods[0], g2, *mods[1], w_gate, w_up, w_out, w_ff1, w_ff2)
    return out.reshape(batch * seq, d)


GATE_COL = 3 * W_BRANCH


def _w_in_kernel(wt_ref, perm_ref, gate_ref, ift_ref):
    one_hot = lambda hit: jnp.where(hit, 1.0, 0.0).astype(BF16)
    eye = one_hot(_iota((LANES, LANES), 0) == _iota((LANES, LANES), 1))
    transpose = lambda rows_bf: _dot_nt(eye, rows_bf).astype(BF16)
    cols = lambda lo, n: transpose(wt_ref[lo:lo + n, :].astype(BF16))
    o_out = GATE_COL + 2 * N_HEADS
    o_ret = o_out + W_BRANCH
    o_sb = o_ret + 4 * W_BRANCH
    o_swa = o_sb + 3 * W_BRANCH
    o_gate = o_swa + 2 * W_BRANCH

    perm_ref[:, 0:GATE_COL] = cols(0, GATE_COL)
    perm_ref[:, GATE_COL:SEG_ML[1]] = cols(o_out, W_BRANCH)
    perm_ref[:, SEG_RET[0]:SEG_RET[1]] = cols(o_ret, 4 * W_BRANCH)
    perm_ref[:, SEG_SB[0]:SEG_SB[1]] = cols(o_sb, 3 * W_BRANCH)
    perm_ref[:, SEG_SWA[0]:SEG_SWA[0] + W_BRANCH] = cols(o_swa, W_BRANCH)
    for n in range(2):
        for h in range(2):
            lo = o_swa + W_BRANCH + n * 2 * HEAD_DIM + h * HEAD_DIM
            head = wt_ref[lo:lo + HEAD_DIM, :].astype(BF16)
            c0 = SEG_SWA[0] + W_BRANCH + (2 * n + h) * LANES
            perm_ref[:, c0:c0 + LANES] = transpose(jnp.concatenate([head, head], axis=0))
    gate_ref[...] = cols(o_gate, gate_ref.shape[1])

    block = wt_ref[GATE_COL:GATE_COL + LANES, :].astype(BF16)
    def placement(n_dst, period):
        dst = _iota((n_dst, LANES), 0)
        j = dst % period
        src = jnp.where(j < 2, j, j + 2) + 2 * (dst // period)
        return one_hot((_iota((n_dst, LANES), 1) == src) & (j < 4))
    ift_ref[...] = _dot(placement(16, 8), block).astype(BF16)
    perm_ref[:, SEG_IFC[0]:SEG_IFC[1]] = transpose(_dot(placement(2 * LANES, LANES), block).astype(BF16))


def _prep_w_in(w_in):
    depth, d, n_cols = w_in.shape
    tk = LANES
    return pl.pallas_call(
        _w_in_kernel,
        grid=(depth, d // tk),
        in_specs=[pl.BlockSpec((None, n_cols, tk), lambda l, i: (l, 0, i))],
        out_specs=[pl.BlockSpec((None, tk, N_IN), lambda l, i: (l, i, 0)),
                   pl.BlockSpec((None, tk, N_HEADS * d), lambda l, i: (l, i, 0)),
                   pl.BlockSpec((None, 16, tk), lambda l, i: (l, 0, i))],
        out_shape=[jax.ShapeDtypeStruct((depth, d, N_IN), BF16), jax.ShapeDtypeStruct((depth, d, N_HEADS * d), BF16),
                   jax.ShapeDtypeStruct((depth, 16, d), BF16)],
        compiler_params=_params(("parallel", "parallel")),
        name="prep_w_in",
    )(jnp.swapaxes(w_in, 1, 2))


CAST_TILE_BYTES = 2 * 1024 * 1024


def _cast_kernel(x_ref, o_ref):
    o_ref[...] = x_ref[...].astype(BF16)


def _to_bf16(a):
    cols = a.shape[-1]
    rows = a.size // cols
    tm = CAST_TILE_BYTES // (4 * cols)
    spec = pl.BlockSpec((tm, cols), lambda i: (i, 0))
    out = pl.pallas_call(
        _cast_kernel,
        grid=(rows // tm,),
        in_specs=[spec],
        out_specs=spec,
        out_shape=jax.ShapeDtypeStruct((rows, cols), BF16),
        compiler_params=_params(("parallel",)),
        name="cast_bf16",
    )(a.reshape(rows, cols))
    return out.reshape(a.shape)


def _gate_bias_layout(gate_b):
    ib, fb = gate_b[0], gate_b[1]
    z = jnp.zeros((LANES - 4,), F32)
    row = jnp.concatenate([ib[0:2], fb[0:2], z, ib[2:4], fb[2:4], z]).reshape(1, 2 * LANES)
    z4 = jnp.zeros((4,), F32)
    col = jnp.concatenate([ib[0:2], fb[0:2], z4, ib[2:4], fb[2:4], z4])
    return row, jnp.broadcast_to(col[:, None], (16, LANES))


def kernel(x, c, positions, w_ada, b_ada, norm_g, w_in, mlstm_conv_w, mlstm_conv_b, mlstm_gate_b,
           ret_norm_g, swa_q_norm_g, swa_k_norm_g, swa_sinks, rel_bias, w_up, w_out, w_ff1, w_ff2):
    batch, seq, d = x.shape
    depth = w_in.shape[0]
    t = batch * seq
    nb = seq // CHUNK

    c8 = jnp.concatenate([c, jnp.zeros((8 - batch, d), F32)], axis=0)
    mod = _ada(c8, w_ada.reshape(depth * 2, d, 3 * d), b_ada.reshape(depth * 2, 1, 3 * d))
    mod = mod[:, :batch].reshape(depth, 2, batch, 3, 1, d)

    pos_col = jnp.broadcast_to(positions.reshape(t, 1), (t, LANES))
    pos_row = positions.reshape(batch, nb, 1, CHUNK)
    half = HEAD_DIM // 2
    inv = ROPE_BASE ** (-(np.arange(LANES) % half).astype(np.float64) / half)
    inv_row = jnp.asarray(inv, F32).reshape(1, LANES)
    bias_tab = jnp.concatenate([rel_bias.T, jnp.zeros((N_HEADS, LANES - N_BUCKETS), F32)], axis=1)

    log_gamma = np.log(1.0 - np.exp2(-(RET_DECAY_BASE + np.arange(N_HEADS, dtype=np.float64))))
    lg_tab = jnp.asarray(np.broadcast_to(np.repeat(log_gamma, HEAD_DIM).reshape(2, 1, LANES), (2, 8, LANES)), F32)

    w_perm, w_gate, w_ift = _prep_w_in(w_in)
    w_up_bf, w_out_bf, w_ff1_bf, w_ff2_bf = (_to_bf16(w) for w in (w_up, w_out, w_ff1, w_ff2))

    xt = x.reshape(t, d)
    for l in range(depth):
        gb_row, gbt = _gate_bias_layout(mlstm_gate_b[l])
        g1 = norm_g[l, 0].reshape(1, d)
        g2 = norm_g[l, 1].reshape(1, d)
        shift1, scale1, gate1 = mod[l, 0, :, 0], mod[l, 0, :, 1], mod[l, 0, :, 2]
        shift2, scale2, gate2 = mod[l, 1, :, 0], mod[l, 1, :, 1], mod[l, 1, :, 2]

        ml, ifc, ret, sb, swa, ift = _in_proj(xt, g1, scale1, shift1, w_perm, w_ift, gb_row, gbt, l, batch, seq)
        small = (mlstm_conv_w[l], mlstm_conv_b[l].reshape(1, 2 * W_BRANCH), lg_tab, ret_norm_g[l].reshape(1, W_BRANCH),
                 swa_sinks[l], jnp.tile(swa_q_norm_g[l], 2).reshape(1, LANES), jnp.tile(swa_k_norm_g[l], 2).reshape(1, LANES))
        xt = _layer(xt, ((shift1, scale1, gate1), (shift2, scale2, gate2)), g1, g2, ml, ifc, ift, ret, sb, swa,
                    (bias_tab, inv_row, pos_col, pos_row), small, (w_gate, w_up_bf, w_out_bf, w_ff1_bf, w_ff2_bf), l, batch, seq)
    return xt.reshape(batch, seq, d)
```

```python
import functools
import math

import numpy as np
import jax
import jax.numpy as jnp
from jax import lax
from jax.experimental import pallas as pl
from jax.experimental.pallas import tpu as pltpu

F32 = jnp.float32
BF16 = jnp.bfloat16

D_MODEL = 1024
HEAD_DIM = 64
N_HEADS = 4
W_BRANCH = N_HEADS * HEAD_DIM
LANES = 128
CHUNK = 128
CONV_K = 4
D_FF = 4 * D_MODEL
N_BUCKETS = 32
MAX_DIST = 128
ROPE_BASE = 10000.0
RET_DECAY_BASE = 5.0
EPS = 1e-6
NEG = -1e30
F32_EXP_UNDERFLOW = -104.0
VMEM_LIMIT = 56 * 1024 * 1024
LAYER_VMEM_LIMIT = 60 * 1024 * 1024

SEG_ML = (0, 1024)
SEG_IFC = (1024, 1152)
SEG_RET = (1152, 2176)
SEG_SB = (2176, 2944)
SEG_SWA = (2944, 3456)
N_IN = 3456


def _dot(a, b):
    return jnp.dot(a, b, preferred_element_type=F32)


def _dot_nt(a, b):
    return lax.dot_general(a, b, (((1,), (1,)), ((), ())), preferred_element_type=F32)


def _dot_tn(a, b):
    return lax.dot_general(a, b, (((0,), (0,)), ((), ())), preferred_element_type=F32)


def _split2(x):
    hi = x.astype(BF16)
    lo = (x - hi.astype(F32)).astype(BF16)
    return hi, lo


def _split3(x):
    hi = x.astype(BF16)
    r = x - hi.astype(F32)
    mid = r.astype(BF16)
    lo = (r - mid.astype(F32)).astype(BF16)
    return hi, mid, lo


def _iota(shape, axis):
    return lax.broadcasted_iota(jnp.int32, shape, axis)


def _log_sigmoid(x):
    return jnp.minimum(x, 0.0) - jnp.log(1.0 + jnp.exp(-jnp.abs(x)))


def _sigmoid(x):
    return 1.0 / (1.0 + jnp.exp(-x))


def _silu(x):
    return x * _sigmoid(x)


def _norm_mod(x, g, scale, shift):
    ms = jnp.mean(x * x, axis=-1, keepdims=True)
    y = x * lax.rsqrt(ms + EPS)
    return (y * g) * (1.0 + scale) + shift


def _head_mean_sq(x):
    lane = _iota(x.shape, 1)
    sq = x * x
    s0 = jnp.sum(jnp.where(lane < HEAD_DIM, sq, 0.0), axis=-1, keepdims=True)
    s1 = jnp.sum(jnp.where(lane >= HEAD_DIM, sq, 0.0), axis=-1, keepdims=True)
    return jnp.where(lane < HEAD_DIM, s0, s1) * (1.0 / HEAD_DIM)


def _params(sem):
    return pltpu.CompilerParams(dimension_semantics=sem, vmem_limit_bytes=VMEM_LIMIT)


def _interleave(chains, filler=None, every=1):
    def advance(gen):
        try:
            next(gen)
            return True
        except StopIteration:
            return False

    live = list(chains)
    filling = filler is not None
    steps = 0
    while live:
        still = []
        for ch in live:
            if advance(ch):
                still.append(ch)
            steps += 1
            if filling and steps % every == 0:
                filling = advance(filler)
        live = still
    while filling:
        filling = advance(filler)


def _resident(a):
    return pl.BlockSpec(a.shape, lambda *_: (0,) * a.ndim, pipeline_mode=pl.Buffered(1))


def _layer_of(a, layer):
    return pl.BlockSpec((None,) + a.shape[1:], lambda *_: (layer,) + (0,) * (a.ndim - 1),
                        pipeline_mode=pl.Buffered(1))


def _ada_kernel(c_ref, w_ref, b_ref, o_ref):
    c = c_ref[...]
    ch, cl = _split2(_silu(c))
    wh, wl = _split2(w_ref[0])
    o_ref[0] = _dot(ch, wh) + _dot(ch, wl) + _dot(cl, wh) + b_ref[0]


def _ada(c8, w_ada, b_ada):
    n_mod, d, n3 = w_ada.shape
    tn = 1024
    return pl.pallas_call(
        _ada_kernel,
        grid=(n_mod, n3 // tn),
        in_specs=[pl.BlockSpec((8, d), lambda m, n: (0, 0)),
                  pl.BlockSpec((1, d, tn), lambda m, n: (m, 0, n)),
                  pl.BlockSpec((1, 1, tn), lambda m, n: (m, 0, n))],
        out_specs=pl.BlockSpec((1, 8, tn), lambda m, n: (m, 0, n)),
        out_shape=jax.ShapeDtypeStruct((n_mod, 8, n3), F32),
        compiler_params=_params(("parallel", "parallel")),
        name="ada_mod",
    )(c8, w_ada, b_ada)


def _t5_bias(tab_ref, pos_q, pos_prev, pos_cur):
    max_exact = N_BUCKETS // 2
    rel = jnp.concatenate([pos_q - pos_prev, pos_q - pos_cur], axis=1)
    n = jnp.maximum(rel, 0)
    nf = jnp.maximum(n, 1).astype(F32)
    large = max_exact + (jnp.log(nf / max_exact) / math.log(MAX_DIST / max_exact)
                         * (N_BUCKETS - max_exact)).astype(jnp.int32)
    large = jnp.minimum(large, N_BUCKETS - 1)
    bucket = jnp.where(n < max_exact, n, large)
    halves = (bucket[:, :CHUNK], bucket[:, CHUNK:])
    tiles = []
    for h in range(N_HEADS):
        row = jnp.broadcast_to(tab_ref[h:h + 1, :], (CHUNK, LANES))
        tiles.append(jnp.concatenate([jnp.take_along_axis(row, idx, axis=1) for idx in halves], axis=1))
    return tiles


def _in_kernel(x_ref, g_ref, sc_ref, sh_ref, w_ref, wt_ref, gb_ref, gbt_ref,
               ml_ref, ifc_ref, ret_ref, sb_ref, swa_ref, ift_ref):
    h = _norm_mod(x_ref[...], g_ref[...], sc_ref[0], sh_ref[0]).astype(BF16)

    def seg(s):
        return _dot(h, w_ref[:, s[0]:s[1]])

    ml_ref[...] = seg(SEG_ML)
    ifc_ref[...] = seg(SEG_IFC) + gb_ref[...]
    ret_ref[...] = seg(SEG_RET)
    sb_ref[...] = seg(SEG_SB).astype(BF16)
    swa_ref[...] = seg(SEG_SWA)
    ift_ref[0] = _dot_nt(wt_ref[...], h) + gbt_ref[:, 0:1]


def _in_proj(x, g, scale, shift, w_perm, w_ift, gb_row, gbt, layer, batch, seq):
    t, d = x.shape
    tm = 1024
    tpb = seq // tm
    row = lambda n: pl.BlockSpec((tm, n), lambda i: (i, 0))
    mod = pl.BlockSpec((1, 1, d), lambda i: (i // tpb, 0, 0))
    widths = [s[1] - s[0] for s in (SEG_ML, SEG_IFC, SEG_RET, SEG_SB, SEG_SWA)]
    dts = [F32, F32, F32, BF16, F32]
    return pl.pallas_call(
        _in_kernel,
        grid=(t // tm,),
        in_specs=[row(d), _resident(g), mod, mod, _layer_of(w_perm, layer), _layer_of(w_ift, layer), _resident(gb_row),
                  _resident(gbt)],
        out_specs=[row(n) for n in widths] + [pl.BlockSpec((1, 16, tm), lambda i: (i // tpb, 0, i % tpb))],
        out_shape=[jax.ShapeDtypeStruct((t, n), dt) for n, dt in zip(widths, dts)]
        + [jax.ShapeDtypeStruct((batch, 16, seq), F32)],
        compiler_params=_params(("parallel",)),
        name="in_proj",
    )(x, g, scale, shift, w_perm, w_ift, gb_row, gbt)


def _mlstm_chains(ml_ref, ifc_ref, ift_ref, cw_ref, cb_ref, put, halo, c_st, m_st):
    L = CHUNK
    HALO = 8
    batch = ml_ref.shape[0]
    lane = _iota((L, LANES), 1)
    row = _iota((L, LANES), 0)
    causal = row >= lane
    tri = jnp.where(causal, 1.0, 0.0).astype(BF16)
    upper = jnp.where(row <= lane, 1.0, 0.0).astype(BF16)

    def conv_silu(b, col, slot):
        hl = halo.at[slot]
        hl[HALO:HALO + L, :] = ml_ref[b, :, col:col + LANES]
        acc = jnp.zeros((L, LANES), F32) + cb_ref[:, col:col + LANES]
        for j in range(CONV_K):
            off = HALO - (CONV_K - 1) + j
            acc = acc + hl[off:off + L, :] * cw_ref[j:j + 1, col:col + LANES]
        hl[0:HALO, :] = hl[L:L + HALO, :]
        return _silu(acc)

    def pair_chain(b, p):
        pc = p * LANES
        ift = ift_ref[b, 0:8, :]
        lr_h, lr_m, lr_l = _split3(_log_sigmoid(ift))
        a_rows = _dot(lr_h, upper) + _dot(lr_m, upper) + _dot(lr_l, upper)
        ifc = ifc_ref[b]
        gates_b = jnp.concatenate([jnp.broadcast_to(ifc[:, 4 * p + j:4 * p + j + 1], (L, LANES)) for j in range(4)],
                                  axis=1)
        q = conv_silu(b, pc, (b * 2 + p) * 2)
        k = conv_silu(b, W_BRANCH + pc, (b * 2 + p) * 2 + 1)
        v = ml_ref[b, :, 2 * W_BRANCH + pc:2 * W_BRANCH + pc + LANES]
        k_bf = k.astype(BF16)
        yield

        lf_h, lf_m, lf_l = _split3(_log_sigmoid(gates_b[:, 2 * LANES:]))
        a_b = _dot(tri, lf_h) + _dot(tri, lf_m) + _dot(tri, lf_l)
        heads = []
        for hh in range(2):
            ch = (b * 2 + p) * 2 + hh
            hm = (lane >= HEAD_DIM) if hh else (lane < HEAD_DIM)
            qh = (jnp.where(hm, q, 0.0) * (HEAD_DIM ** -0.5)).astype(BF16)
            c_in = c_st[ch]
            heads.append(dict(ch=ch, hm=hm, c_in=c_in, s_qk=_dot_nt(qh, k_bf), q_c=_dot(qh, c_in.astype(BF16))))
        yield

        for hh, hd in enumerate(heads):
            i_col = gates_b[:, hh * LANES:(hh + 1) * LANES]
            a_col = a_b[:, hh * LANES:(hh + 1) * LANES]
            a_last = a_col[L - 1:L, :]
            m_in = m_st[hd["ch"], 0:1, :]
            v_aug = jnp.where(hd["hm"], v, 1.0).astype(BF16)

            r0 = 4 * p + hh
            d_intra = jnp.where(causal, a_col - (a_rows[r0 + 2:r0 + 3, :] - ift[r0:r0 + 1, :]), NEG)
            m_intra = jnp.max(d_intra, axis=-1, keepdims=True)
            m_inter = a_col + m_in
            m_tot = jnp.maximum(m_inter, m_intra)
            pm = jnp.exp(d_intra - m_tot) * hd["s_qk"]
            hd["p_v"] = _dot(pm.astype(BF16), v_aug)
            hd["s_inter"] = jnp.exp(m_inter - m_tot)
            hd["floor"] = jnp.exp(-m_tot)

            g_col = a_last - a_col + i_col
            m_loc = jnp.max(g_col, axis=0, keepdims=True)
            kw = (jnp.where(hd["hm"], k, 0.0) * jnp.exp(g_col - m_loc)).astype(BF16)
            hd["c_loc"] = _dot_tn(kw, v_aug)
            m_new = jnp.maximum(a_last + m_in, m_loc)
            hd["w_old"] = jnp.exp(a_last + m_in - m_new)
            hd["w_new"] = jnp.exp(m_loc - m_new)
            m_st[hd["ch"]] = jnp.zeros((8, LANES), F32) + m_new
        yield

        for hd in heads:
            hd["o_aug"] = hd["s_inter"] * hd["q_c"] + hd["p_v"]
        den = pltpu.roll(jnp.where(lane < HEAD_DIM, heads[1]["o_aug"], heads[0]["o_aug"]), HEAD_DIM, 1)
        num = jnp.where(lane < HEAD_DIM, heads[0]["o_aug"], heads[1]["o_aug"])
        floor = jnp.where(lane < HEAD_DIM, heads[0]["floor"], heads[1]["floor"])
        h_ml = num / jnp.maximum(jnp.abs(den), floor)
        o_gate = ml_ref[b, :, 3 * W_BRANCH + pc:3 * W_BRANCH + pc + LANES]
        put(b, pc, h_ml * _sigmoid(o_gate))
        yield

        for hd in heads:
            c_st[hd["ch"]] = hd["w_old"] * hd["c_in"] + hd["w_new"] * hd["c_loc"]

    return [pair_chain(b, p) for b in range(batch) for p in range(2)]


def _retention_chains(ret_ref, posq_ref, inv_ref, lg_ref, ng_ref, put, st):
    L = CHUNK
    batch = ret_ref.shape[0]
    lane = _iota((L, LANES), 1)
    row = _iota((L, LANES), 0)
    rowf = row.astype(F32)
    relf = jnp.maximum(row - lane, 0).astype(F32)
    first = (lane % HEAD_DIM) < (HEAD_DIM // 2)
    same_head = (row // HEAD_DIM) == (lane // HEAD_DIM)

    def rot(t, cos, sin):
        partner = jnp.where(first, -pltpu.roll(t, LANES - HEAD_DIM // 2, 1), pltpu.roll(t, HEAD_DIM // 2, 1))
        return t * cos + partner * sin

    decays = []
    for p in range(2):
        lg = lg_ref[p, 0:1, :]
        decays.append(dict(
            q=jnp.exp((rowf + 1.0) * lg), k=jnp.exp((L - 1.0 - rowf) * lg), c=jnp.exp(float(L) * lg),
            intra=[jnp.where(row >= lane, jnp.exp(relf * lg[:, hh * HEAD_DIM:hh * HEAD_DIM + 1]), 0.0)
                   for hh in range(2)]))

    tables = {}

    def pair_chain(b, p):
        pc = p * LANES
        dec = decays[p]
        if p == 0:
            ang = posq_ref[b].astype(F32) * inv_ref[...]
            tables[b] = (jnp.cos(ang), jnp.sin(ang))
        cos, sin = tables[b]
        q = rot(ret_ref[b, :, pc:pc + LANES], cos, sin)
        k = rot(ret_ref[b, :, W_BRANCH + pc:W_BRANCH + pc + LANES], cos, sin) * (HEAD_DIM ** -0.5)
        v_bf = ret_ref[b, :, 2 * W_BRANCH + pc:2 * W_BRANCH + pc + LANES].astype(BF16)
        k_bf = k.astype(BF16)
        state = st[b * 2 + p]
        s_qk = [_dot_nt(jnp.where((lane >= HEAD_DIM) if hh else (lane < HEAD_DIM), q, 0.0).astype(BF16), k_bf)
                for hh in range(2)]
        inter = _dot(q.astype(BF16), state.astype(BF16))
        kv = _dot_tn((k * dec["k"]).astype(BF16), v_bf)
        yield

        intra = [_dot((s_qk[hh] * dec["intra"][hh]).astype(BF16), v_bf) for hh in range(2)]
        st[b * 2 + p] = dec["c"] * state + jnp.where(same_head, kv, 0.0)
        yield

        o = jnp.where(lane < HEAD_DIM, intra[0], intra[1]) + inter * dec["q"]
        y = o * lax.rsqrt(_head_mean_sq(o) + EPS) * ng_ref[:, pc:pc + LANES]
        gate = ret_ref[b, :, 3 * W_BRANCH + pc:3 * W_BRANCH + pc + LANES]
        put(b, pc, y * _silu(gate))

    return [pair_chain(b, p) for b in range(batch) for p in range(2)]


SB_FUSED_BLOCKS = 3


def _sb_suffix_op():
    L = CHUNK
    r2 = _iota((2 * L, 2 * L), 0) % L
    c2 = _iota((2 * L, 2 * L), 1)
    return jnp.where((c2 >= L) | (r2 >= c2), 1.0, 0.0).astype(BF16)


def _sb_tile_chain(c, b, p, tiles, fresh, qh_s, carry_s, acc_s, suffix_op):
    L = CHUNK
    pc = p * LANES
    zs = [_dot_nt(qh_s[c], k_ref[b, :, pc:pc + LANES]) for k_ref, _, _ in tiles]
    yield
    rrs = []
    for z, (_, _, mask) in zip(zs, tiles):
        lk = -(jnp.maximum(z, 0.0) + jnp.log(1.0 + jnp.exp(-jnp.abs(z))))
        if mask is not None:
            lk = jnp.where(mask, lk, 0.0)
        hi, lo = _split2(lk)
        rrs.append(_dot(jnp.concatenate([hi, lo], axis=1), suffix_op))
    yield
    carry = None if fresh else carry_s[c]
    parts = []
    for z, rr, (_, v_ref, mask) in zip(zs, rrs, tiles):
        e = z + rr[:, :L]
        w = jnp.exp(e if carry is None else e + carry)
        if mask is not None:
            w = jnp.where(mask, w, 0.0)
        parts.append(_dot(w.astype(BF16), v_ref[b, :, pc:pc + LANES]))
        carry = rr[:, L:] if carry is None else carry + rr[:, L:]
    carry_s[c] = carry
    yield
    acc = None if fresh else acc_s[c]
    for part in parts:
        acc = part if acc is None else acc + part
    acc_s[c] = acc


def _sb_chains(i, q_ref, k_refs, v_refs, qh_s, carry_s, acc_s):
    L = CHUNK
    batch = q_ref.shape[0]
    lane = _iota((L, LANES), 1)
    row = _iota((L, LANES), 0)
    heads = [(b, p, hh) for b in range(batch) for p in range(2) for hh in range(2)]
    for c, (b, p, hh) in enumerate(heads):
        hm = (lane >= HEAD_DIM) if hh else (lane < HEAD_DIM)
        qh_s[c] = jnp.where(hm, q_ref[b, :, p * LANES:(p + 1) * LANES] * (HEAD_DIM ** -0.5), 0.0).astype(BF16)
    block_no = jnp.zeros((L, LANES), jnp.int32) + i
    masks = [lane < row] + [block_no >= d for d in range(1, SB_FUSED_BLOCKS)]
    tiles = [(k_refs[d], v_refs[d], masks[d]) for d in range(SB_FUSED_BLOCKS)]
    suffix_op = _sb_suffix_op()
    return [_sb_tile_chain(c, b, p, tiles, True, qh_s, carry_s, acc_s, suffix_op) for c, (b, p, hh) in enumerate(heads)]


def _sb_older_blocks(i, sb_hbm, kbuf, vbuf, sem, qh_s, carry_s, acc_s):
    L = CHUNK
    batch = kbuf.shape[0]
    heads = [(b, p, hh) for b in range(batch) for p in range(2) for hh in range(2)]

    def all_underflow():
        return (jnp.max(jnp.max(carry_s[...], axis=0)) <= F32_EXP_UNDERFLOW).astype(jnp.int32)

    def fetch(j, col, buf, slot):
        return pltpu.make_async_copy(
            sb_hbm.at[:, pl.ds(pl.multiple_of(j * L, L), L), pl.ds(col * W_BRANCH, W_BRANCH)], buf, sem.at[slot])

    def cond(st):
        return jnp.logical_and(st[0] >= 0, st[1] == 0)

    def body(st):
        copies = [fetch(st[0], 1, kbuf, 0), fetch(st[0], 2, vbuf, 1)]
        for cp in copies:
            cp.start()
        for cp in copies:
            cp.wait()
        suffix_op = _sb_suffix_op()
        _interleave([_sb_tile_chain(c, b, p, [(kbuf, vbuf, None)], False, qh_s, carry_s, acc_s, suffix_op)
                     for c, (b, p, hh) in enumerate(heads)])
        return (st[0] - 1, all_underflow())

    lax.while_loop(cond, body, (i - SB_FUSED_BLOCKS, all_underflow()))


def _swa_chains(i, sink_ref, cur_ref, prev_ref, tab_ref, posq_ref, posc_ref, posp_ref, qg_ref, kg_ref, put):
    L = CHUNK
    batch = cur_ref.shape[0]
    lane = _iota((L, LANES), 1)
    t = _iota((L, 2 * L), 0)
    j = _iota((L, 2 * L), 1)
    dist = t + L - j
    block_no = jnp.zeros((L, 2 * L), jnp.int32) + i
    valid = (dist >= 0) & (dist < L) & ((j >= L) | (block_no > 0))

    def qk_norm(x, g_ref):
        return x * lax.rsqrt(_head_mean_sq(x) + EPS) * g_ref[...]

    bias = {}

    def both_lanes(pair, g):
        swapped = pltpu.roll(pair, HEAD_DIM, 1)
        return jnp.where(lane < HEAD_DIM, pair, swapped) if g == 0 else jnp.where(lane < HEAD_DIM, swapped, pair)

    def group_chain(b, g):
        sl = slice(g * LANES, (g + 1) * LANES)
        if g == 0:
            bias[b] = _t5_bias(tab_ref, posq_ref[b], posp_ref[b, 0], posc_ref[b, 0])
        qn = qk_norm(cur_ref[b, :, sl], qg_ref) * (HEAD_DIM ** -0.5)
        k_prev = both_lanes(qk_norm(prev_ref[b, :, 0:LANES], kg_ref), g)
        k_cur = both_lanes(qk_norm(cur_ref[b, :, W_BRANCH:W_BRANCH + LANES], kg_ref), g)
        kcat = jnp.concatenate([k_prev, k_cur], axis=0).astype(BF16)
        qk = [_dot_nt(jnp.where((lane >= HEAD_DIM) if r else (lane < HEAD_DIM), qn, 0.0).astype(BF16), kcat)
              for r in range(2)]
        yield
        vcat = jnp.concatenate([both_lanes(prev_ref[b, :, LANES:2 * LANES], g),
                                both_lanes(cur_ref[b, :, W_BRANCH + LANES:W_BRANCH + 2 * LANES], g)], axis=0).astype(BF16)
        outs = []
        for r in range(2):
            hq = 2 * g + r
            logits = jnp.where(valid, qk[r] + bias[b][hq], NEG)
            sink = sink_ref[hq]
            m = jnp.maximum(jnp.max(logits, axis=-1, keepdims=True), sink)
            pr = jnp.exp(logits - m)
            den = jnp.sum(pr, axis=-1, keepdims=True) + jnp.exp(sink - m)
            outs.append(_dot((pr / den).astype(BF16), vcat))
        yield
        put(b, g * LANES, jnp.where(lane < HEAD_DIM, outs[0], outs[1]))

    return [group_chain(b, g) for b in range(batch) for g in range(2)]


DENSE_COLS = 512


def _dense_chain(x_ref, g1_ref, mod1, g2_ref, mod2, branch, wg_ref, wu_ref, wo_ref, w1_ref, w2_ref, out_ref,
                 h_s, merged_s, x1_s, ff_s):
    L = CHUNK
    batch = x_ref.shape[0]
    rows = lambda b: slice(b * L, (b + 1) * L)
    sh1, sc1, gt1 = mod1
    sh2, sc2, gt2 = mod2
    for b in range(batch):
        h_s[rows(b), :] = _norm_mod(x_ref[b], g1_ref[...], sc1[b], sh1[b]).astype(BF16)
    yield
    n_col = D_MODEL // DENSE_COLS
    units = [(j, n) for j in range(n_col) for n in range(N_HEADS)]
    logits = lambda j, n: _dot(h_s[...], wg_ref[:, n * D_MODEL + j * DENSE_COLS:n * D_MODEL + (j + 1) * DENSE_COLS])
    pending = logits(*units[0])
    acc = None
    for u, (j, n) in enumerate(units):
        cs = slice(j * DENSE_COLS, (j + 1) * DENSE_COLS)
        gate_logits = pending
        if u + 1 < len(units):
            pending = logits(*units[u + 1])
        term = _sigmoid(gate_logits) * _dot(branch(n), wu_ref[n, :, cs])
        acc = term if n == 0 else acc + term
        if n == N_HEADS - 1:
            merged_s[:, cs] = acc.astype(BF16)
        yield
    proj = lambda j: _dot(merged_s[...], wo_ref[:, j * DENSE_COLS:(j + 1) * DENSE_COLS])
    pending = proj(0)
    for j in range(n_col):
        cs = slice(j * DENSE_COLS, (j + 1) * DENSE_COLS)
        mix = pending
        if j + 1 < n_col:
            pending = proj(j + 1)
        for b in range(batch):
            x1_s[rows(b), cs] = x_ref[b, :, cs] + gt1[b][:, cs] * mix[rows(b)]
        yield
    for b in range(batch):
        h_s[rows(b), :] = _norm_mod(x1_s[rows(b), :], g2_ref[...], sc2[b], sh2[b]).astype(BF16)
    yield
    n_ff = D_FF // DENSE_COLS
    up = lambda n: _dot(h_s[...], w1_ref[:, n * DENSE_COLS:(n + 1) * DENSE_COLS])
    pending = up(0)
    for n in range(n_ff):
        a = jnp.maximum(pending, 0.0)
        if n + 1 < n_ff:
            pending = up(n + 1)
        part = _dot((a * a).astype(BF16), w2_ref[n * DENSE_COLS:(n + 1) * DENSE_COLS, :])
        ff_s[...] = part if n == 0 else ff_s[...] + part
        yield
    for b in range(batch):
        out_ref[b] = x1_s[rows(b), :] + gt2[b] * ff_s[rows(b), :]


def _layer_kernel(sink_ref, ml_ref, ifc_ref, ift_ref, cw_ref, cb_ref,
                  ret_ref, inv_ref, lg_ref, ng_ref,
                  sbq_ref, sbk0_ref, sbk1_ref, sbk2_ref, sbv0_ref, sbv1_ref, sbv2_ref, sb_hbm,
                  swc_ref, swp_ref, tab_ref, posq_ref, posc_ref, posp_ref, qg_ref, kg_ref,
                  x_ref, g1_ref, sh1_ref, sc1_ref, gt1_ref, g2_ref, sh2_ref, sc2_ref, gt2_ref,
                  wg_ref, wu_ref, wo_ref, w1_ref, w2_ref,
                  out_ref,
                  halo, c_st, m_st, ret_st, qh_s, carry_s, acc_s, kbuf, vbuf, sem,
                  branch_s, h_s, merged_s, x1_s, ff_s, *, n_chunks):
    L = CHUNK
    batch = ml_ref.shape[0]
    step = pl.program_id(0)
    i = jnp.minimum(step, n_chunks - 1)
    slot = step % 2

    @pl.when(step == 0)
    def _init():
        halo[:, 0:8, :] = jnp.zeros((halo.shape[0], 8, LANES), F32)
        c_st[...] = jnp.zeros(c_st.shape, F32)
        m_st[...] = jnp.zeros(m_st.shape, F32)
        ret_st[...] = jnp.zeros(ret_st.shape, F32)
        branch_s[1] = jnp.zeros(branch_s.shape[1:], BF16)

    def put(n):
        def write(b, col, value):
            branch_s[slot, n, b * L:(b + 1) * L, col:col + LANES] = value.astype(BF16)
        return write

    chains = (_mlstm_chains(ml_ref, ifc_ref, ift_ref, cw_ref, cb_ref, put(0), halo, c_st, m_st)
              + _retention_chains(ret_ref, posq_ref, inv_ref, lg_ref, ng_ref, put(1), ret_st)
              + _sb_chains(i, sbq_ref, (sbk0_ref, sbk1_ref, sbk2_ref), (sbv0_ref, sbv1_ref, sbv2_ref),
                           qh_s, carry_s, acc_s)
              + _swa_chains(i, sink_ref, swc_ref, swp_ref, tab_ref, posq_ref, posc_ref, posp_ref, qg_ref, kg_ref,
                            put(3)))
    dense = _dense_chain(x_ref, g1_ref, (sh1_ref, sc1_ref, gt1_ref), g2_ref, (sh2_ref, sc2_ref, gt2_ref),
                         lambda n: branch_s[1 - slot, n], wg_ref, wu_ref, wo_ref, w1_ref, w2_ref, out_ref,
                         h_s, merged_s, x1_s, ff_s)
    _interleave(chains, filler=dense)

    _sb_older_blocks(i, sb_hbm, kbuf, vbuf, sem, qh_s, carry_s, acc_s)
    lane = _iota((L, LANES), 1)
    for b in range(batch):
        for p in range(2):
            c0 = (b * 2 + p) * 2
            put(2)(b, p * LANES, jnp.where(lane < HEAD_DIM, acc_s[c0], acc_s[c0 + 1]))


def _layer(xt, mods, g1, g2, ml, ifc, ift, ret, sb, swa, pos, small, weights, layer, batch, seq):
    d = xt.shape[1]
    nc = seq // CHUNK
    n_chain = batch * N_HEADS
    conv_w, conv_b, lg_tab, ret_g, sinks, q_g, k_g = small
    bias_tab, inv_row, pos_col, pos_row = pos
    w_gate, w_up, w_out, w_ff1, w_ff2 = weights
    mix = lambda c: jnp.minimum(c, nc - 1)
    lag = lambda c: jnp.maximum(c - 1, 0)
    chunk = lambda n, blk=0, back=0: pl.BlockSpec(
        (batch, CHUNK, n), lambda c: (0, jnp.maximum(mix(c) - back, 0), blk))
    sb3 = sb.reshape(batch, seq, -1)
    swa3 = swa.reshape(batch, seq, -1)
    x3 = xt.reshape(batch, seq, d)
    rowblk = pl.BlockSpec((batch, CHUNK, d), lambda c: (0, lag(c), 0))
    in_specs = (
        [pl.BlockSpec(memory_space=pltpu.SMEM),
         chunk(4 * W_BRANCH), chunk(LANES), pl.BlockSpec((batch, 16, CHUNK), lambda c: (0, 0, mix(c))),
         _resident(conv_w), _resident(conv_b),
         chunk(4 * W_BRANCH), _resident(inv_row), _resident(lg_tab), _resident(ret_g),
         chunk(W_BRANCH, 0)]
        + [chunk(W_BRANCH, 1, back) for back in range(SB_FUSED_BLOCKS)]
        + [chunk(W_BRANCH, 2, back) for back in range(SB_FUSED_BLOCKS)]
        + [pl.BlockSpec(memory_space=pl.ANY),
           chunk(2 * W_BRANCH), chunk(W_BRANCH, 1, 1),
           _resident(bias_tab), chunk(LANES),
           pl.BlockSpec((batch, 1, 1, CHUNK), lambda c: (0, mix(c), 0, 0)),
           pl.BlockSpec((batch, 1, 1, CHUNK), lambda c: (0, jnp.maximum(mix(c) - 1, 0), 0, 0)),
           _resident(q_g), _resident(k_g),
           rowblk, _resident(g1)] + [_resident(m) for m in mods[0]] + [_resident(g2)] + [_resident(m) for m in mods[1]]
        + [_layer_of(w, layer) for w in (w_gate, w_up, w_out, w_ff1, w_ff2)])
    rows = batch * CHUNK
    out = pl.pallas_call(
        functools.partial(_layer_kernel, n_chunks=nc),
        grid=(nc + 1,),
        in_specs=in_specs,
        out_specs=rowblk,
        out_shape=jax.ShapeDtypeStruct((batch, seq, d), F32),
        scratch_shapes=[pltpu.VMEM((n_chain, CHUNK + 8, LANES), F32), pltpu.VMEM((n_chain, LANES, LANES), F32),
                        pltpu.VMEM((n_chain, 8, LANES), F32), pltpu.VMEM((batch * 2, LANES, LANES), F32),
                        pltpu.VMEM((n_chain, CHUNK, LANES), BF16), pltpu.VMEM((n_chain, CHUNK, LANES), F32),
                        pltpu.VMEM((n_chain, CHUNK, LANES), F32),
                        pltpu.VMEM((batch, CHUNK, W_BRANCH), BF16), pltpu.VMEM((batch, CHUNK, W_BRANCH), BF16),
                        pltpu.SemaphoreType.DMA((2,)),
                        pltpu.VMEM((2, N_HEADS, rows, W_BRANCH), BF16), pltpu.VMEM((rows, d), BF16),
                        pltpu.VMEM((rows, d), BF16), pltpu.VMEM((rows, d), F32), pltpu.VMEM((rows, d), F32)],
        compiler_params=pltpu.CompilerParams(dimension_semantics=("arbitrary",), vmem_limit_bytes=LAYER_VMEM_LIMIT),
        name="layer",
    )(sinks, ml.reshape(batch, seq, -1), ifc.reshape(batch, seq, -1), ift, conv_w, conv_b,
      ret.reshape(batch, seq, -1), inv_row, lg_tab, ret_g,
      sb3, sb3, sb3, sb3, sb3, sb3, sb3, sb3,
      swa3, swa3, bias_tab, pos_col.reshape(batch, seq, LANES), pos_row, pos_row, q_g, k_g,
      x3, g1, *mods[0], g2, *mods[1], w_gate, w_up, w_out, w_ff1, w_ff2)
    return out.reshape(batch * seq, d)


GATE_COL = 3 * W_BRANCH


def _w_in_kernel(wt_ref, perm_ref, gate_ref, ift_ref):
    one_hot = lambda hit: jnp.where(hit, 1.0, 0.0).astype(BF16)
    eye = one_hot(_iota((LANES, LANES), 0) == _iota((LANES, LANES), 1))
    transpose = lambda rows_bf: _dot_nt(eye, rows_bf).astype(BF16)
    cols = lambda lo, n: transpose(wt_ref[lo:lo + n, :].astype(BF16))
    o_out = GATE_COL + 2 * N_HEADS
    o_ret = o_out + W_BRANCH
    o_sb = o_ret + 4 * W_BRANCH
    o_swa = o_sb + 3 * W_BRANCH
    o_gate = o_swa + 2 * W_BRANCH

    perm_ref[:, 0:GATE_COL] = cols(0, GATE_COL)
    perm_ref[:, GATE_COL:SEG_ML[1]] = cols(o_out, W_BRANCH)
    perm_ref[:, SEG_RET[0]:SEG_RET[1]] = cols(o_ret, 4 * W_BRANCH)
    perm_ref[:, SEG_SB[0]:SEG_SB[1]] = cols(o_sb, 3 * W_BRANCH)
    perm_ref[:, SEG_SWA[0]:SEG_SWA[1]] = cols(o_swa, 2 * W_BRANCH)
    gate_ref[...] = cols(o_gate, gate_ref.shape[1])

    block = wt_ref[GATE_COL:GATE_COL + LANES, :].astype(BF16)
    def placement(n_dst):
        dst = _iota((n_dst, LANES), 0)
        j = dst % 4
        src = jnp.where(j < 2, j, j + 2) + 2 * (dst // 4)
        return one_hot((_iota((n_dst, LANES), 1) == src) & (dst < 2 * N_HEADS))
    ift_ref[...] = _dot(placement(16), block).astype(BF16)
    perm_ref[:, SEG_IFC[0]:SEG_IFC[1]] = transpose(_dot(placement(LANES), block).astype(BF16))


def _prep_w_in(w_in):
    depth, d, n_cols = w_in.shape
    tk = LANES
    return pl.pallas_call(
        _w_in_kernel,
        grid=(depth, d // tk),
        in_specs=[pl.BlockSpec((None, n_cols, tk), lambda l, i: (l, 0, i))],
        out_specs=[pl.BlockSpec((None, tk, N_IN), lambda l, i: (l, i, 0)),
                   pl.BlockSpec((None, tk, N_HEADS * d), lambda l, i: (l, i, 0)),
                   pl.BlockSpec((None, 16, tk), lambda l, i: (l, 0, i))],
        out_shape=[jax.ShapeDtypeStruct((depth, d, N_IN), BF16), jax.ShapeDtypeStruct((depth, d, N_HEADS * d), BF16),
                   jax.ShapeDtypeStruct((depth, 16, d), BF16)],
        compiler_params=_params(("parallel", "parallel")),
        name="prep_w_in",
    )(jnp.swapaxes(w_in, 1, 2))


CAST_TILE_BYTES = 2 * 1024 * 1024


def _cast_kernel(x_ref, o_ref):
    o_ref[...] = x_ref[...].astype(BF16)


def _to_bf16(a):
    cols = a.shape[-1]
    rows = a.size // cols
    tm = CAST_TILE_BYTES // (4 * cols)
    spec = pl.BlockSpec((tm, cols), lambda i: (i, 0))
    out = pl.pallas_call(
        _cast_kernel,
        grid=(rows // tm,),
        in_specs=[spec],
        out_specs=spec,
        out_shape=jax.ShapeDtypeStruct((rows, cols), BF16),
        compiler_params=_params(("parallel",)),
        name="cast_bf16",
    )(a.reshape(rows, cols))
    return out.reshape(a.shape)


def _gate_bias_layout(gate_b):
    ib, fb = gate_b[0], gate_b[1]
    order = jnp.concatenate([ib[0:2], fb[0:2], ib[2:4], fb[2:4]])
    row = jnp.concatenate([order, jnp.zeros((LANES - 2 * N_HEADS,), F32)]).reshape(1, LANES)
    col = jnp.concatenate([order, jnp.zeros((16 - 2 * N_HEADS,), F32)])
    return row, jnp.broadcast_to(col[:, None], (16, LANES))


def kernel(x, c, positions, w_ada, b_ada, norm_g, w_in, mlstm_conv_w, mlstm_conv_b, mlstm_gate_b,
           ret_norm_g, swa_q_norm_g, swa_k_norm_g, swa_sinks, rel_bias, w_up, w_out, w_ff1, w_ff2):
    batch, seq, d = x.shape
    depth = w_in.shape[0]
    t = batch * seq
    nb = seq // CHUNK

    c8 = jnp.concatenate([c, jnp.zeros((8 - batch, d), F32)], axis=0)
    mod = _ada(c8, w_ada.reshape(depth * 2, d, 3 * d), b_ada.reshape(depth * 2, 1, 3 * d))
    mod = mod[:, :batch].reshape(depth, 2, batch, 3, 1, d)

    pos_col = jnp.broadcast_to(positions.reshape(t, 1), (t, LANES))
    pos_row = positions.reshape(batch, nb, 1, CHUNK)
    half = HEAD_DIM // 2
    inv = ROPE_BASE ** (-(np.arange(LANES) % half).astype(np.float64) / half)
    inv_row = jnp.asarray(inv, F32).reshape(1, LANES)
    bias_tab = jnp.concatenate([rel_bias.T, jnp.zeros((N_HEADS, LANES - N_BUCKETS), F32)], axis=1)

    log_gamma = np.log(1.0 - np.exp2(-(RET_DECAY_BASE + np.arange(N_HEADS, dtype=np.float64))))
    lg_tab = jnp.asarray(np.broadcast_to(np.repeat(log_gamma, HEAD_DIM).reshape(2, 1, LANES), (2, 8, LANES)), F32)

    w_perm, w_gate, w_ift = _prep_w_in(w_in)
    w_up_bf, w_out_bf, w_ff1_bf, w_ff2_bf = (_to_bf16(w) for w in (w_up, w_out, w_ff1, w_ff2))

    xt = x.reshape(t, d)
    for l in range(depth):
        gb_row, gbt = _gate_bias_layout(mlstm_gate_b[l])
        g1 = norm_g[l, 0].reshape(1, d)
        g2 = norm_g[l, 1].reshape(1, d)
        shift1, scale1, gate1 = mod[l, 0, :, 0], mod[l, 0, :, 1], mod[l, 0, :, 2]
        shift2, scale2, gate2 = mod[l, 1, :, 0], mod[l, 1, :, 1], mod[l, 1, :, 2]

        ml, ifc, ret, sb, swa, ift = _in_proj(xt, g1, scale1, shift1, w_perm, w_ift, gb_row, gbt, l, batch, seq)
        small = (mlstm_conv_w[l], mlstm_conv_b[l].reshape(1, 2 * W_BRANCH), lg_tab, ret_norm_g[l].reshape(1, W_BRANCH),
                 swa_sinks[l], jnp.tile(swa_q_norm_g[l], 2).reshape(1, LANES), jnp.tile(swa_k_norm_g[l], 2).reshape(1, LANES))
        xt = _layer(xt, ((shift1, scale1, gate1), (shift2, scale2, gate2)), g1, g2, ml, ifc, ift, ret, sb, swa,
                    (bias_tab, inv_row, pos_col, pos_row), small, (w_gate, w_up_bf, w_out_bf, w_ff1_bf, w_ff2_bf), l, batch, seq)
    return xt.reshape(batch, seq, d)
```

```python
import functools
import math

import numpy as np
import jax
import jax.numpy as jnp
from jax import lax
from jax.experimental import pallas as pl
from jax.experimental.pallas import tpu as pltpu

F32 = jnp.float32
BF16 = jnp.bfloat16

D_MODEL = 1024
HEAD_DIM = 64
N_HEADS = 4
W_BRANCH = N_HEADS * HEAD_DIM
LANES = 128
CHUNK = 128
CONV_K = 4
D_FF = 4 * D_MODEL
N_BUCKETS = 32
MAX_DIST = 128
ROPE_BASE = 10000.0
RET_DECAY_BASE = 5.0
EPS = 1e-6
NEG = -1e30
F32_EXP_UNDERFLOW = -104.0
VMEM_LIMIT = 56 * 1024 * 1024
LAYER_VMEM_LIMIT = 60 * 1024 * 1024

SEG_ML = (0, 1024)
SEG_IFC = (1024, 1152)
SEG_RET = (1152, 2176)
SEG_SB = (2176, 2944)
SEG_SWA = (2944, 3456)
N_IN = 3456


def _dot(a, b):
    return jnp.dot(a, b, preferred_element_type=F32)


def _dot_nt(a, b):
    return lax.dot_general(a, b, (((1,), (1,)), ((), ())), preferred_element_type=F32)


def _dot_tn(a, b):
    return lax.dot_general(a, b, (((0,), (0,)), ((), ())), preferred_element_type=F32)


def _split2(x):
    hi = x.astype(BF16)
    lo = (x - hi.astype(F32)).astype(BF16)
    return hi, lo


def _split3(x):
    hi = x.astype(BF16)
    r = x - hi.astype(F32)
    mid = r.astype(BF16)
    lo = (r - mid.astype(F32)).astype(BF16)
    return hi, mid, lo


def _iota(shape, axis):
    return lax.broadcasted_iota(jnp.int32, shape, axis)


def _log_sigmoid(x):
    return jnp.minimum(x, 0.0) - jnp.log(1.0 + jnp.exp(-jnp.abs(x)))


def _sigmoid(x):
    return 1.0 / (1.0 + jnp.exp(-x))


def _silu(x):
    return x * _sigmoid(x)


def _norm_mod(x, g, scale, shift):
    ms = jnp.mean(x * x, axis=-1, keepdims=True)
    y = x * lax.rsqrt(ms + EPS)
    return (y * g) * (1.0 + scale) + shift


def _head_mean_sq(x):
    lane = _iota(x.shape, 1)
    sq = x * x
    s0 = jnp.sum(jnp.where(lane < HEAD_DIM, sq, 0.0), axis=-1, keepdims=True)
    s1 = jnp.sum(jnp.where(lane >= HEAD_DIM, sq, 0.0), axis=-1, keepdims=True)
    return jnp.where(lane < HEAD_DIM, s0, s1) * (1.0 / HEAD_DIM)


def _params(sem):
    return pltpu.CompilerParams(dimension_semantics=sem, vmem_limit_bytes=VMEM_LIMIT)


def _interleave(chains, filler=None, every=1):
    def advance(gen):
        try:
            next(gen)
            return True
        except StopIteration:
            return False

    live = list(chains)
    filling = filler is not None
    steps = 0
    while live:
        still = []
        for ch in live:
            if advance(ch):
                still.append(ch)
            steps += 1
            if filling and steps % every == 0:
                filling = advance(filler)
        live = still
    while filling:
        filling = advance(filler)


def _resident(a):
    return pl.BlockSpec(a.shape, lambda *_: (0,) * a.ndim, pipeline_mode=pl.Buffered(1))


def _layer_of(a, layer):
    return pl.BlockSpec((None,) + a.shape[1:], lambda *_: (layer,) + (0,) * (a.ndim - 1),
                        pipeline_mode=pl.Buffered(1))


def _ada_kernel(c_ref, w_ref, b_ref, o_ref):
    c = c_ref[...]
    ch, cl = _split2(_silu(c))
    wh, wl = _split2(w_ref[0])
    o_ref[0] = _dot(ch, wh) + _dot(ch, wl) + _dot(cl, wh) + b_ref[0]


def _ada(c8, w_ada, b_ada):
    n_mod, d, n3 = w_ada.shape
    tn = 1024
    return pl.pallas_call(
        _ada_kernel,
        grid=(n_mod, n3 // tn),
        in_specs=[pl.BlockSpec((8, d), lambda m, n: (0, 0)),
                  pl.BlockSpec((1, d, tn), lambda m, n: (m, 0, n)),
                  pl.BlockSpec((1, 1, tn), lambda m, n: (m, 0, n))],
        out_specs=pl.BlockSpec((1, 8, tn), lambda m, n: (m, 0, n)),
        out_shape=jax.ShapeDtypeStruct((n_mod, 8, n3), F32),
        compiler_params=_params(("parallel", "parallel")),
        name="ada_mod",
    )(c8, w_ada, b_ada)


def _t5_bias(tab_ref, pos_q, pos_prev, pos_cur):
    max_exact = N_BUCKETS // 2
    rel = jnp.concatenate([pos_q - pos_prev, pos_q - pos_cur], axis=1)
    n = jnp.maximum(rel, 0)
    nf = jnp.maximum(n, 1).astype(F32)
    large = max_exact + (jnp.log(nf / max_exact) / math.log(MAX_DIST / max_exact)
                         * (N_BUCKETS - max_exact)).astype(jnp.int32)
    large = jnp.minimum(large, N_BUCKETS - 1)
    bucket = jnp.where(n < max_exact, n, large)
    halves = (bucket[:, :CHUNK], bucket[:, CHUNK:])
    tiles = []
    for h in range(N_HEADS):
        row = jnp.broadcast_to(tab_ref[h:h + 1, :], (CHUNK, LANES))
        tiles.append(jnp.concatenate([jnp.take_along_axis(row, idx, axis=1) for idx in halves], axis=1))
    return tiles


def _in_kernel(x_ref, g_ref, sc_ref, sh_ref, w_ref, wt_ref, gb_ref, gbt_ref,
               ml_ref, ifc_ref, ret_ref, sb_ref, swa_ref, ift_ref):
    h = _norm_mod(x_ref[...], g_ref[...], sc_ref[0], sh_ref[0]).astype(BF16)

    def seg(s):
        return _dot(h, w_ref[:, s[0]:s[1]])

    ml_ref[...] = seg(SEG_ML)
    ifc_ref[...] = seg(SEG_IFC) + gb_ref[...]
    ret_ref[...] = seg(SEG_RET)
    sb_ref[...] = seg(SEG_SB).astype(BF16)
    swa_ref[...] = seg(SEG_SWA)
    ift_ref[0] = _dot_nt(wt_ref[...], h) + gbt_ref[:, 0:1]


def _in_proj(x, g, scale, shift, w_perm, w_ift, gb_row, gbt, layer, batch, seq):
    t, d = x.shape
    tm = 1024
    tpb = seq // tm
    row = lambda n: pl.BlockSpec((tm, n), lambda i: (i, 0))
    mod = pl.BlockSpec((1, 1, d), lambda i: (i // tpb, 0, 0))
    widths = [s[1] - s[0] for s in (SEG_ML, SEG_IFC, SEG_RET, SEG_SB, SEG_SWA)]
    dts = [F32, F32, F32, BF16, F32]
    return pl.pallas_call(
        _in_kernel,
        grid=(t // tm,),
        in_specs=[row(d), _resident(g), mod, mod, _layer_of(w_perm, layer), _layer_of(w_ift, layer), _resident(gb_row),
                  _resident(gbt)],
        out_specs=[row(n) for n in widths] + [pl.BlockSpec((1, 16, tm), lambda i: (i // tpb, 0, i % tpb))],
        out_shape=[jax.ShapeDtypeStruct((t, n), dt) for n, dt in zip(widths, dts)]
        + [jax.ShapeDtypeStruct((batch, 16, seq), F32)],
        compiler_params=_params(("parallel",)),
        name="in_proj",
    )(x, g, scale, shift, w_perm, w_ift, gb_row, gbt)


def _mlstm_chains(ml_ref, ifc_ref, ift_ref, cw_ref, cb_ref, put, halo, c_st, m_st):
    L = CHUNK
    HALO = 8
    batch = ml_ref.shape[0]
    lane = _iota((L, LANES), 1)
    row = _iota((L, LANES), 0)
    causal = row >= lane
    tri = jnp.where(causal, 1.0, 0.0).astype(BF16)
    upper = jnp.where(row <= lane, 1.0, 0.0).astype(BF16)

    def conv_silu(b, col, slot):
        hl = halo.at[slot]
        hl[HALO:HALO + L, :] = ml_ref[b, :, col:col + LANES]
        acc = jnp.zeros((L, LANES), F32) + cb_ref[:, col:col + LANES]
        for j in range(CONV_K):
            off = HALO - (CONV_K - 1) + j
            acc = acc + hl[off:off + L, :] * cw_ref[j:j + 1, col:col + LANES]
        hl[0:HALO, :] = hl[L:L + HALO, :]
        return _silu(acc)

    def pair_chain(b, p):
        pc = p * LANES
        ift = ift_ref[b, 0:8, :]
        lr_h, lr_m, lr_l = _split3(_log_sigmoid(ift))
        a_rows = _dot(lr_h, upper) + _dot(lr_m, upper) + _dot(lr_l, upper)
        ifc = ifc_ref[b]
        gates_b = jnp.concatenate([jnp.broadcast_to(ifc[:, 4 * p + j:4 * p + j + 1], (L, LANES)) for j in range(4)],
                                  axis=1)
        q = conv_silu(b, pc, (b * 2 + p) * 2)
        k = conv_silu(b, W_BRANCH + pc, (b * 2 + p) * 2 + 1)
        v = ml_ref[b, :, 2 * W_BRANCH + pc:2 * W_BRANCH + pc + LANES]
        k_bf = k.astype(BF16)
        yield

        lf_h, lf_m, lf_l = _split3(_log_sigmoid(gates_b[:, 2 * LANES:]))
        a_b = _dot(tri, lf_h) + _dot(tri, lf_m) + _dot(tri, lf_l)
        heads = []
        for hh in range(2):
            ch = (b * 2 + p) * 2 + hh
            hm = (lane >= HEAD_DIM) if hh else (lane < HEAD_DIM)
            qh = (jnp.where(hm, q, 0.0) * (HEAD_DIM ** -0.5)).astype(BF16)
            c_in = c_st[ch]
            heads.append(dict(ch=ch, hm=hm, c_in=c_in, s_qk=_dot_nt(qh, k_bf), q_c=_dot(qh, c_in.astype(BF16))))
        yield

        for hh, hd in enumerate(heads):
            i_col = gates_b[:, hh * LANES:(hh + 1) * LANES]
            a_col = a_b[:, hh * LANES:(hh + 1) * LANES]
            a_last = a_col[L - 1:L, :]
            m_in = m_st[hd["ch"], 0:1, :]
            v_aug = jnp.where(hd["hm"], v, 1.0).astype(BF16)

            r0 = 4 * p + hh
            d_intra = jnp.where(causal, a_col - (a_rows[r0 + 2:r0 + 3, :] - ift[r0:r0 + 1, :]), NEG)
            m_intra = jnp.max(d_intra, axis=-1, keepdims=True)
            m_inter = a_col + m_in
            m_tot = jnp.maximum(m_inter, m_intra)
            pm = jnp.exp(d_intra - m_tot) * hd["s_qk"]
            hd["p_v"] = _dot(pm.astype(BF16), v_aug)
            hd["s_inter"] = jnp.exp(m_inter - m_tot)
            hd["floor"] = jnp.exp(-m_tot)

            g_col = a_last - a_col + i_col
            m_loc = jnp.max(g_col, axis=0, keepdims=True)
            kw = (jnp.where(hd["hm"], k, 0.0) * jnp.exp(g_col - m_loc)).astype(BF16)
            hd["c_loc"] = _dot_tn(kw, v_aug)
            m_new = jnp.maximum(a_last + m_in, m_loc)
            hd["w_old"] = jnp.exp(a_last + m_in - m_new)
            hd["w_new"] = jnp.exp(m_loc - m_new)
            m_st[hd["ch"]] = jnp.zeros((8, LANES), F32) + m_new
        yield

        for hd in heads:
            hd["o_aug"] = hd["s_inter"] * hd["q_c"] + hd["p_v"]
        den = pltpu.roll(jnp.where(lane < HEAD_DIM, heads[1]["o_aug"], heads[0]["o_aug"]), HEAD_DIM, 1)
        num = jnp.where(lane < HEAD_DIM, heads[0]["o_aug"], heads[1]["o_aug"])
        floor = jnp.where(lane < HEAD_DIM, heads[0]["floor"], heads[1]["floor"])
        h_ml = num / jnp.maximum(jnp.abs(den), floor)
        o_gate = ml_ref[b, :, 3 * W_BRANCH + pc:3 * W_BRANCH + pc + LANES]
        put(b, pc, h_ml * _sigmoid(o_gate))
        yield

        for hd in heads:
            c_st[hd["ch"]] = hd["w_old"] * hd["c_in"] + hd["w_new"] * hd["c_loc"]

    return [pair_chain(b, p) for b in range(batch) for p in range(2)]


def _retention_chains(ret_ref, posq_ref, inv_ref, lg_ref, ng_ref, put, st):
    L = CHUNK
    batch = ret_ref.shape[0]
    lane = _iota((L, LANES), 1)
    row = _iota((L, LANES), 0)
    rowf = row.astype(F32)
    relf = jnp.maximum(row - lane, 0).astype(F32)
    first = (lane % HEAD_DIM) < (HEAD_DIM // 2)
    same_head = (row // HEAD_DIM) == (lane // HEAD_DIM)

    def rot(t, cos, sin):
        partner = jnp.where(first, -pltpu.roll(t, LANES - HEAD_DIM // 2, 1), pltpu.roll(t, HEAD_DIM // 2, 1))
        return t * cos + partner * sin

    decays = []
    for p in range(2):
        lg = lg_ref[p, 0:1, :]
        decays.append(dict(
            q=jnp.exp((rowf + 1.0) * lg), k=jnp.exp((L - 1.0 - rowf) * lg), c=jnp.exp(float(L) * lg),
            intra=[jnp.where(row >= lane, jnp.exp(relf * lg[:, hh * HEAD_DIM:hh * HEAD_DIM + 1]), 0.0)
                   for hh in range(2)]))

    tables = {}

    def pair_chain(b, p):
        pc = p * LANES
        dec = decays[p]
        if p == 0:
            ang = posq_ref[b].astype(F32) * inv_ref[...]
            tables[b] = (jnp.cos(ang), jnp.sin(ang))
        cos, sin = tables[b]
        q = rot(ret_ref[b, :, pc:pc + LANES], cos, sin)
        k = rot(ret_ref[b, :, W_BRANCH + pc:W_BRANCH + pc + LANES], cos, sin) * (HEAD_DIM ** -0.5)
        v_bf = ret_ref[b, :, 2 * W_BRANCH + pc:2 * W_BRANCH + pc + LANES].astype(BF16)
        k_bf = k.astype(BF16)
        state = st[b * 2 + p]
        s_qk = [_dot_nt(jnp.where((lane >= HEAD_DIM) if hh else (lane < HEAD_DIM), q, 0.0).astype(BF16), k_bf)
                for hh in range(2)]
        inter = _dot(q.astype(BF16), state.astype(BF16))
        kv = _dot_tn((k * dec["k"]).astype(BF16), v_bf)
        yield

        intra = [_dot((s_qk[hh] * dec["intra"][hh]).astype(BF16), v_bf) for hh in range(2)]
        st[b * 2 + p] = dec["c"] * state + jnp.where(same_head, kv, 0.0)
        yield

        o = jnp.where(lane < HEAD_DIM, intra[0], intra[1]) + inter * dec["q"]
        y = o * lax.rsqrt(_head_mean_sq(o) + EPS) * ng_ref[:, pc:pc + LANES]
        gate = ret_ref[b, :, 3 * W_BRANCH + pc:3 * W_BRANCH + pc + LANES]
        put(b, pc, y * _silu(gate))

    return [pair_chain(b, p) for b in range(batch) for p in range(2)]


SB_FUSED_BLOCKS = 3


def _sb_suffix_op():
    L = CHUNK
    r2 = _iota((2 * L, 2 * L), 0) % L
    c2 = _iota((2 * L, 2 * L), 1)
    return jnp.where((c2 >= L) | (r2 >= c2), 1.0, 0.0).astype(BF16)


def _sb_tile_chain(c, b, p, tiles, fresh, qh_s, carry_s, acc_s, suffix_op):
    L = CHUNK
    pc = p * LANES
    zs = [_dot_nt(qh_s[c], k_ref[b, :, pc:pc + LANES]) for k_ref, _, _ in tiles]
    yield
    rrs = []
    for z, (_, _, mask) in zip(zs, tiles):
        lk = -(jnp.maximum(z, 0.0) + jnp.log(1.0 + jnp.exp(-jnp.abs(z))))
        if mask is not None:
            lk = jnp.where(mask, lk, 0.0)
        hi, lo = _split2(lk)
        rrs.append(_dot(jnp.concatenate([hi, lo], axis=1), suffix_op))
    yield
    carry = None if fresh else carry_s[c]
    parts = []
    for z, rr, (_, v_ref, mask) in zip(zs, rrs, tiles):
        e = z + rr[:, :L]
        w = jnp.exp(e if carry is None else e + carry)
        if mask is not None:
            w = jnp.where(mask, w, 0.0)
        parts.append(_dot(w.astype(BF16), v_ref[b, :, pc:pc + LANES]))
        carry = rr[:, L:] if carry is None else carry + rr[:, L:]
    carry_s[c] = carry
    yield
    acc = None if fresh else acc_s[c]
    for part in parts:
        acc = part if acc is None else acc + part
    acc_s[c] = acc


def _sb_chains(i, q_ref, k_refs, v_refs, qh_s, carry_s, acc_s):
    L = CHUNK
    batch = q_ref.shape[0]
    lane = _iota((L, LANES), 1)
    row = _iota((L, LANES), 0)
    heads = [(b, p, hh) for b in range(batch) for p in range(2) for hh in range(2)]
    for c, (b, p, hh) in enumerate(heads):
        hm = (lane >= HEAD_DIM) if hh else (lane < HEAD_DIM)
        qh_s[c] = jnp.where(hm, q_ref[b, :, p * LANES:(p + 1) * LANES] * (HEAD_DIM ** -0.5), 0.0).astype(BF16)
    block_no = jnp.zeros((L, LANES), jnp.int32) + i
    masks = [lane < row] + [block_no >= d for d in range(1, SB_FUSED_BLOCKS)]
    tiles = [(k_refs[d], v_refs[d], masks[d]) for d in range(SB_FUSED_BLOCKS)]
    suffix_op = _sb_suffix_op()
    return [_sb_tile_chain(c, b, p, tiles, True, qh_s, carry_s, acc_s, suffix_op) for c, (b, p, hh) in enumerate(heads)]


def _sb_older_blocks(i, k_next_ref, v_next_ref, sb_hbm, kbuf, vbuf, sem, qh_s, carry_s, acc_s):
    L = CHUNK
    batch = kbuf.shape[0]
    heads = [(b, p, hh) for b in range(batch) for p in range(2) for hh in range(2)]

    def all_underflow():
        return (jnp.max(jnp.max(carry_s[...], axis=0)) <= F32_EXP_UNDERFLOW).astype(jnp.int32)

    def fetch(j, col, buf, slot):
        return pltpu.make_async_copy(
            sb_hbm.at[:, pl.ds(pl.multiple_of(j * L, L), L), pl.ds(col * W_BRANCH, W_BRANCH)], buf, sem.at[slot])

    def cond(st):
        return jnp.logical_and(st[0] >= 0, st[1] == 0)

    def one_block(k_ref, v_ref):
        suffix_op = _sb_suffix_op()
        _interleave([_sb_tile_chain(c, b, p, [(k_ref, v_ref, None)], False, qh_s, carry_s, acc_s, suffix_op)
                     for c, (b, p, hh) in enumerate(heads)])

    def body(st):
        copies = [fetch(st[0], 1, kbuf, 0), fetch(st[0], 2, vbuf, 1)]
        for cp in copies:
            cp.start()
        for cp in copies:
            cp.wait()
        one_block(kbuf, vbuf)
        return (st[0] - 1, all_underflow())

    first = i - SB_FUSED_BLOCKS
    pl.when(jnp.logical_and(first >= 0, all_underflow() == 0))(functools.partial(one_block, k_next_ref, v_next_ref))
    lax.while_loop(cond, body, (first - 1, all_underflow()))


def _swa_chains(i, sink_ref, cur_ref, prev_ref, tab_ref, posq_ref, posc_ref, posp_ref, qg_ref, kg_ref, put):
    L = CHUNK
    batch = cur_ref.shape[0]
    lane = _iota((L, LANES), 1)
    t = _iota((L, 2 * L), 0)
    j = _iota((L, 2 * L), 1)
    dist = t + L - j
    block_no = jnp.zeros((L, 2 * L), jnp.int32) + i
    valid = (dist >= 0) & (dist < L) & ((j >= L) | (block_no > 0))

    def qk_norm(x, g_ref):
        return x * lax.rsqrt(_head_mean_sq(x) + EPS) * g_ref[...]

    bias = {}

    def both_lanes(pair, g):
        swapped = pltpu.roll(pair, HEAD_DIM, 1)
        return jnp.where(lane < HEAD_DIM, pair, swapped) if g == 0 else jnp.where(lane < HEAD_DIM, swapped, pair)

    def group_chain(b, g):
        sl = slice(g * LANES, (g + 1) * LANES)
        if g == 0:
            bias[b] = _t5_bias(tab_ref, posq_ref[b], posp_ref[b, 0], posc_ref[b, 0])
        qn = qk_norm(cur_ref[b, :, sl], qg_ref) * (HEAD_DIM ** -0.5)
        k_prev = both_lanes(qk_norm(prev_ref[b, :, 0:LANES], kg_ref), g)
        k_cur = both_lanes(qk_norm(cur_ref[b, :, W_BRANCH:W_BRANCH + LANES], kg_ref), g)
        kcat = jnp.concatenate([k_prev, k_cur], axis=0).astype(BF16)
        qk = [_dot_nt(jnp.where((lane >= HEAD_DIM) if r else (lane < HEAD_DIM), qn, 0.0).astype(BF16), kcat)
              for r in range(2)]
        yield
        vcat = jnp.concatenate([both_lanes(prev_ref[b, :, LANES:2 * LANES], g),
                                both_lanes(cur_ref[b, :, W_BRANCH + LANES:W_BRANCH + 2 * LANES], g)], axis=0).astype(BF16)
        outs = []
        for r in range(2):
            hq = 2 * g + r
            logits = jnp.where(valid, qk[r] + bias[b][hq], NEG)
            sink = sink_ref[hq]
            m = jnp.maximum(jnp.max(logits, axis=-1, keepdims=True), sink)
            pr = jnp.exp(logits - m)
            den = jnp.sum(pr, axis=-1, keepdims=True) + jnp.exp(sink - m)
            outs.append(_dot((pr / den).astype(BF16), vcat))
        yield
        put(b, g * LANES, jnp.where(lane < HEAD_DIM, outs[0], outs[1]))

    return [group_chain(b, g) for b in range(batch) for g in range(2)]


DENSE_COLS = 512


def _dense_chain(x_ref, g1_ref, mod1, g2_ref, mod2, branch, wg_ref, wu_ref, wo_ref, w1_ref, w2_ref, out_ref,
                 h_s, merged_s, x1_s, ff_s):
    L = CHUNK
    batch = x_ref.shape[0]
    rows = lambda b: slice(b * L, (b + 1) * L)
    sh1, sc1, gt1 = mod1
    sh2, sc2, gt2 = mod2
    for b in range(batch):
        h_s[rows(b), :] = _norm_mod(x_ref[b], g1_ref[...], sc1[b], sh1[b]).astype(BF16)
    yield
    n_col = D_MODEL // DENSE_COLS
    units = [(j, n) for j in range(n_col) for n in range(N_HEADS)]
    logits = lambda j, n: _dot(h_s[...], wg_ref[:, n * D_MODEL + j * DENSE_COLS:n * D_MODEL + (j + 1) * DENSE_COLS])
    pending = logits(*units[0])
    acc = None
    for u, (j, n) in enumerate(units):
        cs = slice(j * DENSE_COLS, (j + 1) * DENSE_COLS)
        gate_logits = pending
        if u + 1 < len(units):
            pending = logits(*units[u + 1])
        term = _sigmoid(gate_logits) * _dot(branch(n), wu_ref[n, :, cs])
        acc = term if n == 0 else acc + term
        if n == N_HEADS - 1:
            merged_s[:, cs] = acc.astype(BF16)
        yield
    proj = lambda j: _dot(merged_s[...], wo_ref[:, j * DENSE_COLS:(j + 1) * DENSE_COLS])
    pending = proj(0)
    for j in range(n_col):
        cs = slice(j * DENSE_COLS, (j + 1) * DENSE_COLS)
        mix = pending
        if j + 1 < n_col:
            pending = proj(j + 1)
        for b in range(batch):
            x1_s[rows(b), cs] = x_ref[b, :, cs] + gt1[b][:, cs] * mix[rows(b)]
        yield
    for b in range(batch):
        h_s[rows(b), :] = _norm_mod(x1_s[rows(b), :], g2_ref[...], sc2[b], sh2[b]).astype(BF16)
    yield
    n_ff = D_FF // DENSE_COLS
    up = lambda n: _dot(h_s[...], w1_ref[:, n * DENSE_COLS:(n + 1) * DENSE_COLS])
    pending = up(0)
    for n in range(n_ff):
        a = jnp.maximum(pending, 0.0)
        if n + 1 < n_ff:
            pending = up(n + 1)
        part = _dot((a * a).astype(BF16), w2_ref[n * DENSE_COLS:(n + 1) * DENSE_COLS, :])
        ff_s[...] = part if n == 0 else ff_s[...] + part
        yield
    for b in range(batch):
        out_ref[b] = x1_s[rows(b), :] + gt2[b] * ff_s[rows(b), :]


def _layer_kernel(sink_ref, ml_ref, ifc_ref, ift_ref, cw_ref, cb_ref,
                  ret_ref, inv_ref, lg_ref, ng_ref,
                  sbq_ref, sbk0_ref, sbk1_ref, sbk2_ref, sbk3_ref, sbv0_ref, sbv1_ref, sbv2_ref, sbv3_ref, sb_hbm,
                  swc_ref, swp_ref, tab_ref, posq_ref, posc_ref, posp_ref, qg_ref, kg_ref,
                  x_ref, g1_ref, sh1_ref, sc1_ref, gt1_ref, g2_ref, sh2_ref, sc2_ref, gt2_ref,
                  wg_ref, wu_ref, wo_ref, w1_ref, w2_ref,
                  out_ref,
                  halo, c_st, m_st, ret_st, qh_s, carry_s, acc_s, kbuf, vbuf, sem,
                  branch_s, h_s, merged_s, x1_s, ff_s, *, n_chunks):
    L = CHUNK
    batch = ml_ref.shape[0]
    step = pl.program_id(0)
    i = jnp.minimum(step, n_chunks - 1)
    slot = step % 2

    @pl.when(step == 0)
    def _init():
        halo[:, 0:8, :] = jnp.zeros((halo.shape[0], 8, LANES), F32)
        c_st[...] = jnp.zeros(c_st.shape, F32)
        m_st[...] = jnp.zeros(m_st.shape, F32)
        ret_st[...] = jnp.zeros(ret_st.shape, F32)
        branch_s[1] = jnp.zeros(branch_s.shape[1:], BF16)

    def put(n):
        def write(b, col, value):
            branch_s[slot, n, b * L:(b + 1) * L, col:col + LANES] = value.astype(BF16)
        return write

    chains = (_mlstm_chains(ml_ref, ifc_ref, ift_ref, cw_ref, cb_ref, put(0), halo, c_st, m_st)
              + _retention_chains(ret_ref, posq_ref, inv_ref, lg_ref, ng_ref, put(1), ret_st)
              + _sb_chains(i, sbq_ref, (sbk0_ref, sbk1_ref, sbk2_ref), (sbv0_ref, sbv1_ref, sbv2_ref),
                           qh_s, carry_s, acc_s)
              + _swa_chains(i, sink_ref, swc_ref, swp_ref, tab_ref, posq_ref, posc_ref, posp_ref, qg_ref, kg_ref,
                            put(3)))
    dense = _dense_chain(x_ref, g1_ref, (sh1_ref, sc1_ref, gt1_ref), g2_ref, (sh2_ref, sc2_ref, gt2_ref),
                         lambda n: branch_s[1 - slot, n], wg_ref, wu_ref, wo_ref, w1_ref, w2_ref, out_ref,
                         h_s, merged_s, x1_s, ff_s)
    _interleave(chains, filler=dense)

    _sb_older_blocks(i, sbk3_ref, sbv3_ref, sb_hbm, kbuf, vbuf, sem, qh_s, carry_s, acc_s)
    lane = _iota((L, LANES), 1)
    for b in range(batch):
        for p in range(2):
            c0 = (b * 2 + p) * 2
            put(2)(b, p * LANES, jnp.where(lane < HEAD_DIM, acc_s[c0], acc_s[c0 + 1]))


def _layer(xt, mods, g1, g2, ml, ifc, ift, ret, sb, swa, pos, small, weights, layer, batch, seq):
    d = xt.shape[1]
    nc = seq // CHUNK
    n_chain = batch * N_HEADS
    conv_w, conv_b, lg_tab, ret_g, sinks, q_g, k_g = small
    bias_tab, inv_row, pos_col, pos_row = pos
    w_gate, w_up, w_out, w_ff1, w_ff2 = weights
    mix = lambda c: jnp.minimum(c, nc - 1)
    lag = lambda c: jnp.maximum(c - 1, 0)
    chunk = lambda n, blk=0, back=0: pl.BlockSpec(
        (batch, CHUNK, n), lambda c: (0, jnp.maximum(mix(c) - back, 0), blk))
    sb3 = sb.reshape(batch, seq, -1)
    swa3 = swa.reshape(batch, seq, -1)
    x3 = xt.reshape(batch, seq, d)
    rowblk = pl.BlockSpec((batch, CHUNK, d), lambda c: (0, lag(c), 0))
    in_specs = (
        [pl.BlockSpec(memory_space=pltpu.SMEM),
         chunk(4 * W_BRANCH), chunk(LANES), pl.BlockSpec((batch, 16, CHUNK), lambda c: (0, 0, mix(c))),
         _resident(conv_w), _resident(conv_b),
         chunk(4 * W_BRANCH), _resident(inv_row), _resident(lg_tab), _resident(ret_g),
         chunk(W_BRANCH, 0)]
        + [chunk(W_BRANCH, 1, back) for back in range(SB_FUSED_BLOCKS + 1)]
        + [chunk(W_BRANCH, 2, back) for back in range(SB_FUSED_BLOCKS + 1)]
        + [pl.BlockSpec(memory_space=pl.ANY),
           chunk(2 * W_BRANCH), chunk(W_BRANCH, 1, 1),
           _resident(bias_tab), chunk(LANES),
           pl.BlockSpec((batch, 1, 1, CHUNK), lambda c: (0, mix(c), 0, 0)),
           pl.BlockSpec((batch, 1, 1, CHUNK), lambda c: (0, jnp.maximum(mix(c) - 1, 0), 0, 0)),
           _resident(q_g), _resident(k_g),
           rowblk, _resident(g1)] + [_resident(m) for m in mods[0]] + [_resident(g2)] + [_resident(m) for m in mods[1]]
        + [_layer_of(w, layer) for w in (w_gate, w_up, w_out, w_ff1, w_ff2)])
    rows = batch * CHUNK
    out = pl.pallas_call(
        functools.partial(_layer_kernel, n_chunks=nc),
        grid=(nc + 1,),
        in_specs=in_specs,
        out_specs=rowblk,
        out_shape=jax.ShapeDtypeStruct((batch, seq, d), F32),
        scratch_shapes=[pltpu.VMEM((n_chain, CHUNK + 8, LANES), F32), pltpu.VMEM((n_chain, LANES, LANES), F32),
                        pltpu.VMEM((n_chain, 8, LANES), F32), pltpu.VMEM((batch * 2, LANES, LANES), F32),
                        pltpu.VMEM((n_chain, CHUNK, LANES), BF16), pltpu.VMEM((n_chain, CHUNK, LANES), F32),
                        pltpu.VMEM((n_chain, CHUNK, LANES), F32),
                        pltpu.VMEM((batch, CHUNK, W_BRANCH), BF16), pltpu.VMEM((batch, CHUNK, W_BRANCH), BF16),
                        pltpu.SemaphoreType.DMA((2,)),
                        pltpu.VMEM((2, N_HEADS, rows, W_BRANCH), BF16), pltpu.VMEM((rows, d), BF16),
                        pltpu.VMEM((rows, d), BF16), pltpu.VMEM((rows, d), F32), pltpu.VMEM((rows, d), F32)],
        compiler_params=pltpu.CompilerParams(dimension_semantics=("arbitrary",), vmem_limit_bytes=LAYER_VMEM_LIMIT),
        name="layer",
    )(sinks, ml.reshape(batch, seq, -1), ifc.reshape(batch, seq, -1), ift, conv_w, conv_b,
      ret.reshape(batch, seq, -1), inv_row, lg_tab, ret_g,
      sb3, sb3, sb3, sb3, sb3, sb3, sb3, sb3, sb3, sb3,
      swa3, swa3, bias_tab, pos_col.reshape(batch, seq, LANES), pos_row, pos_row, q_g, k_g,
      x3, g1, *mods[0], g2, *mods[1], w_gate, w_up, w_out, w_ff1, w_ff2)
    return out.reshape(batch * seq, d)


GATE_COL = 3 * W_BRANCH


def _w_in_kernel(wt_ref, perm_ref, gate_ref, ift_ref):
    one_hot = lambda hit: jnp.where(hit, 1.0, 0.0).astype(BF16)
    eye = one_hot(_iota((LANES, LANES), 0) == _iota((LANES, LANES), 1))
    transpose = lambda rows_bf: _dot_nt(eye, rows_bf).astype(BF16)
    cols = lambda lo, n: transpose(wt_ref[lo:lo + n, :].astype(BF16))
    o_out = GATE_COL + 2 * N_HEADS
    o_ret = o_out + W_BRANCH
    o_sb = o_ret + 4 * W_BRANCH
    o_swa = o_sb + 3 * W_BRANCH
    o_gate = o_swa + 2 * W_BRANCH

    perm_ref[:, 0:GATE_COL] = cols(0, GATE_COL)
    perm_ref[:, GATE_COL:SEG_ML[1]] = cols(o_out, W_BRANCH)
    perm_ref[:, SEG_RET[0]:SEG_RET[1]] = cols(o_ret, 4 * W_BRANCH)
    perm_ref[:, SEG_SB[0]:SEG_SB[1]] = cols(o_sb, 3 * W_BRANCH)
    perm_ref[:, SEG_SWA[0]:SEG_SWA[1]] = cols(o_swa, 2 * W_BRANCH)
    gate_ref[...] = cols(o_gate, gate_ref.shape[1])

    block = wt_ref[GATE_COL:GATE_COL + LANES, :].astype(BF16)
    def placement(n_dst):
        dst = _iota((n_dst, LANES), 0)
        j = dst % 4
        src = jnp.where(j < 2, j, j + 2) + 2 * (dst // 4)
        return one_hot((_iota((n_dst, LANES), 1) == src) & (dst < 2 * N_HEADS))
    ift_ref[...] = _dot(placement(16), block).astype(BF16)
    perm_ref[:, SEG_IFC[0]:SEG_IFC[1]] = transpose(_dot(placement(LANES), block).astype(BF16))


def _prep_w_in(w_in):
    depth, d, n_cols = w_in.shape
    tk = LANES
    return pl.pallas_call(
        _w_in_kernel,
        grid=(depth, d // tk),
        in_specs=[pl.BlockSpec((None, n_cols, tk), lambda l, i: (l, 0, i))],
        out_specs=[pl.BlockSpec((None, tk, N_IN), lambda l, i: (l, i, 0)),
                   pl.BlockSpec((None, tk, N_HEADS * d), lambda l, i: (l, i, 0)),
                   pl.BlockSpec((None, 16, tk), lambda l, i: (l, 0, i))],
        out_shape=[jax.ShapeDtypeStruct((depth, d, N_IN), BF16), jax.ShapeDtypeStruct((depth, d, N_HEADS * d), BF16),
                   jax.ShapeDtypeStruct((depth, 16, d), BF16)],
        compiler_params=_params(("parallel", "parallel")),
        name="prep_w_in",
    )(jnp.swapaxes(w_in, 1, 2))


CAST_TILE_BYTES = 2 * 1024 * 1024


def _cast_kernel(x_ref, o_ref):
    o_ref[...] = x_ref[...].astype(BF16)


def _to_bf16(a):
    cols = a.shape[-1]
    rows = a.size // cols
    tm = CAST_TILE_BYTES // (4 * cols)
    spec = pl.BlockSpec((tm, cols), lambda i: (i, 0))
    out = pl.pallas_call(
        _cast_kernel,
        grid=(rows // tm,),
        in_specs=[spec],
        out_specs=spec,
        out_shape=jax.ShapeDtypeStruct((rows, cols), BF16),
        compiler_params=_params(("parallel",)),
        name="cast_bf16",
    )(a.reshape(rows, cols))
    return out.reshape(a.shape)


def _gate_bias_layout(gate_b):
    ib, fb = gate_b[0], gate_b[1]
    order = jnp.concatenate([ib[0:2], fb[0:2], ib[2:4], fb[2:4]])
    row = jnp.concatenate([order, jnp.zeros((LANES - 2 * N_HEADS,), F32)]).reshape(1, LANES)
    col = jnp.concatenate([order, jnp.zeros((16 - 2 * N_HEADS,), F32)])
    return row, jnp.broadcast_to(col[:, None], (16, LANES))


def kernel(x, c, positions, w_ada, b_ada, norm_g, w_in, mlstm_conv_w, mlstm_conv_b, mlstm_gate_b,
           ret_norm_g, swa_q_norm_g, swa_k_norm_g, swa_sinks, rel_bias, w_up, w_out, w_ff1, w_ff2):
    batch, seq, d = x.shape
    depth = w_in.shape[0]
    t = batch * seq
    nb = seq // CHUNK

    c8 = jnp.concatenate([c, jnp.zeros((8 - batch, d), F32)], axis=0)
    mod = _ada(c8, w_ada.reshape(depth * 2, d, 3 * d), b_ada.reshape(depth * 2, 1, 3 * d))
    mod = mod[:, :batch].reshape(depth, 2, batch, 3, 1, d)

    pos_col = jnp.broadcast_to(positions.reshape(t, 1), (t, LANES))
    pos_row = positions.reshape(batch, nb, 1, CHUNK)
    half = HEAD_DIM // 2
    inv = ROPE_BASE ** (-(np.arange(LANES) % half).astype(np.float64) / half)
    inv_row = jnp.asarray(inv, F32).reshape(1, LANES)
    bias_tab = jnp.concatenate([rel_bias.T, jnp.zeros((N_HEADS, LANES - N_BUCKETS), F32)], axis=1)

    log_gamma = np.log(1.0 - np.exp2(-(RET_DECAY_BASE + np.arange(N_HEADS, dtype=np.float64))))
    lg_tab = jnp.asarray(np.broadcast_to(np.repeat(log_gamma, HEAD_DIM).reshape(2, 1, LANES), (2, 8, LANES)), F32)

    w_perm, w_gate, w_ift = _prep_w_in(w_in)
    w_up_bf, w_out_bf, w_ff1_bf, w_ff2_bf = (_to_bf16(w) for w in (w_up, w_out, w_ff1, w_ff2))

    xt = x.reshape(t, d)
    for l in range(depth):
        gb_row, gbt = _gate_bias_layout(mlstm_gate_b[l])
        g1 = norm_g[l, 0].reshape(1, d)
        g2 = norm_g[l, 1].reshape(1, d)
        shift1, scale1, gate1 = mod[l, 0, :, 0], mod[l, 0, :, 1], mod[l, 0, :, 2]
        shift2, scale2, gate2 = mod[l, 1, :, 0], mod[l, 1, :, 1], mod[l, 1, :, 2]

        ml, ifc, ret, sb, swa, ift = _in_proj(xt, g1, scale1, shift1, w_perm, w_ift, gb_row, gbt, l, batch, seq)
        small = (mlstm_conv_w[l], mlstm_conv_b[l].reshape(1, 2 * W_BRANCH), lg_tab, ret_norm_g[l].reshape(1, W_BRANCH),
                 swa_sinks[l], jnp.tile(swa_q_norm_g[l], 2).reshape(1, LANES), jnp.tile(swa_k_norm_g[l], 2).reshape(1, LANES))
        xt = _layer(xt, ((shift1, scale1, gate1), (shift2, scale2, gate2)), g1, g2, ml, ifc, ift, ret, sb, swa,
                    (bias_tab, inv_row, pos_col, pos_row), small, (w_gate, w_up_bf, w_out_bf, w_ff1_bf, w_ff2_bf), l, batch, seq)
    return xt.reshape(batch, seq, d)
```

```python
import functools
import math

import numpy as np
import jax
import jax.numpy as jnp
from jax import lax
from jax.experimental import pallas as pl
from jax.experimental.pallas import tpu as pltpu

F32 = jnp.float32
BF16 = jnp.bfloat16

D_MODEL = 1024
HEAD_DIM = 64
N_HEADS = 4
W_BRANCH = N_HEADS * HEAD_DIM
LANES = 128
CHUNK = 128
CONV_K = 4
D_FF = 4 * D_MODEL
N_BUCKETS = 32
MAX_DIST = 128
ROPE_BASE = 10000.0
RET_DECAY_BASE = 5.0
EPS = 1e-6
NEG = -1e30
F32_EXP_UNDERFLOW = -104.0
VMEM_LIMIT = 56 * 1024 * 1024
LAYER_VMEM_LIMIT = 60 * 1024 * 1024

SEG_ML = (0, 1024)
SEG_IFC = (1024, 1152)
SEG_RET = (1152, 2176)
SEG_SB = (2176, 2944)
SEG_SWA = (2944, 3456)
N_IN = 3456


def _dot(a, b):
    return jnp.dot(a, b, preferred_element_type=F32)


def _dot_nt(a, b):
    return lax.dot_general(a, b, (((1,), (1,)), ((), ())), preferred_element_type=F32)


def _dot_tn(a, b):
    return lax.dot_general(a, b, (((0,), (0,)), ((), ())), preferred_element_type=F32)


def _split2(x):
    hi = x.astype(BF16)
    lo = (x - hi.astype(F32)).astype(BF16)
    return hi, lo


def _split3(x):
    hi = x.astype(BF16)
    r = x - hi.astype(F32)
    mid = r.astype(BF16)
    lo = (r - mid.astype(F32)).astype(BF16)
    return hi, mid, lo


def _iota(shape, axis):
    return lax.broadcasted_iota(jnp.int32, shape, axis)


def _log_sigmoid(x):
    return jnp.minimum(x, 0.0) - jnp.log(1.0 + jnp.exp(-jnp.abs(x)))


def _sigmoid(x):
    return 1.0 / (1.0 + jnp.exp(-x))


def _silu(x):
    return x * _sigmoid(x)


def _norm_mod(x, g, scale, shift):
    ms = jnp.mean(x * x, axis=-1, keepdims=True)
    y = x * lax.rsqrt(ms + EPS)
    return (y * g) * (1.0 + scale) + shift


def _head_mean_sq(x):
    lane = _iota(x.shape, 1)
    sq = x * x
    s0 = jnp.sum(jnp.where(lane < HEAD_DIM, sq, 0.0), axis=-1, keepdims=True)
    s1 = jnp.sum(jnp.where(lane >= HEAD_DIM, sq, 0.0), axis=-1, keepdims=True)
    return jnp.where(lane < HEAD_DIM, s0, s1) * (1.0 / HEAD_DIM)


def _params(sem):
    return pltpu.CompilerParams(dimension_semantics=sem, vmem_limit_bytes=VMEM_LIMIT)


def _interleave(chains, filler=None, every=1):
    def advance(gen):
        try:
            next(gen)
            return True
        except StopIteration:
            return False

    live = list(chains)
    filling = filler is not None
    steps = 0
    while live:
        still = []
        for ch in live:
            if advance(ch):
                still.append(ch)
            steps += 1
            if filling and steps % every == 0:
                filling = advance(filler)
        live = still
    while filling:
        filling = advance(filler)


def _resident(a):
    return pl.BlockSpec(a.shape, lambda *_: (0,) * a.ndim, pipeline_mode=pl.Buffered(1))


def _layer_of(a, layer):
    return pl.BlockSpec((None,) + a.shape[1:], lambda *_: (layer,) + (0,) * (a.ndim - 1),
                        pipeline_mode=pl.Buffered(1))


def _ada_kernel(c_ref, w_ref, b_ref, o_ref):
    c = c_ref[...]
    ch, cl = _split2(_silu(c))
    wh, wl = _split2(w_ref[0])
    o_ref[0] = _dot(ch, wh) + _dot(ch, wl) + _dot(cl, wh) + b_ref[0]


def _ada(c8, w_ada, b_ada):
    n_mod, d, n3 = w_ada.shape
    tn = 1024
    return pl.pallas_call(
        _ada_kernel,
        grid=(n_mod, n3 // tn),
        in_specs=[pl.BlockSpec((8, d), lambda m, n: (0, 0)),
                  pl.BlockSpec((1, d, tn), lambda m, n: (m, 0, n)),
                  pl.BlockSpec((1, 1, tn), lambda m, n: (m, 0, n))],
        out_specs=pl.BlockSpec((1, 8, tn), lambda m, n: (m, 0, n)),
        out_shape=jax.ShapeDtypeStruct((n_mod, 8, n3), F32),
        compiler_params=_params(("parallel", "parallel")),
        name="ada_mod",
    )(c8, w_ada, b_ada)


def _t5_bias(tab_ref, pos_q, pos_prev, pos_cur):
    max_exact = N_BUCKETS // 2
    rel = jnp.concatenate([pos_q - pos_prev, pos_q - pos_cur], axis=1)
    n = jnp.maximum(rel, 0)
    nf = jnp.maximum(n, 1).astype(F32)
    large = max_exact + (jnp.log(nf / max_exact) / math.log(MAX_DIST / max_exact)
                         * (N_BUCKETS - max_exact)).astype(jnp.int32)
    large = jnp.minimum(large, N_BUCKETS - 1)
    bucket = jnp.where(n < max_exact, n, large)
    halves = (bucket[:, :CHUNK], bucket[:, CHUNK:])
    tiles = []
    for h in range(N_HEADS):
        row = jnp.broadcast_to(tab_ref[h:h + 1, :], (CHUNK, LANES))
        tiles.append(jnp.concatenate([jnp.take_along_axis(row, idx, axis=1) for idx in halves], axis=1))
    return tiles


def _in_kernel(x_ref, g_ref, sc_ref, sh_ref, w_ref, wt_ref, gb_ref, gbt_ref,
               ml_ref, ifc_ref, ret_ref, sb_ref, swa_ref, ift_ref):
    h = _norm_mod(x_ref[...], g_ref[...], sc_ref[0], sh_ref[0]).astype(BF16)

    def seg(s):
        return _dot(h, w_ref[:, s[0]:s[1]])

    ml_ref[...] = seg(SEG_ML)
    ifc_ref[...] = seg(SEG_IFC) + gb_ref[...]
    ret_ref[...] = seg(SEG_RET)
    sb_ref[...] = seg(SEG_SB).astype(BF16)
    swa_ref[...] = seg(SEG_SWA)
    ift_ref[0] = _dot_nt(wt_ref[...], h) + gbt_ref[:, 0:1]


def _in_proj(x, g, scale, shift, w_perm, w_ift, gb_row, gbt, layer, batch, seq):
    t, d = x.shape
    tm = 1024
    tpb = seq // tm
    row = lambda n: pl.BlockSpec((tm, n), lambda i: (i, 0))
    mod = pl.BlockSpec((1, 1, d), lambda i: (i // tpb, 0, 0))
    widths = [s[1] - s[0] for s in (SEG_ML, SEG_IFC, SEG_RET, SEG_SB, SEG_SWA)]
    dts = [F32, F32, F32, BF16, F32]
    return pl.pallas_call(
        _in_kernel,
        grid=(t // tm,),
        in_specs=[row(d), _resident(g), mod, mod, _layer_of(w_perm, layer), _layer_of(w_ift, layer), _resident(gb_row),
                  _resident(gbt)],
        out_specs=[row(n) for n in widths] + [pl.BlockSpec((1, 16, tm), lambda i: (i // tpb, 0, i % tpb))],
        out_shape=[jax.ShapeDtypeStruct((t, n), dt) for n, dt in zip(widths, dts)]
        + [jax.ShapeDtypeStruct((batch, 16, seq), F32)],
        compiler_params=_params(("parallel",)),
        name="in_proj",
    )(x, g, scale, shift, w_perm, w_ift, gb_row, gbt)


def _mlstm_chains(ml_ref, ifc_ref, ift_ref, cw_ref, cb_ref, put, halo, c_st, m_st):
    L = CHUNK
    HALO = 8
    batch = ml_ref.shape[0]
    lane = _iota((L, LANES), 1)
    row = _iota((L, LANES), 0)
    causal = row >= lane
    tri = jnp.where(causal, 1.0, 0.0).astype(BF16)
    upper = jnp.where(row <= lane, 1.0, 0.0).astype(BF16)

    def conv_silu(b, col, slot):
        hl = halo.at[slot]
        hl[HALO:HALO + L, :] = ml_ref[b, :, col:col + LANES]
        acc = jnp.zeros((L, LANES), F32) + cb_ref[:, col:col + LANES]
        for j in range(CONV_K):
            off = HALO - (CONV_K - 1) + j
            acc = acc + hl[off:off + L, :] * cw_ref[j:j + 1, col:col + LANES]
        hl[0:HALO, :] = hl[L:L + HALO, :]
        return _silu(acc)

    def pair_chain(b, p):
        pc = p * LANES
        ift = ift_ref[b, 0:8, :]
        lr_h, lr_m, lr_l = _split3(_log_sigmoid(ift))
        a_rows = _dot(lr_h, upper) + _dot(lr_m, upper) + _dot(lr_l, upper)
        ifc = ifc_ref[b]
        gates_b = jnp.concatenate([jnp.broadcast_to(ifc[:, 4 * p + j:4 * p + j + 1], (L, LANES)) for j in range(4)],
                                  axis=1)
        q = conv_silu(b, pc, (b * 2 + p) * 2)
        k = conv_silu(b, W_BRANCH + pc, (b * 2 + p) * 2 + 1)
        v = ml_ref[b, :, 2 * W_BRANCH + pc:2 * W_BRANCH + pc + LANES]
        k_bf = k.astype(BF16)
        yield

        lf_h, lf_m, lf_l = _split3(_log_sigmoid(gates_b[:, 2 * LANES:]))
        a_b = _dot(tri, lf_h) + _dot(tri, lf_m) + _dot(tri, lf_l)
        heads = []
        for hh in range(2):
            ch = (b * 2 + p) * 2 + hh
            hm = (lane >= HEAD_DIM) if hh else (lane < HEAD_DIM)
            qh = (jnp.where(hm, q, 0.0) * (HEAD_DIM ** -0.5)).astype(BF16)
            c_in = c_st[ch]
            heads.append(dict(ch=ch, hm=hm, c_in=c_in, s_qk=_dot_nt(qh, k_bf), q_c=_dot(qh, c_in.astype(BF16))))
        yield

        for hh, hd in enumerate(heads):
            i_col = gates_b[:, hh * LANES:(hh + 1) * LANES]
            a_col = a_b[:, hh * LANES:(hh + 1) * LANES]
            a_last = a_col[L - 1:L, :]
            m_in = m_st[hd["ch"], 0:1, :]
            v_aug = jnp.where(hd["hm"], v, 1.0).astype(BF16)

            r0 = 4 * p + hh
            d_intra = jnp.where(causal, a_col - (a_rows[r0 + 2:r0 + 3, :] - ift[r0:r0 + 1, :]), NEG)
            m_intra = jnp.max(d_intra, axis=-1, keepdims=True)
            m_inter = a_col + m_in
            m_tot = jnp.maximum(m_inter, m_intra)
            pm = jnp.exp(d_intra - m_tot) * hd["s_qk"]
            hd["p_v"] = _dot(pm.astype(BF16), v_aug)
            hd["s_inter"] = jnp.exp(m_inter - m_tot)
            hd["floor"] = jnp.exp(-m_tot)

            g_col = a_last - a_col + i_col
            m_loc = jnp.max(g_col, axis=0, keepdims=True)
            kw = (jnp.where(hd["hm"], k, 0.0) * jnp.exp(g_col - m_loc)).astype(BF16)
            hd["c_loc"] = _dot_tn(kw, v_aug)
            m_new = jnp.maximum(a_last + m_in, m_loc)
            hd["w_old"] = jnp.exp(a_last + m_in - m_new)
            hd["w_new"] = jnp.exp(m_loc - m_new)
            m_st[hd["ch"]] = jnp.zeros((8, LANES), F32) + m_new
        yield

        for hd in heads:
            hd["o_aug"] = hd["s_inter"] * hd["q_c"] + hd["p_v"]
        den = pltpu.roll(jnp.where(lane < HEAD_DIM, heads[1]["o_aug"], heads[0]["o_aug"]), HEAD_DIM, 1)
        num = jnp.where(lane < HEAD_DIM, heads[0]["o_aug"], heads[1]["o_aug"])
        floor = jnp.where(lane < HEAD_DIM, heads[0]["floor"], heads[1]["floor"])
        h_ml = num / jnp.maximum(jnp.abs(den), floor)
        o_gate = ml_ref[b, :, 3 * W_BRANCH + pc:3 * W_BRANCH + pc + LANES]
        put(b, pc, h_ml * _sigmoid(o_gate))
        yield

        for hd in heads:
            c_st[hd["ch"]] = hd["w_old"] * hd["c_in"] + hd["w_new"] * hd["c_loc"]

    return [pair_chain(b, p) for b in range(batch) for p in range(2)]


def _retention_chains(ret_ref, posq_ref, inv_ref, lg_ref, ng_ref, put, st):
    L = CHUNK
    batch = ret_ref.shape[0]
    lane = _iota((L, LANES), 1)
    row = _iota((L, LANES), 0)
    rowf = row.astype(F32)
    relf = jnp.maximum(row - lane, 0).astype(F32)
    first = (lane % HEAD_DIM) < (HEAD_DIM // 2)
    same_head = (row // HEAD_DIM) == (lane // HEAD_DIM)

    def rot(t, cos, sin):
        partner = jnp.where(first, -pltpu.roll(t, LANES - HEAD_DIM // 2, 1), pltpu.roll(t, HEAD_DIM // 2, 1))
        return t * cos + partner * sin

    decays = []
    for p in range(2):
        lg = lg_ref[p, 0:1, :]
        decays.append(dict(
            q=jnp.exp((rowf + 1.0) * lg), k=jnp.exp((L - 1.0 - rowf) * lg), c=jnp.exp(float(L) * lg),
            intra=[jnp.where(row >= lane, jnp.exp(relf * lg[:, hh * HEAD_DIM:hh * HEAD_DIM + 1]), 0.0)
                   for hh in range(2)]))

    tables = {}

    def pair_chain(b, p):
        pc = p * LANES
        dec = decays[p]
        if p == 0:
            ang = posq_ref[b].astype(F32) * inv_ref[...]
            tables[b] = (jnp.cos(ang), jnp.sin(ang))
        cos, sin = tables[b]
        q = rot(ret_ref[b, :, pc:pc + LANES], cos, sin)
        k = rot(ret_ref[b, :, W_BRANCH + pc:W_BRANCH + pc + LANES], cos, sin) * (HEAD_DIM ** -0.5)
        v_bf = ret_ref[b, :, 2 * W_BRANCH + pc:2 * W_BRANCH + pc + LANES].astype(BF16)
        k_bf = k.astype(BF16)
        state = st[b * 2 + p]
        s_qk = [_dot_nt(jnp.where((lane >= HEAD_DIM) if hh else (lane < HEAD_DIM), q, 0.0).astype(BF16), k_bf)
                for hh in range(2)]
        inter = _dot(q.astype(BF16), state.astype(BF16))
        kv = _dot_tn((k * dec["k"]).astype(BF16), v_bf)
        yield

        intra = [_dot((s_qk[hh] * dec["intra"][hh]).astype(BF16), v_bf) for hh in range(2)]
        st[b * 2 + p] = dec["c"] * state + jnp.where(same_head, kv, 0.0)
        yield

        o = jnp.where(lane < HEAD_DIM, intra[0], intra[1]) + inter * dec["q"]
        y = o * lax.rsqrt(_head_mean_sq(o) + EPS) * ng_ref[:, pc:pc + LANES]
        gate = ret_ref[b, :, 3 * W_BRANCH + pc:3 * W_BRANCH + pc + LANES]
        put(b, pc, y * _silu(gate))

    return [pair_chain(b, p) for b in range(batch) for p in range(2)]


SB_FUSED_BLOCKS = 3


def _sb_suffix_op():
    L = CHUNK
    r2 = _iota((2 * L, 2 * L), 0) % L
    c2 = _iota((2 * L, 2 * L), 1)
    return jnp.where((c2 >= L) | (r2 >= c2), 1.0, 0.0).astype(BF16)


def _sb_tile_chain(c, b, p, tiles, fresh, qh_s, carry_s, acc_s, suffix_op):
    L = CHUNK
    pc = p * LANES
    zs = [_dot_nt(qh_s[c], k_ref[b, :, k_col + pc:k_col + pc + LANES]) for k_ref, k_col, _, _, _ in tiles]
    yield
    rrs = []
    for z, (_, _, _, _, mask) in zip(zs, tiles):
        lk = -(jnp.maximum(z, 0.0) + jnp.log(1.0 + jnp.exp(-jnp.abs(z))))
        if mask is not None:
            lk = jnp.where(mask, lk, 0.0)
        hi, lo = _split2(lk)
        rrs.append(_dot(jnp.concatenate([hi, lo], axis=1), suffix_op))
    yield
    carry = None if fresh else carry_s[c]
    parts = []
    for z, rr, (_, _, v_ref, v_col, mask) in zip(zs, rrs, tiles):
        e = z + rr[:, :L]
        w = jnp.exp(e if carry is None else e + carry)
        if mask is not None:
            w = jnp.where(mask, w, 0.0)
        parts.append(_dot(w.astype(BF16), v_ref[b, :, v_col + pc:v_col + pc + LANES]))
        carry = rr[:, L:] if carry is None else carry + rr[:, L:]
    carry_s[c] = carry
    yield
    acc = None if fresh else acc_s[c]
    for part in parts:
        acc = part if acc is None else acc + part
    acc_s[c] = acc


def _sb_chains(i, qkv_refs, qh_s, carry_s, acc_s):
    L = CHUNK
    q_ref = qkv_refs[0]
    batch = q_ref.shape[0]
    lane = _iota((L, LANES), 1)
    row = _iota((L, LANES), 0)
    heads = [(b, p, hh) for b in range(batch) for p in range(2) for hh in range(2)]
    for c, (b, p, hh) in enumerate(heads):
        hm = (lane >= HEAD_DIM) if hh else (lane < HEAD_DIM)
        qh_s[c] = jnp.where(hm, q_ref[b, :, p * LANES:(p + 1) * LANES] * (HEAD_DIM ** -0.5), 0.0).astype(BF16)
    block_no = jnp.zeros((L, LANES), jnp.int32) + i
    masks = [lane < row] + [block_no >= d for d in range(1, SB_FUSED_BLOCKS)]
    tiles = [(qkv_refs[d], W_BRANCH, qkv_refs[d], 2 * W_BRANCH, masks[d]) for d in range(SB_FUSED_BLOCKS)]
    suffix_op = _sb_suffix_op()
    return [_sb_tile_chain(c, b, p, tiles, True, qh_s, carry_s, acc_s, suffix_op) for c, (b, p, hh) in enumerate(heads)]


def _sb_older_blocks(i, sb_hbm, kbuf, vbuf, sem, qh_s, carry_s, acc_s):
    L = CHUNK
    batch = kbuf.shape[0]
    heads = [(b, p, hh) for b in range(batch) for p in range(2) for hh in range(2)]

    def all_underflow():
        return (jnp.max(jnp.max(carry_s[...], axis=0)) <= F32_EXP_UNDERFLOW).astype(jnp.int32)

    def fetch(j, col, buf, slot):
        return pltpu.make_async_copy(
            sb_hbm.at[:, pl.ds(pl.multiple_of(j * L, L), L), pl.ds(col * W_BRANCH, W_BRANCH)], buf, sem.at[slot])

    def cond(st):
        return jnp.logical_and(st[0] >= 0, st[1] == 0)

    def body(st):
        copies = [fetch(st[0], 1, kbuf, 0), fetch(st[0], 2, vbuf, 1)]
        for cp in copies:
            cp.start()
        for cp in copies:
            cp.wait()
        suffix_op = _sb_suffix_op()
        _interleave([_sb_tile_chain(c, b, p, [(kbuf, 0, vbuf, 0, None)], False, qh_s, carry_s, acc_s, suffix_op)
                     for c, (b, p, hh) in enumerate(heads)])
        return (st[0] - 1, all_underflow())

    lax.while_loop(cond, body, (i - SB_FUSED_BLOCKS, all_underflow()))


def _swa_chains(i, sink_ref, cur_ref, prev_ref, tab_ref, posq_ref, posc_ref, posp_ref, qg_ref, kg_ref, put):
    L = CHUNK
    batch = cur_ref.shape[0]
    lane = _iota((L, LANES), 1)
    t = _iota((L, 2 * L), 0)
    j = _iota((L, 2 * L), 1)
    dist = t + L - j
    block_no = jnp.zeros((L, 2 * L), jnp.int32) + i
    valid = (dist >= 0) & (dist < L) & ((j >= L) | (block_no > 0))

    def qk_norm(x, g_ref):
        return x * lax.rsqrt(_head_mean_sq(x) + EPS) * g_ref[...]

    bias = {}

    def both_lanes(pair, g):
        swapped = pltpu.roll(pair, HEAD_DIM, 1)
        return jnp.where(lane < HEAD_DIM, pair, swapped) if g == 0 else jnp.where(lane < HEAD_DIM, swapped, pair)

    def group_chain(b, g):
        sl = slice(g * LANES, (g + 1) * LANES)
        if g == 0:
            bias[b] = _t5_bias(tab_ref, posq_ref[b], posp_ref[b, 0], posc_ref[b, 0])
        qn = qk_norm(cur_ref[b, :, sl], qg_ref) * (HEAD_DIM ** -0.5)
        k_prev = both_lanes(qk_norm(prev_ref[b, :, 0:LANES], kg_ref), g)
        k_cur = both_lanes(qk_norm(cur_ref[b, :, W_BRANCH:W_BRANCH + LANES], kg_ref), g)
        kcat = jnp.concatenate([k_prev, k_cur], axis=0).astype(BF16)
        qk = [_dot_nt(jnp.where((lane >= HEAD_DIM) if r else (lane < HEAD_DIM), qn, 0.0).astype(BF16), kcat)
              for r in range(2)]
        yield
        vcat = jnp.concatenate([both_lanes(prev_ref[b, :, LANES:2 * LANES], g),
                                both_lanes(cur_ref[b, :, W_BRANCH + LANES:W_BRANCH + 2 * LANES], g)], axis=0).astype(BF16)
        outs = []
        for r in range(2):
            hq = 2 * g + r
            logits = jnp.where(valid, qk[r] + bias[b][hq], NEG)
            sink = sink_ref[hq]
            m = jnp.maximum(jnp.max(logits, axis=-1, keepdims=True), sink)
            pr = jnp.exp(logits - m)
            den = jnp.sum(pr, axis=-1, keepdims=True) + jnp.exp(sink - m)
            outs.append(_dot((pr / den).astype(BF16), vcat))
        yield
        put(b, g * LANES, jnp.where(lane < HEAD_DIM, outs[0], outs[1]))

    return [group_chain(b, g) for b in range(batch) for g in range(2)]


DENSE_COLS = 512


def _dense_chain(x_ref, g1_ref, mod1, g2_ref, mod2, branch, wg_ref, wu_ref, wo_ref, w1_ref, w2_ref, out_ref,
                 h_s, merged_s, x1_s, ff_s):
    L = CHUNK
    batch = x_ref.shape[0]
    rows = lambda b: slice(b * L, (b + 1) * L)
    sh1, sc1, gt1 = mod1
    sh2, sc2, gt2 = mod2
    for b in range(batch):
        h_s[rows(b), :] = _norm_mod(x_ref[b], g1_ref[...], sc1[b], sh1[b]).astype(BF16)
    yield
    n_col = D_MODEL // DENSE_COLS
    units = [(j, n) for j in range(n_col) for n in range(N_HEADS)]
    logits = lambda j, n: _dot(h_s[...], wg_ref[:, n * D_MODEL + j * DENSE_COLS:n * D_MODEL + (j + 1) * DENSE_COLS])
    pending = logits(*units[0])
    acc = None
    for u, (j, n) in enumerate(units):
        cs = slice(j * DENSE_COLS, (j + 1) * DENSE_COLS)
        gate_logits = pending
        if u + 1 < len(units):
            pending = logits(*units[u + 1])
        term = _sigmoid(gate_logits) * _dot(branch(n), wu_ref[n, :, cs])
        acc = term if n == 0 else acc + term
        if n == N_HEADS - 1:
            merged_s[:, cs] = acc.astype(BF16)
        yield
    proj = lambda j: _dot(merged_s[...], wo_ref[:, j * DENSE_COLS:(j + 1) * DENSE_COLS])
    pending = proj(0)
    for j in range(n_col):
        cs = slice(j * DENSE_COLS, (j + 1) * DENSE_COLS)
        mix = pending
        if j + 1 < n_col:
            pending = proj(j + 1)
        for b in range(batch):
            x1_s[rows(b), cs] = x_ref[b, :, cs] + gt1[b][:, cs] * mix[rows(b)]
        yield
    for b in range(batch):
        h_s[rows(b), :] = _norm_mod(x1_s[rows(b), :], g2_ref[...], sc2[b], sh2[b]).astype(BF16)
    yield
    n_ff = D_FF // DENSE_COLS
    up = lambda n: _dot(h_s[...], w1_ref[:, n * DENSE_COLS:(n + 1) * DENSE_COLS])
    pending = up(0)
    for n in range(n_ff):
        a = jnp.maximum(pending, 0.0)
        if n + 1 < n_ff:
            pending = up(n + 1)
        part = _dot((a * a).astype(BF16), w2_ref[n * DENSE_COLS:(n + 1) * DENSE_COLS, :])
        ff_s[...] = part if n == 0 else ff_s[...] + part
        yield
    for b in range(batch):
        out_ref[b] = x1_s[rows(b), :] + gt2[b] * ff_s[rows(b), :]


def _layer_kernel(sink_ref, ml_ref, ifc_ref, ift_ref, cw_ref, cb_ref,
                  ret_ref, inv_ref, lg_ref, ng_ref,
                  sb0_ref, sb1_ref, sb2_ref, sb_hbm,
                  swc_ref, swp_ref, tab_ref, posq_ref, posc_ref, posp_ref, qg_ref, kg_ref,
                  x_ref, g1_ref, sh1_ref, sc1_ref, gt1_ref, g2_ref, sh2_ref, sc2_ref, gt2_ref,
                  wg_ref, wu_ref, wo_ref, w1_ref, w2_ref,
                  out_ref,
                  halo, c_st, m_st, ret_st, qh_s, carry_s, acc_s, kbuf, vbuf, sem,
                  branch_s, h_s, merged_s, x1_s, ff_s, *, n_chunks):
    L = CHUNK
    batch = ml_ref.shape[0]
    step = pl.program_id(0)
    i = jnp.minimum(step, n_chunks - 1)
    slot = step % 2

    @pl.when(step == 0)
    def _init():
        halo[:, 0:8, :] = jnp.zeros((halo.shape[0], 8, LANES), F32)
        c_st[...] = jnp.zeros(c_st.shape, F32)
        m_st[...] = jnp.zeros(m_st.shape, F32)
        ret_st[...] = jnp.zeros(ret_st.shape, F32)
        branch_s[1] = jnp.zeros(branch_s.shape[1:], BF16)

    def put(n):
        def write(b, col, value):
            branch_s[slot, n, b * L:(b + 1) * L, col:col + LANES] = value.astype(BF16)
        return write

    chains = (_mlstm_chains(ml_ref, ifc_ref, ift_ref, cw_ref, cb_ref, put(0), halo, c_st, m_st)
              + _retention_chains(ret_ref, posq_ref, inv_ref, lg_ref, ng_ref, put(1), ret_st)
              + _sb_chains(i, (sb0_ref, sb1_ref, sb2_ref), qh_s, carry_s, acc_s)
              + _swa_chains(i, sink_ref, swc_ref, swp_ref, tab_ref, posq_ref, posc_ref, posp_ref, qg_ref, kg_ref,
                            put(3)))
    dense = _dense_chain(x_ref, g1_ref, (sh1_ref, sc1_ref, gt1_ref), g2_ref, (sh2_ref, sc2_ref, gt2_ref),
                         lambda n: branch_s[1 - slot, n], wg_ref, wu_ref, wo_ref, w1_ref, w2_ref, out_ref,
                         h_s, merged_s, x1_s, ff_s)
    _interleave(chains, filler=dense)

    _sb_older_blocks(i, sb_hbm, kbuf, vbuf, sem, qh_s, carry_s, acc_s)
    lane = _iota((L, LANES), 1)
    for b in range(batch):
        for p in range(2):
            c0 = (b * 2 + p) * 2
            put(2)(b, p * LANES, jnp.where(lane < HEAD_DIM, acc_s[c0], acc_s[c0 + 1]))


def _layer(xt, mods, g1, g2, ml, ifc, ift, ret, sb, swa, pos, small, weights, layer, batch, seq):
    d = xt.shape[1]
    nc = seq // CHUNK
    n_chain = batch * N_HEADS
    conv_w, conv_b, lg_tab, ret_g, sinks, q_g, k_g = small
    bias_tab, inv_row, pos_col, pos_row = pos
    w_gate, w_up, w_out, w_ff1, w_ff2 = weights
    mix = lambda c: jnp.minimum(c, nc - 1)
    lag = lambda c: jnp.maximum(c - 1, 0)
    chunk = lambda n, blk=0, back=0: pl.BlockSpec(
        (batch, CHUNK, n), lambda c: (0, jnp.maximum(mix(c) - back, 0), blk))
    sb3 = sb.reshape(batch, seq, -1)
    swa3 = swa.reshape(batch, seq, -1)
    x3 = xt.reshape(batch, seq, d)
    rowblk = pl.BlockSpec((batch, CHUNK, d), lambda c: (0, lag(c), 0))
    in_specs = (
        [pl.BlockSpec(memory_space=pltpu.SMEM),
         chunk(4 * W_BRANCH), chunk(LANES), pl.BlockSpec((batch, 16, CHUNK), lambda c: (0, 0, mix(c))),
         _resident(conv_w), _resident(conv_b),
         chunk(4 * W_BRANCH), _resident(inv_row), _resident(lg_tab), _resident(ret_g),
         ]
        + [chunk(3 * W_BRANCH, 0, back) for back in range(SB_FUSED_BLOCKS)]
        + [pl.BlockSpec(memory_space=pl.ANY),
           chunk(2 * W_BRANCH), chunk(W_BRANCH, 1, 1),
           _resident(bias_tab), chunk(LANES),
           pl.BlockSpec((batch, 1, 1, CHUNK), lambda c: (0, mix(c), 0, 0)),
           pl.BlockSpec((batch, 1, 1, CHUNK), lambda c: (0, jnp.maximum(mix(c) - 1, 0), 0, 0)),
           _resident(q_g), _resident(k_g),
           rowblk, _resident(g1)] + [_resident(m) for m in mods[0]] + [_resident(g2)] + [_resident(m) for m in mods[1]]
        + [_layer_of(w, layer) for w in (w_gate, w_up, w_out, w_ff1, w_ff2)])
    rows = batch * CHUNK
    out = pl.pallas_call(
        functools.partial(_layer_kernel, n_chunks=nc),
        grid=(nc + 1,),
        in_specs=in_specs,
        out_specs=rowblk,
        out_shape=jax.ShapeDtypeStruct((batch, seq, d), F32),
        scratch_shapes=[pltpu.VMEM((n_chain, CHUNK + 8, LANES), F32), pltpu.VMEM((n_chain, LANES, LANES), F32),
                        pltpu.VMEM((n_chain, 8, LANES), F32), pltpu.VMEM((batch * 2, LANES, LANES), F32),
                        pltpu.VMEM((n_chain, CHUNK, LANES), BF16), pltpu.VMEM((n_chain, CHUNK, LANES), F32),
                        pltpu.VMEM((n_chain, CHUNK, LANES), F32),
                        pltpu.VMEM((batch, CHUNK, W_BRANCH), BF16), pltpu.VMEM((batch, CHUNK, W_BRANCH), BF16),
                        pltpu.SemaphoreType.DMA((2,)),
                        pltpu.VMEM((2, N_HEADS, rows, W_BRANCH), BF16), pltpu.VMEM((rows, d), BF16),
                        pltpu.VMEM((rows, d), BF16), pltpu.VMEM((rows, d), F32), pltpu.VMEM((rows, d), F32)],
        compiler_params=pltpu.CompilerParams(dimension_semantics=("arbitrary",), vmem_limit_bytes=LAYER_VMEM_LIMIT),
        name="layer",
    )(sinks, ml.reshape(batch, seq, -1), ifc.reshape(batch, seq, -1), ift, conv_w, conv_b,
      ret.reshape(batch, seq, -1), inv_row, lg_tab, ret_g,
      sb3, sb3, sb3, sb3,
      swa3, swa3, bias_tab, pos_col.reshape(batch, seq, LANES), pos_row, pos_row, q_g, k_g,
      x3, g1, *mods[0], g2, *mods[1], w_gate, w_up, w_out, w_ff1, w_ff2)
    return out.reshape(batch * seq, d)


GATE_COL = 3 * W_BRANCH


def _w_in_kernel(wt_ref, perm_ref, gate_ref, ift_ref):
    one_hot = lambda hit: jnp.where(hit, 1.0, 0.0).astype(BF16)
    eye = one_hot(_iota((LANES, LANES), 0) == _iota((LANES, LANES), 1))
    transpose = lambda rows_bf: _dot_nt(eye, rows_bf).astype(BF16)
    cols = lambda lo, n: transpose(wt_ref[lo:lo + n, :].astype(BF16))
    o_out = GATE_COL + 2 * N_HEADS
    o_ret = o_out + W_BRANCH
    o_sb = o_ret + 4 * W_BRANCH
    o_swa = o_sb + 3 * W_BRANCH
    o_gate = o_swa + 2 * W_BRANCH

    perm_ref[:, 0:GATE_COL] = cols(0, GATE_COL)
    perm_ref[:, GATE_COL:SEG_ML[1]] = cols(o_out, W_BRANCH)
    perm_ref[:, SEG_RET[0]:SEG_RET[1]] = cols(o_ret, 4 * W_BRANCH)
    perm_ref[:, SEG_SB[0]:SEG_SB[1]] = cols(o_sb, 3 * W_BRANCH)
    perm_ref[:, SEG_SWA[0]:SEG_SWA[1]] = cols(o_swa, 2 * W_BRANCH)
    gate_ref[...] = cols(o_gate, gate_ref.shape[1])

    block = wt_ref[GATE_COL:GATE_COL + LANES, :].astype(BF16)
    def placement(n_dst):
        dst = _iota((n_dst, LANES), 0)
        j = dst % 4
        src = jnp.where(j < 2, j, j + 2) + 2 * (dst // 4)
        return one_hot((_iota((n_dst, LANES), 1) == src) & (dst < 2 * N_HEADS))
    ift_ref[...] = _dot(placement(16), block).astype(BF16)
    perm_ref[:, SEG_IFC[0]:SEG_IFC[1]] = transpose(_dot(placement(LANES), block).astype(BF16))


def _prep_w_in(w_in):
    depth, d, n_cols = w_in.shape
    tk = LANES
    return pl.pallas_call(
        _w_in_kernel,
        grid=(depth, d // tk),
        in_specs=[pl.BlockSpec((None, n_cols, tk), lambda l, i: (l, 0, i))],
        out_specs=[pl.BlockSpec((None, tk, N_IN), lambda l, i: (l, i, 0)),
                   pl.BlockSpec((None, tk, N_HEADS * d), lambda l, i: (l, i, 0)),
                   pl.BlockSpec((None, 16, tk), lambda l, i: (l, 0, i))],
        out_shape=[jax.ShapeDtypeStruct((depth, d, N_IN), BF16), jax.ShapeDtypeStruct((depth, d, N_HEADS * d), BF16),
                   jax.ShapeDtypeStruct((depth, 16, d), BF16)],
        compiler_params=_params(("parallel", "parallel")),
        name="prep_w_in",
    )(jnp.swapaxes(w_in, 1, 2))


CAST_TILE_BYTES = 2 * 1024 * 1024


def _cast_kernel(x_ref, o_ref):
    o_ref[...] = x_ref[...].astype(BF16)


def _to_bf16(a):
    cols = a.shape[-1]
    rows = a.size // cols
    tm = CAST_TILE_BYTES // (4 * cols)
    spec = pl.BlockSpec((tm, cols), lambda i: (i, 0))
    out = pl.pallas_call(
        _cast_kernel,
        grid=(rows // tm,),
        in_specs=[spec],
        out_specs=spec,
        out_shape=jax.ShapeDtypeStruct((rows, cols), BF16),
        compiler_params=_params(("parallel",)),
        name="cast_bf16",
    )(a.reshape(rows, cols))
    return out.reshape(a.shape)


def _gate_bias_layout(gate_b):
    ib, fb = gate_b[0], gate_b[1]
    order = jnp.concatenate([ib[0:2], fb[0:2], ib[2:4], fb[2:4]])
    row = jnp.concatenate([order, jnp.zeros((LANES - 2 * N_HEADS,), F32)]).reshape(1, LANES)
    col = jnp.concatenate([order, jnp.zeros((16 - 2 * N_HEADS,), F32)])
    return row, jnp.broadcast_to(col[:, None], (16, LANES))


def kernel(x, c, positions, w_ada, b_ada, norm_g, w_in, mlstm_conv_w, mlstm_conv_b, mlstm_gate_b,
           ret_norm_g, swa_q_norm_g, swa_k_norm_g, swa_sinks, rel_bias, w_up, w_out, w_ff1, w_ff2):
    batch, seq, d = x.shape
    depth = w_in.shape[0]
    t = batch * seq
    nb = seq // CHUNK

    c8 = jnp.concatenate([c, jnp.zeros((8 - batch, d), F32)], axis=0)
    mod = _ada(c8, w_ada.reshape(depth * 2, d, 3 * d), b_ada.reshape(depth * 2, 1, 3 * d))
    mod = mod[:, :batch].reshape(depth, 2, batch, 3, 1, d)

    pos_col = jnp.broadcast_to(positions.reshape(t, 1), (t, LANES))
    pos_row = positions.reshape(batch, nb, 1, CHUNK)
    half = HEAD_DIM // 2
    inv = ROPE_BASE ** (-(np.arange(LANES) % half).astype(np.float64) / half)
    inv_row = jnp.asarray(inv, F32).reshape(1, LANES)
    bias_tab = jnp.concatenate([rel_bias.T, jnp.zeros((N_HEADS, LANES - N_BUCKETS), F32)], axis=1)

    log_gamma = np.log(1.0 - np.exp2(-(RET_DECAY_BASE + np.arange(N_HEADS, dtype=np.float64))))
    lg_tab = jnp.asarray(np.broadcast_to(np.repeat(log_gamma, HEAD_DIM).reshape(2, 1, LANES), (2, 8, LANES)), F32)

    w_perm, w_gate, w_ift = _prep_w_in(w_in)
    w_up_bf, w_out_bf, w_ff1_bf, w_ff2_bf = (_to_bf16(w) for w in (w_up, w_out, w_ff1, w_ff2))

    xt = x.reshape(t, d)
    for l in range(depth):
        gb_row, gbt = _gate_bias_layout(mlstm_gate_b[l])
        g1 = norm_g[l, 0].reshape(1, d)
        g2 = norm_g[l, 1].reshape(1, d)
        shift1, scale1, gate1 = mod[l, 0, :, 0], mod[l, 0, :, 1], mod[l, 0, :, 2]
        shift2, scale2, gate2 = mod[l, 1, :, 0], mod[l, 1, :, 1], mod[l, 1, :, 2]

        ml, ifc, ret, sb, swa, ift = _in_proj(xt, g1, scale1, shift1, w_perm, w_ift, gb_row, gbt, l, batch, seq)
        small = (mlstm_conv_w[l], mlstm_conv_b[l].reshape(1, 2 * W_BRANCH), lg_tab, ret_norm_g[l].reshape(1, W_BRANCH),
                 swa_sinks[l], jnp.tile(swa_q_norm_g[l], 2).reshape(1, LANES), jnp.tile(swa_k_norm_g[l], 2).reshape(1, LANES))
        xt = _layer(xt, ((shift1, scale1, gate1), (shift2, scale2, gate2)), g1, g2, ml, ifc, ift, ret, sb, swa,
                    (bias_tab, inv_row, pos_col, pos_row), small, (w_gate, w_up_bf, w_out_bf, w_ff1_bf, w_ff2_bf), l, batch, seq)
    return xt.reshape(batch, seq, d)
```

```python
import functools
import math

import numpy as np
import jax
import jax.numpy as jnp
from jax import lax
from jax.experimental import pallas as pl
from jax.experimental.pallas import tpu as pltpu

F32 = jnp.float32
BF16 = jnp.bfloat16

D_MODEL = 1024
HEAD_DIM = 64
N_HEADS = 4
W_BRANCH = N_HEADS * HEAD_DIM
LANES = 128
CHUNK = 128
CONV_K = 4
D_FF = 4 * D_MODEL
N_BUCKETS = 32
MAX_DIST = 128
ROPE_BASE = 10000.0
RET_DECAY_BASE = 5.0
EPS = 1e-6
NEG = -1e30
F32_EXP_UNDERFLOW = -104.0
VMEM_LIMIT = 56 * 1024 * 1024
LAYER_VMEM_LIMIT = 60 * 1024 * 1024

SEG_ML = (0, 1024)
SEG_IFC = (1024, 1152)
SEG_RET = (1152, 2176)
SEG_SB = (2176, 2944)
SEG_SWA = (2944, 3456)
N_IN = 3456


def _dot(a, b):
    return jnp.dot(a, b, preferred_element_type=F32)


def _dot_nt(a, b):
    return lax.dot_general(a, b, (((1,), (1,)), ((), ())), preferred_element_type=F32)


def _dot_tn(a, b):
    return lax.dot_general(a, b, (((0,), (0,)), ((), ())), preferred_element_type=F32)


def _split2(x):
    hi = x.astype(BF16)
    lo = (x - hi.astype(F32)).astype(BF16)
    return hi, lo


def _split3(x):
    hi = x.astype(BF16)
    r = x - hi.astype(F32)
    mid = r.astype(BF16)
    lo = (r - mid.astype(F32)).astype(BF16)
    return hi, mid, lo


def _iota(shape, axis):
    return lax.broadcasted_iota(jnp.int32, shape, axis)


def _log_sigmoid(x):
    return jnp.minimum(x, 0.0) - jnp.log(1.0 + jnp.exp(-jnp.abs(x)))


def _sigmoid(x):
    return 1.0 / (1.0 + jnp.exp(-x))


def _silu(x):
    return x * _sigmoid(x)


def _norm_mod(x, g, scale, shift):
    ms = jnp.mean(x * x, axis=-1, keepdims=True)
    y = x * lax.rsqrt(ms + EPS)
    return (y * g) * (1.0 + scale) + shift


def _head_mean_sq(x):
    lane = _iota(x.shape, 1)
    sq = x * x
    s0 = jnp.sum(jnp.where(lane < HEAD_DIM, sq, 0.0), axis=-1, keepdims=True)
    s1 = jnp.sum(jnp.where(lane >= HEAD_DIM, sq, 0.0), axis=-1, keepdims=True)
    return jnp.where(lane < HEAD_DIM, s0, s1) * (1.0 / HEAD_DIM)


def _params(sem):
    return pltpu.CompilerParams(dimension_semantics=sem, vmem_limit_bytes=VMEM_LIMIT)


def _interleave(chains, filler=None, every=1):
    def advance(gen):
        try:
            next(gen)
            return True
        except StopIteration:
            return False

    live = list(chains)
    filling = filler is not None
    steps = 0
    while live:
        still = []
        for ch in live:
            if advance(ch):
                still.append(ch)
            steps += 1
            if filling and steps % every == 0:
                filling = advance(filler)
        live = still
    while filling:
        filling = advance(filler)


def _resident(a):
    return pl.BlockSpec(a.shape, lambda *_: (0,) * a.ndim, pipeline_mode=pl.Buffered(1))


def _layer_of(a, layer):
    return pl.BlockSpec((None,) + a.shape[1:], lambda *_: (layer,) + (0,) * (a.ndim - 1),
                        pipeline_mode=pl.Buffered(1))


def _ada_kernel(c_ref, w_ref, b_ref, o_ref):
    c = c_ref[...]
    ch, cl = _split2(_silu(c))
    wh, wl = _split2(w_ref[0])
    o_ref[0] = _dot(ch, wh) + _dot(ch, wl) + _dot(cl, wh) + b_ref[0]


def _ada(c8, w_ada, b_ada):
    n_mod, d, n3 = w_ada.shape
    tn = 1024
    return pl.pallas_call(
        _ada_kernel,
        grid=(n_mod, n3 // tn),
        in_specs=[pl.BlockSpec((8, d), lambda m, n: (0, 0)),
                  pl.BlockSpec((1, d, tn), lambda m, n: (m, 0, n)),
                  pl.BlockSpec((1, 1, tn), lambda m, n: (m, 0, n))],
        out_specs=pl.BlockSpec((1, 8, tn), lambda m, n: (m, 0, n)),
        out_shape=jax.ShapeDtypeStruct((n_mod, 8, n3), F32),
        compiler_params=_params(("parallel", "parallel")),
        name="ada_mod",
    )(c8, w_ada, b_ada)


def _t5_bias(tab_ref, pos_q, pos_prev, pos_cur):
    max_exact = N_BUCKETS // 2
    rel = jnp.concatenate([pos_q - pos_prev, pos_q - pos_cur], axis=1)
    n = jnp.maximum(rel, 0)
    nf = jnp.maximum(n, 1).astype(F32)
    large = max_exact + (jnp.log(nf / max_exact) / math.log(MAX_DIST / max_exact)
                         * (N_BUCKETS - max_exact)).astype(jnp.int32)
    large = jnp.minimum(large, N_BUCKETS - 1)
    bucket = jnp.where(n < max_exact, n, large)
    halves = (bucket[:, :CHUNK], bucket[:, CHUNK:])
    tiles = []
    for h in range(N_HEADS):
        row = jnp.broadcast_to(tab_ref[h:h + 1, :], (CHUNK, LANES))
        tiles.append(jnp.concatenate([jnp.take_along_axis(row, idx, axis=1) for idx in halves], axis=1))
    return tiles


def _in_kernel(x_ref, g_ref, sc_ref, sh_ref, w_ref, gb_ref, ml_ref, ifc_ref, ret_ref, sb_ref, swa_ref, ift_ref):
    h = _norm_mod(x_ref[...], g_ref[...], sc_ref[0], sh_ref[0]).astype(BF16)

    def seg(s):
        return _dot(h, w_ref[:, s[0]:s[1]])

    ml_ref[...] = seg(SEG_ML)
    gates = seg(SEG_IFC) + gb_ref[...]
    ifc_ref[...] = gates
    ift_ref[0] = gates.T[0:16, :]
    ret_ref[...] = seg(SEG_RET)
    sb_ref[...] = seg(SEG_SB).astype(BF16)
    swa_ref[...] = seg(SEG_SWA)


def _in_proj(x, g, scale, shift, w_perm, gb_row, layer, batch, seq):
    t, d = x.shape
    tm = 1024
    tpb = seq // tm
    row = lambda n: pl.BlockSpec((tm, n), lambda i: (i, 0))
    mod = pl.BlockSpec((1, 1, d), lambda i: (i // tpb, 0, 0))
    widths = [s[1] - s[0] for s in (SEG_ML, SEG_IFC, SEG_RET, SEG_SB, SEG_SWA)]
    dts = [F32, F32, F32, BF16, F32]
    return pl.pallas_call(
        _in_kernel,
        grid=(t // tm,),
        in_specs=[row(d), _resident(g), mod, mod, _layer_of(w_perm, layer), _resident(gb_row)],
        out_specs=[row(n) for n in widths] + [pl.BlockSpec((1, 16, tm), lambda i: (i // tpb, 0, i % tpb))],
        out_shape=[jax.ShapeDtypeStruct((t, n), dt) for n, dt in zip(widths, dts)]
        + [jax.ShapeDtypeStruct((batch, 16, seq), F32)],
        compiler_params=_params(("parallel",)),
        name="in_proj",
    )(x, g, scale, shift, w_perm, gb_row)


def _mlstm_chains(ml_ref, ifc_ref, ift_ref, cw_ref, cb_ref, put, halo, c_st, m_st):
    L = CHUNK
    HALO = 8
    batch = ml_ref.shape[0]
    lane = _iota((L, LANES), 1)
    row = _iota((L, LANES), 0)
    causal = row >= lane
    tri = jnp.where(causal, 1.0, 0.0).astype(BF16)
    upper = jnp.where(row <= lane, 1.0, 0.0).astype(BF16)

    def conv_silu(b, col, slot):
        hl = halo.at[slot]
        hl[HALO:HALO + L, :] = ml_ref[b, :, col:col + LANES]
        acc = jnp.zeros((L, LANES), F32) + cb_ref[:, col:col + LANES]
        for j in range(CONV_K):
            off = HALO - (CONV_K - 1) + j
            acc = acc + hl[off:off + L, :] * cw_ref[j:j + 1, col:col + LANES]
        hl[0:HALO, :] = hl[L:L + HALO, :]
        return _silu(acc)

    def pair_chain(b, p):
        pc = p * LANES
        ift = ift_ref[b, 0:8, :]
        lr_h, lr_m, lr_l = _split3(_log_sigmoid(ift))
        a_rows = _dot(lr_h, upper) + _dot(lr_m, upper) + _dot(lr_l, upper)
        ifc = ifc_ref[b]
        gates_b = jnp.concatenate([jnp.broadcast_to(ifc[:, 4 * p + j:4 * p + j + 1], (L, LANES)) for j in range(4)],
                                  axis=1)
        q = conv_silu(b, pc, (b * 2 + p) * 2)
        k = conv_silu(b, W_BRANCH + pc, (b * 2 + p) * 2 + 1)
        v = ml_ref[b, :, 2 * W_BRANCH + pc:2 * W_BRANCH + pc + LANES]
        k_bf = k.astype(BF16)
        yield

        lf_h, lf_m, lf_l = _split3(_log_sigmoid(gates_b[:, 2 * LANES:]))
        a_b = _dot(tri, lf_h) + _dot(tri, lf_m) + _dot(tri, lf_l)
        heads = []
        for hh in range(2):
            ch = (b * 2 + p) * 2 + hh
            hm = (lane >= HEAD_DIM) if hh else (lane < HEAD_DIM)
            qh = (jnp.where(hm, q, 0.0) * (HEAD_DIM ** -0.5)).astype(BF16)
            c_in = c_st[ch]
            heads.append(dict(ch=ch, hm=hm, c_in=c_in, s_qk=_dot_nt(qh, k_bf), q_c=_dot(qh, c_in.astype(BF16))))
        yield

        for hh, hd in enumerate(heads):
            i_col = gates_b[:, hh * LANES:(hh + 1) * LANES]
            a_col = a_b[:, hh * LANES:(hh + 1) * LANES]
            a_last = a_col[L - 1:L, :]
            m_in = m_st[hd["ch"], 0:1, :]
            v_aug = jnp.where(hd["hm"], v, 1.0).astype(BF16)

            r0 = 4 * p + hh
            d_intra = jnp.where(causal, a_col - (a_rows[r0 + 2:r0 + 3, :] - ift[r0:r0 + 1, :]), NEG)
            m_intra = jnp.max(d_intra, axis=-1, keepdims=True)
            m_inter = a_col + m_in
            m_tot = jnp.maximum(m_inter, m_intra)
            pm = jnp.exp(d_intra - m_tot) * hd["s_qk"]
            hd["p_v"] = _dot(pm.astype(BF16), v_aug)
            hd["s_inter"] = jnp.exp(m_inter - m_tot)
            hd["floor"] = jnp.exp(-m_tot)

            g_col = a_last - a_col + i_col
            m_loc = jnp.max(g_col, axis=0, keepdims=True)
            kw = (jnp.where(hd["hm"], k, 0.0) * jnp.exp(g_col - m_loc)).astype(BF16)
            hd["c_loc"] = _dot_tn(kw, v_aug)
            m_new = jnp.maximum(a_last + m_in, m_loc)
            hd["w_old"] = jnp.exp(a_last + m_in - m_new)
            hd["w_new"] = jnp.exp(m_loc - m_new)
            m_st[hd["ch"]] = jnp.zeros((8, LANES), F32) + m_new
        yield

        for hd in heads:
            hd["o_aug"] = hd["s_inter"] * hd["q_c"] + hd["p_v"]
        den = pltpu.roll(jnp.where(lane < HEAD_DIM, heads[1]["o_aug"], heads[0]["o_aug"]), HEAD_DIM, 1)
        num = jnp.where(lane < HEAD_DIM, heads[0]["o_aug"], heads[1]["o_aug"])
        floor = jnp.where(lane < HEAD_DIM, heads[0]["floor"], heads[1]["floor"])
        h_ml = num / jnp.maximum(jnp.abs(den), floor)
        o_gate = ml_ref[b, :, 3 * W_BRANCH + pc:3 * W_BRANCH + pc + LANES]
        put(b, pc, h_ml * _sigmoid(o_gate))
        yield

        for hd in heads:
            c_st[hd["ch"]] = hd["w_old"] * hd["c_in"] + hd["w_new"] * hd["c_loc"]

    return [pair_chain(b, p) for b in range(batch) for p in range(2)]


def _retention_chains(ret_ref, posq_ref, inv_ref, lg_ref, ng_ref, put, st):
    L = CHUNK
    batch = ret_ref.shape[0]
    lane = _iota((L, LANES), 1)
    row = _iota((L, LANES), 0)
    rowf = row.astype(F32)
    relf = jnp.maximum(row - lane, 0).astype(F32)
    first = (lane % HEAD_DIM) < (HEAD_DIM // 2)
    same_head = (row // HEAD_DIM) == (lane // HEAD_DIM)

    def rot(t, cos, sin):
        partner = jnp.where(first, -pltpu.roll(t, LANES - HEAD_DIM // 2, 1), pltpu.roll(t, HEAD_DIM // 2, 1))
        return t * cos + partner * sin

    decays = []
    for p in range(2):
        lg = lg_ref[p, 0:1, :]
        decays.append(dict(
            q=jnp.exp((rowf + 1.0) * lg), k=jnp.exp((L - 1.0 - rowf) * lg), c=jnp.exp(float(L) * lg),
            intra=[jnp.where(row >= lane, jnp.exp(relf * lg[:, hh * HEAD_DIM:hh * HEAD_DIM + 1]), 0.0)
                   for hh in range(2)]))

    tables = {}

    def pair_chain(b, p):
        pc = p * LANES
        dec = decays[p]
        if p == 0:
            ang = posq_ref[b].astype(F32) * inv_ref[...]
            tables[b] = (jnp.cos(ang), jnp.sin(ang))
        cos, sin = tables[b]
        q = rot(ret_ref[b, :, pc:pc + LANES], cos, sin)
        k = rot(ret_ref[b, :, W_BRANCH + pc:W_BRANCH + pc + LANES], cos, sin) * (HEAD_DIM ** -0.5)
        v_bf = ret_ref[b, :, 2 * W_BRANCH + pc:2 * W_BRANCH + pc + LANES].astype(BF16)
        k_bf = k.astype(BF16)
        state = st[b * 2 + p]
        s_qk = [_dot_nt(jnp.where((lane >= HEAD_DIM) if hh else (lane < HEAD_DIM), q, 0.0).astype(BF16), k_bf)
                for hh in range(2)]
        inter = _dot(q.astype(BF16), state.astype(BF16))
        kv = _dot_tn((k * dec["k"]).astype(BF16), v_bf)
        yield

        intra = [_dot((s_qk[hh] * dec["intra"][hh]).astype(BF16), v_bf) for hh in range(2)]
        st[b * 2 + p] = dec["c"] * state + jnp.where(same_head, kv, 0.0)
        yield

        o = jnp.where(lane < HEAD_DIM, intra[0], intra[1]) + inter * dec["q"]
        y = o * lax.rsqrt(_head_mean_sq(o) + EPS) * ng_ref[:, pc:pc + LANES]
        gate = ret_ref[b, :, 3 * W_BRANCH + pc:3 * W_BRANCH + pc + LANES]
        put(b, pc, y * _silu(gate))

    return [pair_chain(b, p) for b in range(batch) for p in range(2)]


SB_FUSED_BLOCKS = 3


def _sb_suffix_op():
    L = CHUNK
    r2 = _iota((2 * L, 2 * L), 0) % L
    c2 = _iota((2 * L, 2 * L), 1)
    return jnp.where((c2 >= L) | (r2 >= c2), 1.0, 0.0).astype(BF16)


def _sb_tile_chain(c, b, p, tiles, fresh, qh_s, carry_s, acc_s, suffix_op):
    L = CHUNK
    pc = p * LANES
    zs = [_dot_nt(qh_s[c], k_ref[b, :, k_col + pc:k_col + pc + LANES]) for k_ref, k_col, _, _, _ in tiles]
    yield
    rrs = []
    for z, (_, _, _, _, mask) in zip(zs, tiles):
        lk = -(jnp.maximum(z, 0.0) + jnp.log(1.0 + jnp.exp(-jnp.abs(z))))
        if mask is not None:
            lk = jnp.where(mask, lk, 0.0)
        hi, lo = _split2(lk)
        rrs.append(_dot(jnp.concatenate([hi, lo], axis=1), suffix_op))
    yield
    carry = None if fresh else carry_s[c]
    parts = []
    for z, rr, (_, _, v_ref, v_col, mask) in zip(zs, rrs, tiles):
        e = z + rr[:, :L]
        w = jnp.exp(e if carry is None else e + carry)
        if mask is not None:
            w = jnp.where(mask, w, 0.0)
        parts.append(_dot(w.astype(BF16), v_ref[b, :, v_col + pc:v_col + pc + LANES]))
        carry = rr[:, L:] if carry is None else carry + rr[:, L:]
    carry_s[c] = carry
    yield
    acc = None if fresh else acc_s[c]
    for part in parts:
        acc = part if acc is None else acc + part
    acc_s[c] = acc


def _sb_chains(i, qkv_refs, qh_s, carry_s, acc_s):
    L = CHUNK
    q_ref = qkv_refs[0]
    batch = q_ref.shape[0]
    lane = _iota((L, LANES), 1)
    row = _iota((L, LANES), 0)
    heads = [(b, p, hh) for b in range(batch) for p in range(2) for hh in range(2)]
    for c, (b, p, hh) in enumerate(heads):
        hm = (lane >= HEAD_DIM) if hh else (lane < HEAD_DIM)
        qh_s[c] = jnp.where(hm, q_ref[b, :, p * LANES:(p + 1) * LANES] * (HEAD_DIM ** -0.5), 0.0).astype(BF16)
    block_no = jnp.zeros((L, LANES), jnp.int32) + i
    masks = [lane < row] + [block_no >= d for d in range(1, SB_FUSED_BLOCKS)]
    tiles = [(qkv_refs[d], W_BRANCH, qkv_refs[d], 2 * W_BRANCH, masks[d]) for d in range(SB_FUSED_BLOCKS)]
    suffix_op = _sb_suffix_op()
    return [_sb_tile_chain(c, b, p, tiles, True, qh_s, carry_s, acc_s, suffix_op) for c, (b, p, hh) in enumerate(heads)]


def _sb_older_blocks(i, sb_hbm, kbuf, vbuf, sem, qh_s, carry_s, acc_s):
    L = CHUNK
    batch = kbuf.shape[0]
    heads = [(b, p, hh) for b in range(batch) for p in range(2) for hh in range(2)]

    def all_underflow():
        return (jnp.max(jnp.max(carry_s[...], axis=0)) <= F32_EXP_UNDERFLOW).astype(jnp.int32)

    def fetch(j, col, buf, slot):
        return pltpu.make_async_copy(
            sb_hbm.at[:, pl.ds(pl.multiple_of(j * L, L), L), pl.ds(col * W_BRANCH, W_BRANCH)], buf, sem.at[slot])

    def cond(st):
        return jnp.logical_and(st[0] >= 0, st[1] == 0)

    def body(st):
        copies = [fetch(st[0], 1, kbuf, 0), fetch(st[0], 2, vbuf, 1)]
        for cp in copies:
            cp.start()
        for cp in copies:
            cp.wait()
        suffix_op = _sb_suffix_op()
        _interleave([_sb_tile_chain(c, b, p, [(kbuf, 0, vbuf, 0, None)], False, qh_s, carry_s, acc_s, suffix_op)
                     for c, (b, p, hh) in enumerate(heads)])
        return (st[0] - 1, all_underflow())

    lax.while_loop(cond, body, (i - SB_FUSED_BLOCKS, all_underflow()))


def _swa_chains(i, sink_ref, cur_ref, prev_ref, tab_ref, posq_ref, posc_ref, posp_ref, qg_ref, kg_ref, put):
    L = CHUNK
    batch = cur_ref.shape[0]
    lane = _iota((L, LANES), 1)
    t = _iota((L, 2 * L), 0)
    j = _iota((L, 2 * L), 1)
    dist = t + L - j
    block_no = jnp.zeros((L, 2 * L), jnp.int32) + i
    valid = (dist >= 0) & (dist < L) & ((j >= L) | (block_no > 0))

    def qk_norm(x, g_ref):
        return x * lax.rsqrt(_head_mean_sq(x) + EPS) * g_ref[...]

    bias = {}

    def both_lanes(pair, g):
        swapped = pltpu.roll(pair, HEAD_DIM, 1)
        return jnp.where(lane < HEAD_DIM, pair, swapped) if g == 0 else jnp.where(lane < HEAD_DIM, swapped, pair)

    def group_chain(b, g):
        sl = slice(g * LANES, (g + 1) * LANES)
        if g == 0:
            bias[b] = _t5_bias(tab_ref, posq_ref[b], posp_ref[b, 0], posc_ref[b, 0])
        qn = qk_norm(cur_ref[b, :, sl], qg_ref) * (HEAD_DIM ** -0.5)
        k_prev = both_lanes(qk_norm(prev_ref[b, :, 0:LANES], kg_ref), g)
        k_cur = both_lanes(qk_norm(cur_ref[b, :, W_BRANCH:W_BRANCH + LANES], kg_ref), g)
        kcat = jnp.concatenate([k_prev, k_cur], axis=0).astype(BF16)
        qk = [_dot_nt(jnp.where((lane >= HEAD_DIM) if r else (lane < HEAD_DIM), qn, 0.0).astype(BF16), kcat)
              for r in range(2)]
        yield
        vcat = jnp.concatenate([both_lanes(prev_ref[b, :, LANES:2 * LANES], g),
                                both_lanes(cur_ref[b, :, W_BRANCH + LANES:W_BRANCH + 2 * LANES], g)], axis=0).astype(BF16)
        outs = []
        for r in range(2):
            hq = 2 * g + r
            logits = jnp.where(valid, qk[r] + bias[b][hq], NEG)
            sink = sink_ref[hq]
            m = jnp.maximum(jnp.max(logits, axis=-1, keepdims=True), sink)
            pr = jnp.exp(logits - m)
            den = jnp.sum(pr, axis=-1, keepdims=True) + jnp.exp(sink - m)
            outs.append(_dot((pr / den).astype(BF16), vcat))
        yield
        put(b, g * LANES, jnp.where(lane < HEAD_DIM, outs[0], outs[1]))

    return [group_chain(b, g) for b in range(batch) for g in range(2)]


DENSE_COLS = 512


def _dense_chain(x_ref, g1_ref, mod1, g2_ref, mod2, branch, wg_ref, wu_ref, wo_ref, w1_ref, w2_ref, out_ref,
                 h_s, merged_s, x1_s, ff_s):
    L = CHUNK
    batch = x_ref.shape[0]
    rows = lambda b: slice(b * L, (b + 1) * L)
    sh1, sc1, gt1 = mod1
    sh2, sc2, gt2 = mod2
    for b in range(batch):
        h_s[rows(b), :] = _norm_mod(x_ref[b], g1_ref[...], sc1[b], sh1[b]).astype(BF16)
    yield
    n_col = D_MODEL // DENSE_COLS
    units = [(j, n) for j in range(n_col) for n in range(N_HEADS)]
    logits = lambda j, n: _dot(h_s[...], wg_ref[:, n * D_MODEL + j * DENSE_COLS:n * D_MODEL + (j + 1) * DENSE_COLS])
    pending = logits(*units[0])
    acc = None
    for u, (j, n) in enumerate(units):
        cs = slice(j * DENSE_COLS, (j + 1) * DENSE_COLS)
        gate_logits = pending
        if u + 1 < len(units):
            pending = logits(*units[u + 1])
        term = _sigmoid(gate_logits) * _dot(branch(n), wu_ref[n, :, cs])
        acc = term if n == 0 else acc + term
        if n == N_HEADS - 1:
            merged_s[:, cs] = acc.astype(BF16)
        yield
    proj = lambda j: _dot(merged_s[...], wo_ref[:, j * DENSE_COLS:(j + 1) * DENSE_COLS])
    pending = proj(0)
    for j in range(n_col):
        cs = slice(j * DENSE_COLS, (j + 1) * DENSE_COLS)
        mix = pending
        if j + 1 < n_col:
            pending = proj(j + 1)
        for b in range(batch):
            x1_s[rows(b), cs] = x_ref[b, :, cs] + gt1[b][:, cs] * mix[rows(b)]
        yield
    for b in range(batch):
        h_s[rows(b), :] = _norm_mod(x1_s[rows(b), :], g2_ref[...], sc2[b], sh2[b]).astype(BF16)
    yield
    n_ff = D_FF // DENSE_COLS
    up = lambda n: _dot(h_s[...], w1_ref[:, n * DENSE_COLS:(n + 1) * DENSE_COLS])
    pending = up(0)
    for n in range(n_ff):
        a = jnp.maximum(pending, 0.0)
        if n + 1 < n_ff:
            pending = up(n + 1)
        part = _dot((a * a).astype(BF16), w2_ref[n * DENSE_COLS:(n + 1) * DENSE_COLS, :])
        ff_s[...] = part if n == 0 else ff_s[...] + part
        yield
    for b in range(batch):
        out_ref[b] = x1_s[rows(b), :] + gt2[b] * ff_s[rows(b), :]


def _layer_kernel(sink_ref, ml_ref, ifc_ref, ift_ref, cw_ref, cb_ref,
                  ret_ref, inv_ref, lg_ref, ng_ref,
                  sb0_ref, sb1_ref, sb2_ref, sb_hbm,
                  swc_ref, swp_ref, tab_ref, posq_ref, posc_ref, posp_ref, qg_ref, kg_ref,
                  x_ref, g1_ref, sh1_ref, sc1_ref, gt1_ref, g2_ref, sh2_ref, sc2_ref, gt2_ref,
                  wg_ref, wu_ref, wo_ref, w1_ref, w2_ref,
                  out_ref,
                  halo, c_st, m_st, ret_st, qh_s, carry_s, acc_s, kbuf, vbuf, sem,
                  branch_s, h_s, merged_s, x1_s, ff_s, *, n_chunks):
    L = CHUNK
    batch = ml_ref.shape[0]
    step = pl.program_id(0)
    i = jnp.minimum(step, n_chunks - 1)
    slot = step % 2

    @pl.when(step == 0)
    def _init():
        halo[:, 0:8, :] = jnp.zeros((halo.shape[0], 8, LANES), F32)
        c_st[...] = jnp.zeros(c_st.shape, F32)
        m_st[...] = jnp.zeros(m_st.shape, F32)
        ret_st[...] = jnp.zeros(ret_st.shape, F32)
        branch_s[1] = jnp.zeros(branch_s.shape[1:], BF16)

    def put(n):
        def write(b, col, value):
            branch_s[slot, n, b * L:(b + 1) * L, col:col + LANES] = value.astype(BF16)
        return write

    chains = (_mlstm_chains(ml_ref, ifc_ref, ift_ref, cw_ref, cb_ref, put(0), halo, c_st, m_st)
              + _retention_chains(ret_ref, posq_ref, inv_ref, lg_ref, ng_ref, put(1), ret_st)
              + _sb_chains(i, (sb0_ref, sb1_ref, sb2_ref), qh_s, carry_s, acc_s)
              + _swa_chains(i, sink_ref, swc_ref, swp_ref, tab_ref, posq_ref, posc_ref, posp_ref, qg_ref, kg_ref,
                            put(3)))
    dense = _dense_chain(x_ref, g1_ref, (sh1_ref, sc1_ref, gt1_ref), g2_ref, (sh2_ref, sc2_ref, gt2_ref),
                         lambda n: branch_s[1 - slot, n], wg_ref, wu_ref, wo_ref, w1_ref, w2_ref, out_ref,
                         h_s, merged_s, x1_s, ff_s)
    _interleave(chains, filler=dense)

    _sb_older_blocks(i, sb_hbm, kbuf, vbuf, sem, qh_s, carry_s, acc_s)
    lane = _iota((L, LANES), 1)
    for b in range(batch):
        for p in range(2):
            c0 = (b * 2 + p) * 2
            put(2)(b, p * LANES, jnp.where(lane < HEAD_DIM, acc_s[c0], acc_s[c0 + 1]))


def _layer(xt, mods, g1, g2, ml, ifc, ift, ret, sb, swa, pos, small, weights, layer, batch, seq):
    d = xt.shape[1]
    nc = seq // CHUNK
    n_chain = batch * N_HEADS
    conv_w, conv_b, lg_tab, ret_g, sinks, q_g, k_g = small
    bias_tab, inv_row, pos_col, pos_row = pos
    w_gate, w_up, w_out, w_ff1, w_ff2 = weights
    mix = lambda c: jnp.minimum(c, nc - 1)
    lag = lambda c: jnp.maximum(c - 1, 0)
    chunk = lambda n, blk=0, back=0: pl.BlockSpec(
        (batch, CHUNK, n), lambda c: (0, jnp.maximum(mix(c) - back, 0), blk))
    sb3 = sb.reshape(batch, seq, -1)
    swa3 = swa.reshape(batch, seq, -1)
    x3 = xt.reshape(batch, seq, d)
    rowblk = pl.BlockSpec((batch, CHUNK, d), lambda c: (0, lag(c), 0))
    in_specs = (
        [pl.BlockSpec(memory_space=pltpu.SMEM),
         chunk(4 * W_BRANCH), chunk(LANES), pl.BlockSpec((batch, 16, CHUNK), lambda c: (0, 0, mix(c))),
         _resident(conv_w), _resident(conv_b),
         chunk(4 * W_BRANCH), _resident(inv_row), _resident(lg_tab), _resident(ret_g),
         ]
        + [chunk(3 * W_BRANCH, 0, back) for back in range(SB_FUSED_BLOCKS)]
        + [pl.BlockSpec(memory_space=pl.ANY),
           chunk(2 * W_BRANCH), chunk(W_BRANCH, 1, 1),
           _resident(bias_tab), chunk(LANES),
           pl.BlockSpec((batch, 1, 1, CHUNK), lambda c: (0, mix(c), 0, 0)),
           pl.BlockSpec((batch, 1, 1, CHUNK), lambda c: (0, jnp.maximum(mix(c) - 1, 0), 0, 0)),
           _resident(q_g), _resident(k_g),
           rowblk, _resident(g1)] + [_resident(m) for m in mods[0]] + [_resident(g2)] + [_resident(m) for m in mods[1]]
        + [_layer_of(w, layer) for w in (w_gate, w_up, w_out, w_ff1, w_ff2)])
    rows = batch * CHUNK
    out = pl.pallas_call(
        functools.partial(_layer_kernel, n_chunks=nc),
        grid=(nc + 1,),
        in_specs=in_specs,
        out_specs=rowblk,
        out_shape=jax.ShapeDtypeStruct((batch, seq, d), F32),
        scratch_shapes=[pltpu.VMEM((n_chain, CHUNK + 8, LANES), F32), pltpu.VMEM((n_chain, LANES, LANES), F32),
                        pltpu.VMEM((n_chain, 8, LANES), F32), pltpu.VMEM((batch * 2, LANES, LANES), F32),
                        pltpu.VMEM((n_chain, CHUNK, LANES), BF16), pltpu.VMEM((n_chain, CHUNK, LANES), F32),
                        pltpu.VMEM((n_chain, CHUNK, LANES), F32),
                        pltpu.VMEM((batch, CHUNK, W_BRANCH), BF16), pltpu.VMEM((batch, CHUNK, W_BRANCH), BF16),
                        pltpu.SemaphoreType.DMA((2,)),
                        pltpu.VMEM((2, N_HEADS, rows, W_BRANCH), BF16), pltpu.VMEM((rows, d), BF16),
                        pltpu.VMEM((rows, d), BF16), pltpu.VMEM((rows, d), F32), pltpu.VMEM((rows, d), F32)],
        compiler_params=pltpu.CompilerParams(dimension_semantics=("arbitrary",), vmem_limit_bytes=LAYER_VMEM_LIMIT),
        name="layer",
    )(sinks, ml.reshape(batch, seq, -1), ifc.reshape(batch, seq, -1), ift, conv_w, conv_b,
      ret.reshape(batch, seq, -1), inv_row, lg_tab, ret_g,
      sb3, sb3, sb3, sb3,
      swa3, swa3, bias_tab, pos_col.reshape(batch, seq, LANES), pos_row, pos_row, q_g, k_g,
      x3, g1, *mods[0], g2, *mods[1], w_gate, w_up, w_out, w_ff1, w_ff2)
    return out.reshape(batch * seq, d)


GATE_COL = 3 * W_BRANCH


def _w_in_kernel(wt_ref, perm_ref, gate_ref):
    one_hot = lambda hit: jnp.where(hit, 1.0, 0.0).astype(BF16)
    eye = one_hot(_iota((LANES, LANES), 0) == _iota((LANES, LANES), 1))
    transpose = lambda rows_bf: _dot_nt(eye, rows_bf).astype(BF16)
    cols = lambda lo, n: transpose(wt_ref[lo:lo + n, :].astype(BF16))
    o_out = GATE_COL + 2 * N_HEADS
    o_ret = o_out + W_BRANCH
    o_sb = o_ret + 4 * W_BRANCH
    o_swa = o_sb + 3 * W_BRANCH
    o_gate = o_swa + 2 * W_BRANCH

    perm_ref[:, 0:GATE_COL] = cols(0, GATE_COL)
    perm_ref[:, GATE_COL:SEG_ML[1]] = cols(o_out, W_BRANCH)
    perm_ref[:, SEG_RET[0]:SEG_RET[1]] = cols(o_ret, 4 * W_BRANCH)
    perm_ref[:, SEG_SB[0]:SEG_SB[1]] = cols(o_sb, 3 * W_BRANCH)
    perm_ref[:, SEG_SWA[0]:SEG_SWA[1]] = cols(o_swa, 2 * W_BRANCH)
    gate_ref[...] = cols(o_gate, gate_ref.shape[1])

    block = wt_ref[GATE_COL:GATE_COL + LANES, :].astype(BF16)
    dst = _iota((LANES, LANES), 0)
    j = dst % 4
    src = jnp.where(j < 2, j, j + 2) + 2 * (dst // 4)
    placement = one_hot((_iota((LANES, LANES), 1) == src) & (dst < 2 * N_HEADS))
    perm_ref[:, SEG_IFC[0]:SEG_IFC[1]] = transpose(_dot(placement, block).astype(BF16))


def _prep_w_in(w_in):
    depth, d, n_cols = w_in.shape
    tk = LANES
    return pl.pallas_call(
        _w_in_kernel,
        grid=(depth, d // tk),
        in_specs=[pl.BlockSpec((None, n_cols, tk), lambda l, i: (l, 0, i))],
        out_specs=[pl.BlockSpec((None, tk, N_IN), lambda l, i: (l, i, 0)),
                   pl.BlockSpec((None, tk, N_HEADS * d), lambda l, i: (l, i, 0))],
        out_shape=[jax.ShapeDtypeStruct((depth, d, N_IN), BF16), jax.ShapeDtypeStruct((depth, d, N_HEADS * d), BF16)],
        compiler_params=_params(("parallel", "parallel")),
        name="prep_w_in",
    )(jnp.swapaxes(w_in, 1, 2))


CAST_TILE_BYTES = 2 * 1024 * 1024


def _cast_kernel(x_ref, o_ref):
    o_ref[...] = x_ref[...].astype(BF16)


def _to_bf16(a):
    cols = a.shape[-1]
    rows = a.size // cols
    tm = CAST_TILE_BYTES // (4 * cols)
    spec = pl.BlockSpec((tm, cols), lambda i: (i, 0))
    out = pl.pallas_call(
        _cast_kernel,
        grid=(rows // tm,),
        in_specs=[spec],
        out_specs=spec,
        out_shape=jax.ShapeDtypeStruct((rows, cols), BF16),
        compiler_params=_params(("parallel",)),
        name="cast_bf16",
    )(a.reshape(rows, cols))
    return out.reshape(a.shape)


def _gate_bias_row(gate_b):
    ib, fb = gate_b[0], gate_b[1]
    order = jnp.concatenate([ib[0:2], fb[0:2], ib[2:4], fb[2:4]])
    return jnp.concatenate([order, jnp.zeros((LANES - 2 * N_HEADS,), F32)]).reshape(1, LANES)


def kernel(x, c, positions, w_ada, b_ada, norm_g, w_in, mlstm_conv_w, mlstm_conv_b, mlstm_gate_b,
           ret_norm_g, swa_q_norm_g, swa_k_norm_g, swa_sinks, rel_bias, w_up, w_out, w_ff1, w_ff2):
    batch, seq, d = x.shape
    depth = w_in.shape[0]
    t = batch * seq
    nb = seq // CHUNK

    c8 = jnp.concatenate([c, jnp.zeros((8 - batch, d), F32)], axis=0)
    mod = _ada(c8, w_ada.reshape(depth * 2, d, 3 * d), b_ada.reshape(depth * 2, 1, 3 * d))
    mod = mod[:, :batch].reshape(depth, 2, batch, 3, 1, d)

    pos_col = jnp.broadcast_to(positions.reshape(t, 1), (t, LANES))
    pos_row = positions.reshape(batch, nb, 1, CHUNK)
    half = HEAD_DIM // 2
    inv = ROPE_BASE ** (-(np.arange(LANES) % half).astype(np.float64) / half)
    inv_row = jnp.asarray(inv, F32).reshape(1, LANES)
    bias_tab = jnp.concatenate([rel_bias.T, jnp.zeros((N_HEADS, LANES - N_BUCKETS), F32)], axis=1)

    log_gamma = np.log(1.0 - np.exp2(-(RET_DECAY_BASE + np.arange(N_HEADS, dtype=np.float64))))
    lg_tab = jnp.asarray(np.broadcast_to(np.repeat(log_gamma, HEAD_DIM).reshape(2, 1, LANES), (2, 8, LANES)), F32)

    w_perm, w_gate = _prep_w_in(w_in)
    w_up_bf, w_out_bf, w_ff1_bf, w_ff2_bf = (_to_bf16(w) for w in (w_up, w_out, w_ff1, w_ff2))

    xt = x.reshape(t, d)
    for l in range(depth):
        gb_row = _gate_bias_row(mlstm_gate_b[l])
        g1 = norm_g[l, 0].reshape(1, d)
        g2 = norm_g[l, 1].reshape(1, d)
        shift1, scale1, gate1 = mod[l, 0, :, 0], mod[l, 0, :, 1], mod[l, 0, :, 2]
        shift2, scale2, gate2 = mod[l, 1, :, 0], mod[l, 1, :, 1], mod[l, 1, :, 2]

        ml, ifc, ret, sb, swa, ift = _in_proj(xt, g1, scale1, shift1, w_perm, gb_row, l, batch, seq)
        small = (mlstm_conv_w[l], mlstm_conv_b[l].reshape(1, 2 * W_BRANCH), lg_tab, ret_norm_g[l].reshape(1, W_BRANCH),
                 swa_sinks[l], jnp.tile(swa_q_norm_g[l], 2).reshape(1, LANES), jnp.tile(swa_k_norm_g[l], 2).reshape(1, LANES))
        xt = _layer(xt, ((shift1, scale1, gate1), (shift2, scale2, gate2)), g1, g2, ml, ifc, ift, ret, sb, swa,
                    (bias_tab, inv_row, pos_col, pos_row), small, (w_gate, w_up_bf, w_out_bf, w_ff1_bf, w_ff2_bf), l, batch, seq)
    return xt.reshape(batch, seq, d)
```

```python
import functools
import math

import numpy as np
import jax
import jax.numpy as jnp
from jax import lax
from jax.experimental import pallas as pl
from jax.experimental.pallas import tpu as pltpu

F32 = jnp.float32
BF16 = jnp.bfloat16

D_MODEL = 1024
HEAD_DIM = 64
N_HEADS = 4
W_BRANCH = N_HEADS * HEAD_DIM
LANES = 128
CHUNK = 128
CONV_K = 4
D_FF = 4 * D_MODEL
N_BUCKETS = 32
MAX_DIST = 128
ROPE_BASE = 10000.0
RET_DECAY_BASE = 5.0
EPS = 1e-6
NEG = -1e30
F32_EXP_UNDERFLOW = -104.0
VMEM_LIMIT = 56 * 1024 * 1024
LAYER_VMEM_LIMIT = 60 * 1024 * 1024

SEG_ML = (0, 1024)
SEG_IFC = (1024, 1152)
SEG_RET = (1152, 2176)
SEG_SB = (2176, 2944)
SEG_SWA = (2944, 3456)
N_IN = 3456


def _dot(a, b):
    return jnp.dot(a, b, preferred_element_type=F32)


def _dot_nt(a, b):
    return lax.dot_general(a, b, (((1,), (1,)), ((), ())), preferred_element_type=F32)


def _dot_tn(a, b):
    return lax.dot_general(a, b, (((0,), (0,)), ((), ())), preferred_element_type=F32)


def _split2(x):
    hi = x.astype(BF16)
    lo = (x - hi.astype(F32)).astype(BF16)
    return hi, lo


def _split3(x):
    hi = x.astype(BF16)
    r = x - hi.astype(F32)
    mid = r.astype(BF16)
    lo = (r - mid.astype(F32)).astype(BF16)
    return hi, mid, lo


def _iota(shape, axis):
    return lax.broadcasted_iota(jnp.int32, shape, axis)


def _log_sigmoid(x):
    return jnp.minimum(x, 0.0) - jnp.log(1.0 + jnp.exp(-jnp.abs(x)))


def _sigmoid(x):
    return 1.0 / (1.0 + jnp.exp(-x))


def _silu(x):
    return x * _sigmoid(x)


def _norm_mod(x, g, scale, shift):
    ms = jnp.mean(x * x, axis=-1, keepdims=True)
    y = x * lax.rsqrt(ms + EPS)
    return (y * g) * (1.0 + scale) + shift


def _head_mean_sq(x):
    lane = _iota(x.shape, 1)
    sq = x * x
    s0 = jnp.sum(jnp.where(lane < HEAD_DIM, sq, 0.0), axis=-1, keepdims=True)
    s1 = jnp.sum(jnp.where(lane >= HEAD_DIM, sq, 0.0), axis=-1, keepdims=True)
    return jnp.where(lane < HEAD_DIM, s0, s1) * (1.0 / HEAD_DIM)


def _params(sem):
    return pltpu.CompilerParams(dimension_semantics=sem, vmem_limit_bytes=VMEM_LIMIT)


def _interleave(chains, filler=None, every=1):
    def advance(gen):
        try:
            next(gen)
            return True
        except StopIteration:
            return False

    live = list(chains)
    filling = filler is not None
    steps = 0
    while live:
        still = []
        for ch in live:
            if advance(ch):
                still.append(ch)
            steps += 1
            if filling and steps % every == 0:
                filling = advance(filler)
        live = still
    while filling:
        filling = advance(filler)


def _resident(a):
    return pl.BlockSpec(a.shape, lambda *_: (0,) * a.ndim, pipeline_mode=pl.Buffered(1))


def _layer_of(a, layer):
    return pl.BlockSpec((None,) + a.shape[1:], lambda *_: (layer,) + (0,) * (a.ndim - 1),
                        pipeline_mode=pl.Buffered(1))


def _ada_kernel(c_ref, w_ref, b_ref, o_ref):
    c = c_ref[...]
    ch, cl = _split2(_silu(c))
    wh, wl = _split2(w_ref[0])
    o_ref[0] = _dot(ch, wh) + _dot(ch, wl) + _dot(cl, wh) + b_ref[0]


def _ada(c8, w_ada, b_ada):
    n_mod, d, n3 = w_ada.shape
    tn = 1024
    return pl.pallas_call(
        _ada_kernel,
        grid=(n_mod, n3 // tn),
        in_specs=[pl.BlockSpec((8, d), lambda m, n: (0, 0)),
                  pl.BlockSpec((1, d, tn), lambda m, n: (m, 0, n)),
                  pl.BlockSpec((1, 1, tn), lambda m, n: (m, 0, n))],
        out_specs=pl.BlockSpec((1, 8, tn), lambda m, n: (m, 0, n)),
        out_shape=jax.ShapeDtypeStruct((n_mod, 8, n3), F32),
        compiler_params=_params(("parallel", "parallel")),
        name="ada_mod",
    )(c8, w_ada, b_ada)


def _t5_bias(tab_ref, pos_q, pos_prev, pos_cur):
    max_exact = N_BUCKETS // 2
    rel = jnp.concatenate([pos_q - pos_prev, pos_q - pos_cur], axis=1)
    n = jnp.maximum(rel, 0)
    nf = jnp.maximum(n, 1).astype(F32)
    large = max_exact + (jnp.log(nf / max_exact) / math.log(MAX_DIST / max_exact)
                         * (N_BUCKETS - max_exact)).astype(jnp.int32)
    large = jnp.minimum(large, N_BUCKETS - 1)
    bucket = jnp.where(n < max_exact, n, large)
    halves = (bucket[:, :CHUNK], bucket[:, CHUNK:])
    tiles = []
    for h in range(N_HEADS):
        row = jnp.broadcast_to(tab_ref[h:h + 1, :], (CHUNK, LANES))
        tiles.append(jnp.concatenate([jnp.take_along_axis(row, idx, axis=1) for idx in halves], axis=1))
    return tiles


def _in_kernel(x_ref, g_ref, sc_ref, sh_ref, w_ref, gb_ref, ml_ref, ifc_ref, ret_ref, sb_ref, swa_ref, ift_ref):
    h = _norm_mod(x_ref[...], g_ref[...], sc_ref[0], sh_ref[0]).astype(BF16)

    def seg(s):
        return _dot(h, w_ref[:, s[0]:s[1]])

    ml_ref[...] = seg(SEG_ML)
    gates = seg(SEG_IFC) + gb_ref[...]
    ifc_ref[...] = gates
    ift_ref[0] = gates.T[0:16, :]
    ret_ref[...] = seg(SEG_RET)
    sb_ref[...] = seg(SEG_SB).astype(BF16)
    swa_ref[...] = seg(SEG_SWA)


def _in_proj(x, g, scale, shift, w_perm, gb_row, layer, batch, seq):
    t, d = x.shape
    tm = 1024
    tpb = seq // tm
    row = lambda n: pl.BlockSpec((tm, n), lambda i: (i, 0))
    mod = pl.BlockSpec((1, 1, d), lambda i: (i // tpb, 0, 0))
    widths = [s[1] - s[0] for s in (SEG_ML, SEG_IFC, SEG_RET, SEG_SB, SEG_SWA)]
    dts = [F32, F32, F32, BF16, F32]
    return pl.pallas_call(
        _in_kernel,
        grid=(t // tm,),
        in_specs=[row(d), _resident(g), mod, mod, _layer_of(w_perm, layer), _resident(gb_row)],
        out_specs=[row(n) for n in widths] + [pl.BlockSpec((1, 16, tm), lambda i: (i // tpb, 0, i % tpb))],
        out_shape=[jax.ShapeDtypeStruct((t, n), dt) for n, dt in zip(widths, dts)]
        + [jax.ShapeDtypeStruct((batch, 16, seq), F32)],
        compiler_params=_params(("parallel",)),
        name="in_proj",
    )(x, g, scale, shift, w_perm, gb_row)


def _mlstm_chains(ml_ref, ifc_ref, ift_ref, cw_ref, cb_ref, put, halo, c_st, m_st):
    L = CHUNK
    HALO = 8
    batch = ml_ref.shape[0]
    lane = _iota((L, LANES), 1)
    row = _iota((L, LANES), 0)
    causal = row >= lane
    tri = jnp.where(causal, 1.0, 0.0).astype(BF16)
    upper = jnp.where(row <= lane, 1.0, 0.0).astype(BF16)

    def conv_silu(b, col, slot):
        hl = halo.at[slot]
        hl[HALO:HALO + L, :] = ml_ref[b, :, col:col + LANES]
        acc = jnp.zeros((L, LANES), F32) + cb_ref[:, col:col + LANES]
        for j in range(CONV_K):
            off = HALO - (CONV_K - 1) + j
            acc = acc + hl[off:off + L, :] * cw_ref[j:j + 1, col:col + LANES]
        hl[0:HALO, :] = hl[L:L + HALO, :]
        return _silu(acc)

    def pair_chain(b, p):
        pc = p * LANES
        ift = ift_ref[b, 0:8, :]
        lr_h, lr_m, lr_l = _split3(_log_sigmoid(ift))
        a_rows = _dot(lr_h, upper) + _dot(lr_m, upper) + _dot(lr_l, upper)
        ifc = ifc_ref[b]
        gates_b = jnp.concatenate([jnp.broadcast_to(ifc[:, 4 * p + j:4 * p + j + 1], (L, LANES)) for j in range(4)],
                                  axis=1)
        q = conv_silu(b, pc, (b * 2 + p) * 2)
        k = conv_silu(b, W_BRANCH + pc, (b * 2 + p) * 2 + 1)
        v = ml_ref[b, :, 2 * W_BRANCH + pc:2 * W_BRANCH + pc + LANES]
        k_bf = k.astype(BF16)
        yield

        lf_h, lf_m, lf_l = _split3(_log_sigmoid(gates_b[:, 2 * LANES:]))
        a_b = _dot(tri, lf_h) + _dot(tri, lf_m) + _dot(tri, lf_l)
        heads = []
        for hh in range(2):
            ch = (b * 2 + p) * 2 + hh
            hm = (lane >= HEAD_DIM) if hh else (lane < HEAD_DIM)
            qh = (jnp.where(hm, q, 0.0) * (HEAD_DIM ** -0.5)).astype(BF16)
            c_in = c_st[ch]
            heads.append(dict(ch=ch, hm=hm, c_in=c_in, s_qk=_dot_nt(qh, k_bf), q_c=_dot(qh, c_in.astype(BF16))))
        yield

        for hh, hd in enumerate(heads):
            i_col = gates_b[:, hh * LANES:(hh + 1) * LANES]
            a_col = a_b[:, hh * LANES:(hh + 1) * LANES]
            a_last = a_col[L - 1:L, :]
            m_in = m_st[hd["ch"], 0:1, :]
            v_aug = jnp.where(hd["hm"], v, 1.0).astype(BF16)

            r0 = 4 * p + hh
            d_intra = jnp.where(causal, a_col - (a_rows[r0 + 2:r0 + 3, :] - ift[r0:r0 + 1, :]), NEG)
            m_intra = jnp.max(d_intra, axis=-1, keepdims=True)
            m_inter = a_col + m_in
            m_tot = jnp.maximum(m_inter, m_intra)
            pm = jnp.exp(d_intra - m_tot) * hd["s_qk"]
            hd["p_v"] = _dot(pm.astype(BF16), v_aug)
            hd["s_inter"] = jnp.exp(m_inter - m_tot)
            hd["floor"] = jnp.exp(-m_tot)

            g_col = a_last - a_col + i_col
            m_loc = jnp.max(g_col, axis=0, keepdims=True)
            kw = (jnp.where(hd["hm"], k, 0.0) * jnp.exp(g_col - m_loc)).astype(BF16)
            hd["c_loc"] = _dot_tn(kw, v_aug)
            m_new = jnp.maximum(a_last + m_in, m_loc)
            hd["w_old"] = jnp.exp(a_last + m_in - m_new)
            hd["w_new"] = jnp.exp(m_loc - m_new)
            m_st[hd["ch"]] = jnp.zeros((8, LANES), F32) + m_new
        yield

        for hd in heads:
            hd["o_aug"] = hd["s_inter"] * hd["q_c"] + hd["p_v"]
        den = pltpu.roll(jnp.where(lane < HEAD_DIM, heads[1]["o_aug"], heads[0]["o_aug"]), HEAD_DIM, 1)
        num = jnp.where(lane < HEAD_DIM, heads[0]["o_aug"], heads[1]["o_aug"])
        floor = jnp.where(lane < HEAD_DIM, heads[0]["floor"], heads[1]["floor"])
        h_ml = num / jnp.maximum(jnp.abs(den), floor)
        o_gate = ml_ref[b, :, 3 * W_BRANCH + pc:3 * W_BRANCH + pc + LANES]
        put(b, pc, h_ml * _sigmoid(o_gate))
        yield

        for hd in heads:
            c_st[hd["ch"]] = hd["w_old"] * hd["c_in"] + hd["w_new"] * hd["c_loc"]

    return [pair_chain(b, p) for b in range(batch) for p in range(2)]


def _retention_chains(ret_ref, posq_ref, inv_ref, lg_ref, ng_ref, put, st):
    L = CHUNK
    batch = ret_ref.shape[0]
    lane = _iota((L, LANES), 1)
    row = _iota((L, LANES), 0)
    rowf = row.astype(F32)
    relf = jnp.maximum(row - lane, 0).astype(F32)
    first = (lane % HEAD_DIM) < (HEAD_DIM // 2)
    same_head = (row // HEAD_DIM) == (lane // HEAD_DIM)

    def rot(t, cos, sin):
        partner = jnp.where(first, -pltpu.roll(t, LANES - HEAD_DIM // 2, 1), pltpu.roll(t, HEAD_DIM // 2, 1))
        return t * cos + partner * sin

    decays = []
    for p in range(2):
        lg = lg_ref[p, 0:1, :]
        decays.append(dict(
            q=jnp.exp((rowf + 1.0) * lg), k=jnp.exp((L - 1.0 - rowf) * lg), c=jnp.exp(float(L) * lg),
            intra=[jnp.where(row >= lane, jnp.exp(relf * lg[:, hh * HEAD_DIM:hh * HEAD_DIM + 1]), 0.0)
                   for hh in range(2)]))

    tables = {}

    def pair_chain(b, p):
        pc = p * LANES
        dec = decays[p]
        if p == 0:
            ang = posq_ref[b].astype(F32) * inv_ref[...]
            tables[b] = (jnp.cos(ang), jnp.sin(ang))
        cos, sin = tables[b]
        q = rot(ret_ref[b, :, pc:pc + LANES], cos, sin)
        k = rot(ret_ref[b, :, W_BRANCH + pc:W_BRANCH + pc + LANES], cos, sin) * (HEAD_DIM ** -0.5)
        v_bf = ret_ref[b, :, 2 * W_BRANCH + pc:2 * W_BRANCH + pc + LANES].astype(BF16)
        k_bf = k.astype(BF16)
        state = st[b * 2 + p]
        s_qk = [_dot_nt(jnp.where((lane >= HEAD_DIM) if hh else (lane < HEAD_DIM), q, 0.0).astype(BF16), k_bf)
                for hh in range(2)]
        inter = _dot(q.astype(BF16), state.astype(BF16))
        kv = _dot_tn((k * dec["k"]).astype(BF16), v_bf)
        yield

        intra = [_dot((s_qk[hh] * dec["intra"][hh]).astype(BF16), v_bf) for hh in range(2)]
        st[b * 2 + p] = dec["c"] * state + jnp.where(same_head, kv, 0.0)
        yield

        o = jnp.where(lane < HEAD_DIM, intra[0], intra[1]) + inter * dec["q"]
        y = o * lax.rsqrt(_head_mean_sq(o) + EPS) * ng_ref[:, pc:pc + LANES]
        gate = ret_ref[b, :, 3 * W_BRANCH + pc:3 * W_BRANCH + pc + LANES]
        put(b, pc, y * _silu(gate))

    return [pair_chain(b, p) for b in range(batch) for p in range(2)]


SB_FUSED_BLOCKS = 3


def _sb_suffix_op():
    L = CHUNK
    r2 = _iota((2 * L, 2 * L), 0) % L
    c2 = _iota((2 * L, 2 * L), 1)
    return jnp.where((c2 >= L) | (r2 >= c2), 1.0, 0.0).astype(BF16)


def _sb_tile_chain(c, b, p, tiles, fresh, qh_s, carry_s, acc_s, suffix_op):
    L = CHUNK
    pc = p * LANES
    zs = [_dot_nt(qh_s[c], k_ref[b, :, k_col + pc:k_col + pc + LANES]) for k_ref, k_col, _, _, _ in tiles]
    yield
    rrs = []
    for z, (_, _, _, _, mask) in zip(zs, tiles):
        lk = -(jnp.maximum(z, 0.0) + jnp.log(1.0 + jnp.exp(-jnp.abs(z))))
        if mask is not None:
            lk = jnp.where(mask, lk, 0.0)
        hi, lo = _split2(lk)
        rrs.append(_dot(jnp.concatenate([hi, lo], axis=1), suffix_op))
    yield
    carry = None if fresh else carry_s[c]
    parts = []
    for z, rr, (_, _, v_ref, v_col, mask) in zip(zs, rrs, tiles):
        e = z + rr[:, :L]
        w = jnp.exp(e if carry is None else e + carry)
        if mask is not None:
            w = jnp.where(mask, w, 0.0)
        parts.append(_dot(w.astype(BF16), v_ref[b, :, v_col + pc:v_col + pc + LANES]))
        carry = rr[:, L:] if carry is None else carry + rr[:, L:]
    carry_s[c] = carry
    yield
    acc = None if fresh else acc_s[c]
    for part in parts:
        acc = part if acc is None else acc + part
    acc_s[c] = acc


def _sb_chains(i, qkv_refs, qh_s, carry_s, acc_s):
    L = CHUNK
    q_ref = qkv_refs[0]
    batch = q_ref.shape[0]
    lane = _iota((L, LANES), 1)
    row = _iota((L, LANES), 0)
    heads = [(b, p, hh) for b in range(batch) for p in range(2) for hh in range(2)]
    for c, (b, p, hh) in enumerate(heads):
        hm = (lane >= HEAD_DIM) if hh else (lane < HEAD_DIM)
        qh_s[c] = jnp.where(hm, q_ref[b, :, p * LANES:(p + 1) * LANES] * (HEAD_DIM ** -0.5), 0.0).astype(BF16)
    block_no = jnp.zeros((L, LANES), jnp.int32) + i
    masks = [lane < row] + [block_no >= d for d in range(1, SB_FUSED_BLOCKS)]
    tiles = [(qkv_refs[d], W_BRANCH, qkv_refs[d], 2 * W_BRANCH, masks[d]) for d in range(SB_FUSED_BLOCKS)]
    suffix_op = _sb_suffix_op()
    return [_sb_tile_chain(c, b, p, tiles, True, qh_s, carry_s, acc_s, suffix_op) for c, (b, p, hh) in enumerate(heads)]


def _sb_older_blocks(i, sb_hbm, kbuf, vbuf, sem, qh_s, carry_s, acc_s):
    L = CHUNK
    batch = kbuf.shape[0]
    heads = [(b, p, hh) for b in range(batch) for p in range(2) for hh in range(2)]

    def all_underflow():
        return (jnp.max(jnp.max(carry_s[...], axis=0)) <= F32_EXP_UNDERFLOW).astype(jnp.int32)

    def fetch(j, col, buf, slot):
        return pltpu.make_async_copy(
            sb_hbm.at[:, pl.ds(pl.multiple_of(j * L, L), L), pl.ds(col * W_BRANCH, W_BRANCH)], buf, sem.at[slot])

    def cond(st):
        return jnp.logical_and(st[0] >= 0, st[1] == 0)

    def body(st):
        copies = [fetch(st[0], 1, kbuf, 0), fetch(st[0], 2, vbuf, 1)]
        for cp in copies:
            cp.start()
        for cp in copies:
            cp.wait()
        suffix_op = _sb_suffix_op()
        _interleave([_sb_tile_chain(c, b, p, [(kbuf, 0, vbuf, 0, None)], False, qh_s, carry_s, acc_s, suffix_op)
                     for c, (b, p, hh) in enumerate(heads)])
        return (st[0] - 1, all_underflow())

    lax.while_loop(cond, body, (i - SB_FUSED_BLOCKS, all_underflow()))


def _swa_chains(i, sink_ref, cur_ref, prev_ref, tab_ref, posq_ref, posc_ref, posp_ref, qg_ref, kg_ref, put):
    L = CHUNK
    batch = cur_ref.shape[0]
    lane = _iota((L, LANES), 1)
    t = _iota((L, 2 * L), 0)
    j = _iota((L, 2 * L), 1)
    dist = t + L - j
    block_no = jnp.zeros((L, 2 * L), jnp.int32) + i
    valid = (dist >= 0) & (dist < L) & ((j >= L) | (block_no > 0))

    def qk_norm(x, g_ref):
        return x * lax.rsqrt(_head_mean_sq(x) + EPS) * g_ref[...]

    bias = {}

    def both_lanes(pair, g):
        swapped = pltpu.roll(pair, HEAD_DIM, 1)
        return jnp.where(lane < HEAD_DIM, pair, swapped) if g == 0 else jnp.where(lane < HEAD_DIM, swapped, pair)

    def group_chain(b, g):
        sl = slice(g * LANES, (g + 1) * LANES)
        if g == 0:
            bias[b] = _t5_bias(tab_ref, posq_ref[b], posp_ref[b, 0], posc_ref[b, 0])
        qn = qk_norm(cur_ref[b, :, sl], qg_ref) * (HEAD_DIM ** -0.5)
        k_prev = both_lanes(qk_norm(prev_ref[b, :, 0:LANES], kg_ref), g)
        k_cur = both_lanes(qk_norm(cur_ref[b, :, W_BRANCH:W_BRANCH + LANES], kg_ref), g)
        kcat = jnp.concatenate([k_prev, k_cur], axis=0).astype(BF16)
        qk = [_dot_nt(jnp.where((lane >= HEAD_DIM) if r else (lane < HEAD_DIM), qn, 0.0).astype(BF16), kcat)
              for r in range(2)]
        yield
        vcat = jnp.concatenate([both_lanes(prev_ref[b, :, LANES:2 * LANES], g),
                                both_lanes(cur_ref[b, :, W_BRANCH + LANES:W_BRANCH + 2 * LANES], g)], axis=0).astype(BF16)
        outs = []
        for r in range(2):
            hq = 2 * g + r
            logits = jnp.where(valid, qk[r] + bias[b][hq], NEG)
            sink = sink_ref[hq]
            m = jnp.maximum(jnp.max(logits, axis=-1, keepdims=True), sink)
            pr = jnp.exp(logits - m)
            den = jnp.sum(pr, axis=-1, keepdims=True) + jnp.exp(sink - m)
            outs.append(_dot((pr / den).astype(BF16), vcat))
        yield
        put(b, g * LANES, jnp.where(lane < HEAD_DIM, outs[0], outs[1]))

    return [group_chain(b, g) for b in range(batch) for g in range(2)]


DENSE_COLS = 512


def _dense_chain(x_ref, g1_ref, mod1, g2_ref, mod2, branch, wg_ref, wu_ref, wo_ref, w1_ref, w2_ref, out_ref,
                 h_s, merged_s, x1_s, ff_s):
    L = CHUNK
    batch = x_ref.shape[0]
    rows = lambda b: slice(b * L, (b + 1) * L)
    sh1, sc1, gt1 = mod1
    sh2, sc2, gt2 = mod2
    for b in range(batch):
        h_s[rows(b), :] = _norm_mod(x_ref[b], g1_ref[...], sc1[b], sh1[b]).astype(BF16)
    yield
    n_col = D_MODEL // DENSE_COLS
    units = [(j, n) for j in range(n_col) for n in range(N_HEADS)]
    logits = lambda j, n: _dot(h_s[...], wg_ref[:, n * D_MODEL + j * DENSE_COLS:n * D_MODEL + (j + 1) * DENSE_COLS])
    pending = logits(*units[0])
    acc = None
    for u, (j, n) in enumerate(units):
        cs = slice(j * DENSE_COLS, (j + 1) * DENSE_COLS)
        gate_logits = pending
        if u + 1 < len(units):
            pending = logits(*units[u + 1])
        term = _sigmoid(gate_logits) * _dot(branch(n), wu_ref[n, :, cs])
        acc = term if n == 0 else acc + term
        if n == N_HEADS - 1:
            merged_s[:, cs] = acc.astype(BF16)
        yield
    proj = lambda j: _dot(merged_s[...], wo_ref[:, j * DENSE_COLS:(j + 1) * DENSE_COLS])
    pending = proj(0)
    for j in range(n_col):
        cs = slice(j * DENSE_COLS, (j + 1) * DENSE_COLS)
        mix = pending
        if j + 1 < n_col:
            pending = proj(j + 1)
        for b in range(batch):
            x1_s[rows(b), cs] = x_ref[b, :, cs] + gt1[b][:, cs] * mix[rows(b)]
        yield
    for b in range(batch):
        h_s[rows(b), :] = _norm_mod(x1_s[rows(b), :], g2_ref[...], sc2[b], sh2[b]).astype(BF16)
    yield
    n_ff = D_FF // DENSE_COLS
    up = lambda n: _dot(h_s[...], w1_ref[:, n * DENSE_COLS:(n + 1) * DENSE_COLS])
    pending = up(0)
    for n in range(n_ff):
        a = jnp.maximum(pending, 0.0)
        if n + 1 < n_ff:
            pending = up(n + 1)
        part = _dot((a * a).astype(BF16), w2_ref[n * DENSE_COLS:(n + 1) * DENSE_COLS, :])
        ff_s[...] = part if n == 0 else ff_s[...] + part
        yield
    for b in range(batch):
        out_ref[b] = x1_s[rows(b), :] + gt2[b] * ff_s[rows(b), :]


def _layer_kernel(sink_ref, ml_ref, ifc_ref, ift_ref, cw_ref, cb_ref,
                  ret_ref, inv_ref, lg_ref, ng_ref,
                  sb0_ref, sb1_ref, sb2_ref, sb_hbm,
                  swc_ref, swp_ref, tab_ref, posq_ref, posc_ref, posp_ref, qg_ref, kg_ref,
                  x_ref, g1_ref, sh1_ref, sc1_ref, gt1_ref, g2_ref, sh2_ref, sc2_ref, gt2_ref,
                  wg_ref, wu_ref, wo_ref, w1_ref, w2_ref,
                  out_ref,
                  halo, c_st, m_st, ret_st, qh_s, carry_s, acc_s, kbuf, vbuf, sem,
                  branch_s, h_s, merged_s, x1_s, ff_s, *, n_chunks):
    L = CHUNK
    batch = ml_ref.shape[0]
    step = pl.program_id(0)
    i = jnp.minimum(step, n_chunks - 1)
    slot = step % 2

    @pl.when(step == 0)
    def _init():
        halo[:, 0:8, :] = jnp.zeros((halo.shape[0], 8, LANES), F32)
        c_st[...] = jnp.zeros(c_st.shape, F32)
        m_st[...] = jnp.zeros(m_st.shape, F32)
        ret_st[...] = jnp.zeros(ret_st.shape, F32)
        branch_s[1] = jnp.zeros(branch_s.shape[1:], BF16)

    def put(n):
        def write(b, col, value):
            branch_s[slot, n, b * L:(b + 1) * L, col:col + LANES] = value.astype(BF16)
        return write

    def dense_chain():
        return _dense_chain(x_ref, g1_ref, (sh1_ref, sc1_ref, gt1_ref), g2_ref, (sh2_ref, sc2_ref, gt2_ref),
                            lambda n: branch_s[1 - slot, n], wg_ref, wu_ref, wo_ref, w1_ref, w2_ref, out_ref,
                            h_s, merged_s, x1_s, ff_s)

    @pl.when(step < n_chunks)
    def _mixers_and_dense():
        chains = (_mlstm_chains(ml_ref, ifc_ref, ift_ref, cw_ref, cb_ref, put(0), halo, c_st, m_st)
                  + _retention_chains(ret_ref, posq_ref, inv_ref, lg_ref, ng_ref, put(1), ret_st)
                  + _sb_chains(i, (sb0_ref, sb1_ref, sb2_ref), qh_s, carry_s, acc_s)
                  + _swa_chains(i, sink_ref, swc_ref, swp_ref, tab_ref, posq_ref, posc_ref, posp_ref, qg_ref, kg_ref,
                                put(3)))
        _interleave(chains, filler=dense_chain())

        _sb_older_blocks(i, sb_hbm, kbuf, vbuf, sem, qh_s, carry_s, acc_s)
        lane = _iota((L, LANES), 1)
        for b in range(batch):
            for p in range(2):
                c0 = (b * 2 + p) * 2
                put(2)(b, p * LANES, jnp.where(lane < HEAD_DIM, acc_s[c0], acc_s[c0 + 1]))

    @pl.when(step == n_chunks)
    def _last_dense_only():
        _interleave([], filler=dense_chain())


def _layer(xt, mods, g1, g2, ml, ifc, ift, ret, sb, swa, pos, small, weights, layer, batch, seq):
    d = xt.shape[1]
    nc = seq // CHUNK
    n_chain = batch * N_HEADS
    conv_w, conv_b, lg_tab, ret_g, sinks, q_g, k_g = small
    bias_tab, inv_row, pos_col, pos_row = pos
    w_gate, w_up, w_out, w_ff1, w_ff2 = weights
    mix = lambda c: jnp.minimum(c, nc - 1)
    lag = lambda c: jnp.maximum(c - 1, 0)
    chunk = lambda n, blk=0, back=0: pl.BlockSpec(
        (batch, CHUNK, n), lambda c: (0, jnp.maximum(mix(c) - back, 0), blk))
    sb3 = sb.reshape(batch, seq, -1)
    swa3 = swa.reshape(batch, seq, -1)
    x3 = xt.reshape(batch, seq, d)
    rowblk = pl.BlockSpec((batch, CHUNK, d), lambda c: (0, lag(c), 0))
    in_specs = (
        [pl.BlockSpec(memory_space=pltpu.SMEM),
         chunk(4 * W_BRANCH), chunk(LANES), pl.BlockSpec((batch, 16, CHUNK), lambda c: (0, 0, mix(c))),
         _resident(conv_w), _resident(conv_b),
         chunk(4 * W_BRANCH), _resident(inv_row), _resident(lg_tab), _resident(ret_g),
         ]
        + [chunk(3 * W_BRANCH, 0, back) for back in range(SB_FUSED_BLOCKS)]
        + [pl.BlockSpec(memory_space=pl.ANY),
           chunk(2 * W_BRANCH), chunk(W_BRANCH, 1, 1),
           _resident(bias_tab), chunk(LANES),
           pl.BlockSpec((batch, 1, 1, CHUNK), lambda c: (0, mix(c), 0, 0)),
           pl.BlockSpec((batch, 1, 1, CHUNK), lambda c: (0, jnp.maximum(mix(c) - 1, 0), 0, 0)),
           _resident(q_g), _resident(k_g),
           rowblk, _resident(g1)] + [_resident(m) for m in mods[0]] + [_resident(g2)] + [_resident(m) for m in mods[1]]
        + [_layer_of(w, layer) for w in (w_gate, w_up, w_out, w_ff1, w_ff2)])
    rows = batch * CHUNK
    out = pl.pallas_call(
        functools.partial(_layer_kernel, n_chunks=nc),
        grid=(nc + 1,),
        in_specs=in_specs,
        out_specs=rowblk,
        out_shape=jax.ShapeDtypeStruct((batch, seq, d), F32),
        scratch_shapes=[pltpu.VMEM((n_chain, CHUNK + 8, LANES), F32), pltpu.VMEM((n_chain, LANES, LANES), F32),
                        pltpu.VMEM((n_chain, 8, LANES), F32), pltpu.VMEM((batch * 2, LANES, LANES), F32),
                        pltpu.VMEM((n_chain, CHUNK, LANES), BF16), pltpu.VMEM((n_chain, CHUNK, LANES), F32),
                        pltpu.VMEM((n_chain, CHUNK, LANES), F32),
                        pltpu.VMEM((batch, CHUNK, W_BRANCH), BF16), pltpu.VMEM((batch, CHUNK, W_BRANCH), BF16),
                        pltpu.SemaphoreType.DMA((2,)),
                        pltpu.VMEM((2, N_HEADS, rows, W_BRANCH), BF16), pltpu.VMEM((rows, d), BF16),
                        pltpu.VMEM((rows, d), BF16), pltpu.VMEM((rows, d), F32), pltpu.VMEM((rows, d), F32)],
        compiler_params=pltpu.CompilerParams(dimension_semantics=("arbitrary",), vmem_limit_bytes=LAYER_VMEM_LIMIT),
        name="layer",
    )(sinks, ml.reshape(batch, seq, -1), ifc.reshape(batch, seq, -1), ift, conv_w, conv_b,
      ret.reshape(batch, seq, -1), inv_row, lg_tab, ret_g,
      sb3, sb3, sb3, sb3,
      swa3, swa3, bias_tab, pos_col.reshape(batch, seq, LANES), pos_row, pos_row, q_g, k_g,
      x3, g1, *mods[0], g2, *mods[1], w_gate, w_up, w_out, w_ff1, w_ff2)
    return out.reshape(batch * seq, d)


GATE_COL = 3 * W_BRANCH


def _w_in_kernel(wt_ref, perm_ref, gate_ref):
    one_hot = lambda hit: jnp.where(hit, 1.0, 0.0).astype(BF16)
    eye = one_hot(_iota((LANES, LANES), 0) == _iota((LANES, LANES), 1))
    transpose = lambda rows_bf: _dot_nt(eye, rows_bf).astype(BF16)
    cols = lambda lo, n: transpose(wt_ref[lo:lo + n, :].astype(BF16))
    o_out = GATE_COL + 2 * N_HEADS
    o_ret = o_out + W_BRANCH
    o_sb = o_ret + 4 * W_BRANCH
    o_swa = o_sb + 3 * W_BRANCH
    o_gate = o_swa + 2 * W_BRANCH

    perm_ref[:, 0:GATE_COL] = cols(0, GATE_COL)
    perm_ref[:, GATE_COL:SEG_ML[1]] = cols(o_out, W_BRANCH)
    perm_ref[:, SEG_RET[0]:SEG_RET[1]] = cols(o_ret, 4 * W_BRANCH)
    perm_ref[:, SEG_SB[0]:SEG_SB[1]] = cols(o_sb, 3 * W_BRANCH)
    perm_ref[:, SEG_SWA[0]:SEG_SWA[1]] = cols(o_swa, 2 * W_BRANCH)
    gate_ref[...] = cols(o_gate, gate_ref.shape[1])

    block = wt_ref[GATE_COL:GATE_COL + LANES, :].astype(BF16)
    dst = _iota((LANES, LANES), 0)
    j = dst % 4
    src = jnp.where(j < 2, j, j + 2) + 2 * (dst // 4)
    placement = one_hot((_iota((LANES, LANES), 1) == src) & (dst < 2 * N_HEADS))
    perm_ref[:, SEG_IFC[0]:SEG_IFC[1]] = transpose(_dot(placement, block).astype(BF16))


def _prep_w_in(w_in):
    depth, d, n_cols = w_in.shape
    tk = LANES
    return pl.pallas_call(
        _w_in_kernel,
        grid=(depth, d // tk),
        in_specs=[pl.BlockSpec((None, n_cols, tk), lambda l, i: (l, 0, i))],
        out_specs=[pl.BlockSpec((None, tk, N_IN), lambda l, i: (l, i, 0)),
                   pl.BlockSpec((None, tk, N_HEADS * d), lambda l, i: (l, i, 0))],
        out_shape=[jax.ShapeDtypeStruct((depth, d, N_IN), BF16), jax.ShapeDtypeStruct((depth, d, N_HEADS * d), BF16)],
        compiler_params=_params(("parallel", "parallel")),
        name="prep_w_in",
    )(jnp.swapaxes(w_in, 1, 2))


CAST_TILE_BYTES = 2 * 1024 * 1024


def _cast_kernel(x_ref, o_ref):
    o_ref[...] = x_ref[...].astype(BF16)


def _to_bf16(a):
    cols = a.shape[-1]
    rows = a.size // cols
    tm = CAST_TILE_BYTES // (4 * cols)
    spec = pl.BlockSpec((tm, cols), lambda i: (i, 0))
    out = pl.pallas_call(
        _cast_kernel,
        grid=(rows // tm,),
        in_specs=[spec],
        out_specs=spec,
        out_shape=jax.ShapeDtypeStruct((rows, cols), BF16),
        compiler_params=_params(("parallel",)),
        name="cast_bf16",
    )(a.reshape(rows, cols))
    return out.reshape(a.shape)


def _gate_bias_row(gate_b):
    ib, fb = gate_b[0], gate_b[1]
    order = jnp.concatenate([ib[0:2], fb[0:2], ib[2:4], fb[2:4]])
    return jnp.concatenate([order, jnp.zeros((LANES - 2 * N_HEADS,), F32)]).reshape(1, LANES)


def kernel(x, c, positions, w_ada, b_ada, norm_g, w_in, mlstm_conv_w, mlstm_conv_b, mlstm_gate_b,
           ret_norm_g, swa_q_norm_g, swa_k_norm_g, swa_sinks, rel_bias, w_up, w_out, w_ff1, w_ff2):
    batch, seq, d = x.shape
    depth = w_in.shape[0]
    t = batch * seq
    nb = seq // CHUNK

    c8 = jnp.concatenate([c, jnp.zeros((8 - batch, d), F32)], axis=0)
    mod = _ada(c8, w_ada.reshape(depth * 2, d, 3 * d), b_ada.reshape(depth * 2, 1, 3 * d))
    mod = mod[:, :batch].reshape(depth, 2, batch, 3, 1, d)

    pos_col = jnp.broadcast_to(positions.reshape(t, 1), (t, LANES))
    pos_row = positions.reshape(batch, nb, 1, CHUNK)
    half = HEAD_DIM // 2
    inv = ROPE_BASE ** (-(np.arange(LANES) % half).astype(np.float64) / half)
    inv_row = jnp.asarray(inv, F32).reshape(1, LANES)
    bias_tab = jnp.concatenate([rel_bias.T, jnp.zeros((N_HEADS, LANES - N_BUCKETS), F32)], axis=1)

    log_gamma = np.log(1.0 - np.exp2(-(RET_DECAY_BASE + np.arange(N_HEADS, dtype=np.float64))))
    lg_tab = jnp.asarray(np.broadcast_to(np.repeat(log_gamma, HEAD_DIM).reshape(2, 1, LANES), (2, 8, LANES)), F32)

    w_perm, w_gate = _prep_w_in(w_in)
    w_up_bf, w_out_bf, w_ff1_bf, w_ff2_bf = (_to_bf16(w) for w in (w_up, w_out, w_ff1, w_ff2))

    xt = x.reshape(t, d)
    for l in range(depth):
        gb_row = _gate_bias_row(mlstm_gate_b[l])
        g1 = norm_g[l, 0].reshape(1, d)
        g2 = norm_g[l, 1].reshape(1, d)
        shift1, scale1, gate1 = mod[l, 0, :, 0], mod[l, 0, :, 1], mod[l, 0, :, 2]
        shift2, scale2, gate2 = mod[l, 1, :, 0], mod[l, 1, :, 1], mod[l, 1, :, 2]

        ml, ifc, ret, sb, swa, ift = _in_proj(xt, g1, scale1, shift1, w_perm, gb_row, l, batch, seq)
        small = (mlstm_conv_w[l], mlstm_conv_b[l].reshape(1, 2 * W_BRANCH), lg_tab, ret_norm_g[l].reshape(1, W_BRANCH),
                 swa_sinks[l], jnp.tile(swa_q_norm_g[l], 2).reshape(1, LANES), jnp.tile(swa_k_norm_g[l], 2).reshape(1, LANES))
        xt = _layer(xt, ((shift1, scale1, gate1), (shift2, scale2, gate2)), g1, g2, ml, ifc, ift, ret, sb, swa,
                    (bias_tab, inv_row, pos_col, pos_row), small, (w_gate, w_up_bf, w_out_bf, w_ff1_bf, w_ff2_bf), l, batch, seq)
    return xt.reshape(batch, seq, d)
```

```python
import functools
import math

import numpy as np
import jax
import jax.numpy as jnp
from jax import lax
from jax.experimental import pallas as pl
from jax.experimental.pallas import tpu as pltpu

F32 = jnp.float32
BF16 = jnp.bfloat16

D_MODEL = 1024
HEAD_DIM = 64
N_HEADS = 4
W_BRANCH = N_HEADS * HEAD_DIM
LANES = 128
CHUNK = 128
CONV_K = 4
D_FF = 4 * D_MODEL
N_BUCKETS = 32
MAX_DIST = 128
ROPE_BASE = 10000.0
RET_DECAY_BASE = 5.0
EPS = 1e-6
NEG = -1e30
F32_EXP_UNDERFLOW = -104.0
VMEM_LIMIT = 56 * 1024 * 1024
LAYER_VMEM_LIMIT = 60 * 1024 * 1024

SEG_ML = (0, 1024)
SEG_IFC = (1024, 1152)
SEG_RET = (1152, 2176)
SEG_SB = (2176, 2944)
SEG_SWA = (2944, 3456)
N_IN = 3456


def _dot(a, b):
    return jnp.dot(a, b, preferred_element_type=F32)


def _dot_nt(a, b):
    return lax.dot_general(a, b, (((1,), (1,)), ((), ())), preferred_element_type=F32)


def _dot_tn(a, b):
    return lax.dot_general(a, b, (((0,), (0,)), ((), ())), preferred_element_type=F32)


def _split2(x):
    hi = x.astype(BF16)
    lo = (x - hi.astype(F32)).astype(BF16)
    return hi, lo


def _split3(x):
    hi = x.astype(BF16)
    r = x - hi.astype(F32)
    mid = r.astype(BF16)
    lo = (r - mid.astype(F32)).astype(BF16)
    return hi, mid, lo


def _iota(shape, axis):
    return lax.broadcasted_iota(jnp.int32, shape, axis)


def _log_sigmoid(x):
    return jnp.minimum(x, 0.0) - jnp.log(1.0 + jnp.exp(-jnp.abs(x)))


def _sigmoid(x):
    return 1.0 / (1.0 + jnp.exp(-x))


def _silu(x):
    return x * _sigmoid(x)


def _norm_mod(x, g, scale, shift):
    ms = jnp.mean(x * x, axis=-1, keepdims=True)
    y = x * lax.rsqrt(ms + EPS)
    return (y * g) * (1.0 + scale) + shift


def _head_mean_sq(x):
    lane = _iota(x.shape, 1)
    sq = x * x
    s0 = jnp.sum(jnp.where(lane < HEAD_DIM, sq, 0.0), axis=-1, keepdims=True)
    s1 = jnp.sum(jnp.where(lane >= HEAD_DIM, sq, 0.0), axis=-1, keepdims=True)
    return jnp.where(lane < HEAD_DIM, s0, s1) * (1.0 / HEAD_DIM)


def _params(sem):
    return pltpu.CompilerParams(dimension_semantics=sem, vmem_limit_bytes=VMEM_LIMIT)


def _interleave(chains, filler=None, every=1):
    def advance(gen):
        try:
            next(gen)
            return True
        except StopIteration:
            return False

    live = list(chains)
    filling = filler is not None
    steps = 0
    while live:
        still = []
        for ch in live:
            if advance(ch):
                still.append(ch)
            steps += 1
            if filling and steps % every == 0:
                filling = advance(filler)
        live = still
    while filling:
        filling = advance(filler)


def _resident(a):
    return pl.BlockSpec(a.shape, lambda *_: (0,) * a.ndim, pipeline_mode=pl.Buffered(1))


def _layer_of(a, layer):
    return pl.BlockSpec((None,) + a.shape[1:], lambda *_: (layer,) + (0,) * (a.ndim - 1),
                        pipeline_mode=pl.Buffered(1))


def _ada_kernel(c_ref, w_ref, b_ref, o_ref):
    c = c_ref[...]
    ch, cl = _split2(_silu(c))
    wh, wl = _split2(w_ref[0])
    o_ref[0] = _dot(ch, wh) + _dot(ch, wl) + _dot(cl, wh) + b_ref[0]


def _ada(c8, w_ada, b_ada):
    n_mod, d, n3 = w_ada.shape
    tn = 1024
    return pl.pallas_call(
        _ada_kernel,
        grid=(n_mod, n3 // tn),
        in_specs=[pl.BlockSpec((8, d), lambda m, n: (0, 0)),
                  pl.BlockSpec((1, d, tn), lambda m, n: (m, 0, n)),
                  pl.BlockSpec((1, 1, tn), lambda m, n: (m, 0, n))],
        out_specs=pl.BlockSpec((1, 8, tn), lambda m, n: (m, 0, n)),
        out_shape=jax.ShapeDtypeStruct((n_mod, 8, n3), F32),
        compiler_params=_params(("parallel", "parallel")),
        name="ada_mod",
    )(c8, w_ada, b_ada)


def _t5_bias(tab_ref, pos_q, pos_prev, pos_cur):
    max_exact = N_BUCKETS // 2
    rel = jnp.concatenate([pos_q - pos_prev, pos_q - pos_cur], axis=1)
    n = jnp.maximum(rel, 0)
    nf = jnp.maximum(n, 1).astype(F32)
    large = max_exact + (jnp.log(nf / max_exact) / math.log(MAX_DIST / max_exact)
                         * (N_BUCKETS - max_exact)).astype(jnp.int32)
    large = jnp.minimum(large, N_BUCKETS - 1)
    bucket = jnp.where(n < max_exact, n, large)
    halves = (bucket[:, :CHUNK], bucket[:, CHUNK:])
    tiles = []
    for h in range(N_HEADS):
        row = jnp.broadcast_to(tab_ref[h:h + 1, :], (CHUNK, LANES))
        tiles.append(jnp.concatenate([jnp.take_along_axis(row, idx, axis=1) for idx in halves], axis=1))
    return tiles


def _in_kernel(x_ref, g_ref, sc_ref, sh_ref, w_ref, gb_ref, ml_ref, ifc_ref, ret_ref, sb_ref, swa_ref, ift_ref):
    h = _norm_mod(x_ref[...], g_ref[...], sc_ref[0], sh_ref[0]).astype(BF16)

    def seg(s):
        return _dot(h, w_ref[:, s[0]:s[1]])

    ml_ref[...] = seg(SEG_ML)
    gates = seg(SEG_IFC) + gb_ref[...]
    ifc_ref[...] = gates
    ift_ref[0] = gates.T[0:16, :]
    ret_ref[...] = seg(SEG_RET)
    sb_ref[...] = seg(SEG_SB).astype(BF16)
    swa_ref[...] = seg(SEG_SWA)


def _in_proj(x, g, scale, shift, w_perm, gb_row, layer, batch, seq):
    t, d = x.shape
    tm = 1024
    tpb = seq // tm
    row = lambda n: pl.BlockSpec((tm, n), lambda i: (i, 0))
    mod = pl.BlockSpec((1, 1, d), lambda i: (i // tpb, 0, 0))
    widths = [s[1] - s[0] for s in (SEG_ML, SEG_IFC, SEG_RET, SEG_SB, SEG_SWA)]
    dts = [F32, F32, F32, BF16, F32]
    return pl.pallas_call(
        _in_kernel,
        grid=(t // tm,),
        in_specs=[row(d), _resident(g), mod, mod, _layer_of(w_perm, layer), _resident(gb_row)],
        out_specs=[row(n) for n in widths] + [pl.BlockSpec((1, 16, tm), lambda i: (i // tpb, 0, i % tpb))],
        out_shape=[jax.ShapeDtypeStruct((t, n), dt) for n, dt in zip(widths, dts)]
        + [jax.ShapeDtypeStruct((batch, 16, seq), F32)],
        compiler_params=_params(("parallel",)),
        name="in_proj",
    )(x, g, scale, shift, w_perm, gb_row)


def _mlstm_chains(ml_ref, ifc_ref, ift_ref, cw_ref, cb_ref, put, halo, c_st, m_st):
    L = CHUNK
    HALO = 8
    batch = ml_ref.shape[0]
    lane = _iota((L, LANES), 1)
    row = _iota((L, LANES), 0)
    causal = row >= lane
    tri = jnp.where(causal, 1.0, 0.0).astype(BF16)
    upper = jnp.where(row <= lane, 1.0, 0.0).astype(BF16)

    def conv_silu(b, col, slot):
        hl = halo.at[slot]
        hl[HALO:HALO + L, :] = ml_ref[b, :, col:col + LANES]
        acc = jnp.zeros((L, LANES), F32) + cb_ref[:, col:col + LANES]
        for j in range(CONV_K):
            off = HALO - (CONV_K - 1) + j
            acc = acc + hl[off:off + L, :] * cw_ref[j:j + 1, col:col + LANES]
        hl[0:HALO, :] = hl[L:L + HALO, :]
        return _silu(acc)

    def pair_chain(b, p):
        pc = p * LANES
        ift = ift_ref[b, 0:8, :]
        lr_h, lr_m, lr_l = _split3(_log_sigmoid(ift))
        a_rows = _dot(lr_h, upper) + _dot(lr_m, upper) + _dot(lr_l, upper)
        ifc = ifc_ref[b]
        gates_b = jnp.concatenate([jnp.broadcast_to(ifc[:, 4 * p + j:4 * p + j + 1], (L, LANES)) for j in range(4)],
                                  axis=1)
        q = conv_silu(b, pc, (b * 2 + p) * 2)
        k = conv_silu(b, W_BRANCH + pc, (b * 2 + p) * 2 + 1)
        v = ml_ref[b, :, 2 * W_BRANCH + pc:2 * W_BRANCH + pc + LANES]
        k_bf = k.astype(BF16)
        yield

        lf_h, lf_m, lf_l = _split3(_log_sigmoid(gates_b[:, 2 * LANES:]))
        a_b = _dot(tri, lf_h) + _dot(tri, lf_m) + _dot(tri, lf_l)
        heads = []
        for hh in range(2):
            ch = (b * 2 + p) * 2 + hh
            hm = (lane >= HEAD_DIM) if hh else (lane < HEAD_DIM)
            qh = (jnp.where(hm, q, 0.0) * (HEAD_DIM ** -0.5)).astype(BF16)
            c_in = c_st[ch]
            heads.append(dict(ch=ch, hm=hm, c_in=c_in, s_qk=_dot_nt(qh, k_bf), q_c=_dot(qh, c_in.astype(BF16))))
        yield

        for hh, hd in enumerate(heads):
            i_col = gates_b[:, hh * LANES:(hh + 1) * LANES]
            a_col = a_b[:, hh * LANES:(hh + 1) * LANES]
            a_last = a_col[L - 1:L, :]
            m_in = m_st[hd["ch"], 0:1, :]
            v_aug = jnp.where(hd["hm"], v, 1.0).astype(BF16)

            r0 = 4 * p + hh
            d_intra = jnp.where(causal, a_col - (a_rows[r0 + 2:r0 + 3, :] - ift[r0:r0 + 1, :]), NEG)
            m_intra = jnp.max(d_intra, axis=-1, keepdims=True)
            m_inter = a_col + m_in
            m_tot = jnp.maximum(m_inter, m_intra)
            pm = jnp.exp(d_intra - m_tot) * hd["s_qk"]
            hd["p_v"] = _dot(pm.astype(BF16), v_aug)
            hd["s_inter"] = jnp.exp(m_inter - m_tot)
            hd["floor"] = jnp.exp(-m_tot)

            g_col = a_last - a_col + i_col
            m_loc = jnp.max(g_col, axis=0, keepdims=True)
            kw = (jnp.where(hd["hm"], k, 0.0) * jnp.exp(g_col - m_loc)).astype(BF16)
            hd["c_loc"] = _dot_tn(kw, v_aug)
            m_new = jnp.maximum(a_last + m_in, m_loc)
            hd["w_old"] = jnp.exp(a_last + m_in - m_new)
            hd["w_new"] = jnp.exp(m_loc - m_new)
            m_st[hd["ch"]] = jnp.zeros((8, LANES), F32) + m_new
        yield

        for hd in heads:
            hd["o_aug"] = hd["s_inter"] * hd["q_c"] + hd["p_v"]
        den = pltpu.roll(jnp.where(lane < HEAD_DIM, heads[1]["o_aug"], heads[0]["o_aug"]), HEAD_DIM, 1)
        num = jnp.where(lane < HEAD_DIM, heads[0]["o_aug"], heads[1]["o_aug"])
        floor = jnp.where(lane < HEAD_DIM, heads[0]["floor"], heads[1]["floor"])
        h_ml = num / jnp.maximum(jnp.abs(den), floor)
        o_gate = ml_ref[b, :, 3 * W_BRANCH + pc:3 * W_BRANCH + pc + LANES]
        put(b, pc, h_ml * _sigmoid(o_gate))
        yield

        for hd in heads:
            c_st[hd["ch"]] = hd["w_old"] * hd["c_in"] + hd["w_new"] * hd["c_loc"]

    return [pair_chain(b, p) for b in range(batch) for p in range(2)]


def _retention_chains(ret_ref, posq_ref, inv_ref, lg_ref, ng_ref, put, st):
    L = CHUNK
    batch = ret_ref.shape[0]
    lane = _iota((L, LANES), 1)
    row = _iota((L, LANES), 0)
    rowf = row.astype(F32)
    relf = jnp.maximum(row - lane, 0).astype(F32)
    first = (lane % HEAD_DIM) < (HEAD_DIM // 2)
    same_head = (row // HEAD_DIM) == (lane // HEAD_DIM)

    def rot(t, cos, sin):
        partner = jnp.where(first, -pltpu.roll(t, LANES - HEAD_DIM // 2, 1), pltpu.roll(t, HEAD_DIM // 2, 1))
        return t * cos + partner * sin

    decays = []
    for p in range(2):
        lg = lg_ref[p, 0:1, :]
        decays.append(dict(
            q=jnp.exp((rowf + 1.0) * lg), k=jnp.exp((L - 1.0 - rowf) * lg), c=jnp.exp(float(L) * lg),
            intra=[jnp.where(row >= lane, jnp.exp(relf * lg[:, hh * HEAD_DIM:hh * HEAD_DIM + 1]), 0.0)
                   for hh in range(2)]))

    tables = {}

    def pair_chain(b, p):
        pc = p * LANES
        dec = decays[p]
        if p == 0:
            ang = posq_ref[b].astype(F32) * inv_ref[...]
            tables[b] = (jnp.cos(ang), jnp.sin(ang))
        cos, sin = tables[b]
        q = rot(ret_ref[b, :, pc:pc + LANES], cos, sin)
        k = rot(ret_ref[b, :, W_BRANCH + pc:W_BRANCH + pc + LANES], cos, sin) * (HEAD_DIM ** -0.5)
        v_bf = ret_ref[b, :, 2 * W_BRANCH + pc:2 * W_BRANCH + pc + LANES].astype(BF16)
        k_bf = k.astype(BF16)
        state = st[b * 2 + p]
        s_qk = [_dot_nt(jnp.where((lane >= HEAD_DIM) if hh else (lane < HEAD_DIM), q, 0.0).astype(BF16), k_bf)
                for hh in range(2)]
        inter = _dot(q.astype(BF16), state.astype(BF16))
        kv = _dot_tn((k * dec["k"]).astype(BF16), v_bf)
        yield

        intra = [_dot((s_qk[hh] * dec["intra"][hh]).astype(BF16), v_bf) for hh in range(2)]
        st[b * 2 + p] = dec["c"] * state + jnp.where(same_head, kv, 0.0)
        yield

        o = jnp.where(lane < HEAD_DIM, intra[0], intra[1]) + inter * dec["q"]
        y = o * lax.rsqrt(_head_mean_sq(o) + EPS) * ng_ref[:, pc:pc + LANES]
        gate = ret_ref[b, :, 3 * W_BRANCH + pc:3 * W_BRANCH + pc + LANES]
        put(b, pc, y * _silu(gate))

    return [pair_chain(b, p) for b in range(batch) for p in range(2)]


SB_FUSED_BLOCKS = 3


def _sb_suffix_op():
    L = CHUNK
    r2 = _iota((2 * L, 2 * L), 0) % L
    c2 = _iota((2 * L, 2 * L), 1)
    return jnp.where((c2 >= L) | (r2 >= c2), 1.0, 0.0).astype(BF16)


def _sb_tile_chain(c, b, p, tiles, fresh, qh_s, carry_s, acc_s, suffix_op):
    L = CHUNK
    pc = p * LANES
    zs = [_dot_nt(qh_s[c], k_ref[b, :, k_col + pc:k_col + pc + LANES]) for k_ref, k_col, _, _, _ in tiles]
    yield
    rrs = []
    for z, (_, _, _, _, mask) in zip(zs, tiles):
        lk = -(jnp.maximum(z, 0.0) + jnp.log(1.0 + jnp.exp(-jnp.abs(z))))
        if mask is not None:
            lk = jnp.where(mask, lk, 0.0)
        hi, lo = _split2(lk)
        rrs.append(_dot(jnp.concatenate([hi, lo], axis=1), suffix_op))
    yield
    carry = None if fresh else carry_s[c]
    parts = []
    for z, rr, (_, _, v_ref, v_col, mask) in zip(zs, rrs, tiles):
        e = z + rr[:, :L]
        w = jnp.exp(e if carry is None else e + carry)
        if mask is not None:
            w = jnp.where(mask, w, 0.0)
        parts.append(_dot(w.astype(BF16), v_ref[b, :, v_col + pc:v_col + pc + LANES]))
        carry = rr[:, L:] if carry is None else carry + rr[:, L:]
    carry_s[c] = carry
    yield
    acc = None if fresh else acc_s[c]
    for part in parts:
        acc = part if acc is None else acc + part
    acc_s[c] = acc


def _sb_chains(i, qkv_refs, qh_s, carry_s, acc_s):
    L = CHUNK
    q_ref = qkv_refs[0]
    batch = q_ref.shape[0]
    lane = _iota((L, LANES), 1)
    row = _iota((L, LANES), 0)
    heads = [(b, p, hh) for b in range(batch) for p in range(2) for hh in range(2)]
    for c, (b, p, hh) in enumerate(heads):
        hm = (lane >= HEAD_DIM) if hh else (lane < HEAD_DIM)
        qh_s[c] = jnp.where(hm, q_ref[b, :, p * LANES:(p + 1) * LANES] * (HEAD_DIM ** -0.5), 0.0).astype(BF16)
    block_no = jnp.zeros((L, LANES), jnp.int32) + i
    masks = [lane < row] + [block_no >= d for d in range(1, SB_FUSED_BLOCKS)]
    tiles = [(qkv_refs[d], W_BRANCH, qkv_refs[d], 2 * W_BRANCH, masks[d]) for d in range(SB_FUSED_BLOCKS)]
    suffix_op = _sb_suffix_op()
    return [_sb_tile_chain(c, b, p, tiles, True, qh_s, carry_s, acc_s, suffix_op) for c, (b, p, hh) in enumerate(heads)]


def _sb_older_blocks(i, sb_hbm, kbuf, vbuf, sem, qh_s, carry_s, acc_s):
    L = CHUNK
    batch = kbuf.shape[0]
    heads = [(b, p, hh) for b in range(batch) for p in range(2) for hh in range(2)]

    def all_underflow():
        return (jnp.max(jnp.max(carry_s[...], axis=0)) <= F32_EXP_UNDERFLOW).astype(jnp.int32)

    def fetch(j, col, buf, slot):
        return pltpu.make_async_copy(
            sb_hbm.at[:, pl.ds(pl.multiple_of(j * L, L), L), pl.ds(col * W_BRANCH, W_BRANCH)], buf, sem.at[slot])

    def cond(st):
        return jnp.logical_and(st[0] >= 0, st[1] == 0)

    def body(st):
        copies = [fetch(st[0], 1, kbuf, 0), fetch(st[0], 2, vbuf, 1)]
        for cp in copies:
            cp.start()
        for cp in copies:
            cp.wait()
        suffix_op = _sb_suffix_op()
        _interleave([_sb_tile_chain(c, b, p, [(kbuf, 0, vbuf, 0, None)], False, qh_s, carry_s, acc_s, suffix_op)
                     for c, (b, p, hh) in enumerate(heads)])
        return (st[0] - 1, all_underflow())

    lax.while_loop(cond, body, (i - SB_FUSED_BLOCKS, all_underflow()))


def _swa_chains(i, sink_ref, cur_ref, prev_ref, tab_ref, posq_ref, posc_ref, posp_ref, qg_ref, kg_ref, put):
    L = CHUNK
    batch = cur_ref.shape[0]
    lane = _iota((L, LANES), 1)
    t = _iota((L, 2 * L), 0)
    j = _iota((L, 2 * L), 1)
    dist = t + L - j
    block_no = jnp.zeros((L, 2 * L), jnp.int32) + i
    valid = (dist >= 0) & (dist < L) & ((j >= L) | (block_no > 0))

    def qk_norm(x, g_ref):
        return x * lax.rsqrt(_head_mean_sq(x) + EPS) * g_ref[...]

    bias = {}

    def both_lanes(pair, g):
        swapped = pltpu.roll(pair, HEAD_DIM, 1)
        return jnp.where(lane < HEAD_DIM, pair, swapped) if g == 0 else jnp.where(lane < HEAD_DIM, swapped, pair)

    def group_chain(b, g):
        sl = slice(g * LANES, (g + 1) * LANES)
        if g == 0:
            bias[b] = _t5_bias(tab_ref, posq_ref[b], posp_ref[b, 0], posc_ref[b, 0])
        qn = qk_norm(cur_ref[b, :, sl], qg_ref) * (HEAD_DIM ** -0.5)
        k_prev = both_lanes(qk_norm(prev_ref[b, :, 0:LANES], kg_ref), g)
        k_cur = both_lanes(qk_norm(cur_ref[b, :, W_BRANCH:W_BRANCH + LANES], kg_ref), g)
        kcat = jnp.concatenate([k_prev, k_cur], axis=0).astype(BF16)
        qk = [_dot_nt(jnp.where((lane >= HEAD_DIM) if r else (lane < HEAD_DIM), qn, 0.0).astype(BF16), kcat)
              for r in range(2)]
        yield
        vcat = jnp.concatenate([both_lanes(prev_ref[b, :, LANES:2 * LANES], g),
                                both_lanes(cur_ref[b, :, W_BRANCH + LANES:W_BRANCH + 2 * LANES], g)], axis=0).astype(BF16)
        outs = []
        for r in range(2):
            hq = 2 * g + r
            logits = jnp.where(valid, qk[r] + bias[b][hq], NEG)
            sink = sink_ref[hq]
            m = jnp.maximum(jnp.max(logits, axis=-1, keepdims=True), sink)
            pr = jnp.exp(logits - m)
            den = jnp.sum(pr, axis=-1, keepdims=True) + jnp.exp(sink - m)
            outs.append(_dot((pr / den).astype(BF16), vcat))
        yield
        put(b, g * LANES, jnp.where(lane < HEAD_DIM, outs[0], outs[1]))

    return [group_chain(b, g) for b in range(batch) for g in range(2)]


DENSE_COLS = 512


def _dense_chain(x_ref, g1_ref, mod1, g2_ref, mod2, branch, wg_ref, wu_ref, wo_ref, w1_ref, w2_ref, out_ref,
                 h_s, merged_s, x1_s, ff_s):
    L = CHUNK
    batch = x_ref.shape[0]
    rows = lambda b: slice(b * L, (b + 1) * L)
    sh1, sc1, gt1 = mod1
    sh2, sc2, gt2 = mod2
    for b in range(batch):
        h_s[rows(b), :] = _norm_mod(x_ref[b], g1_ref[...], sc1[b], sh1[b]).astype(BF16)
    yield
    n_col = D_MODEL // DENSE_COLS
    units = [(j, n) for j in range(n_col) for n in range(N_HEADS)]
    logits = lambda j, n: _dot(h_s[...], wg_ref[:, n * D_MODEL + j * DENSE_COLS:n * D_MODEL + (j + 1) * DENSE_COLS])
    pending = logits(*units[0])
    acc = None
    for u, (j, n) in enumerate(units):
        cs = slice(j * DENSE_COLS, (j + 1) * DENSE_COLS)
        gate_logits = pending
        if u + 1 < len(units):
            pending = logits(*units[u + 1])
        term = _sigmoid(gate_logits) * _dot(branch(n), wu_ref[n, :, cs])
        acc = term if n == 0 else acc + term
        if n == N_HEADS - 1:
            merged_s[:, cs] = acc.astype(BF16)
        yield
    proj = lambda j: _dot(merged_s[...], wo_ref[:, j * DENSE_COLS:(j + 1) * DENSE_COLS])
    pending = proj(0)
    for j in range(n_col):
        cs = slice(j * DENSE_COLS, (j + 1) * DENSE_COLS)
        mix = pending
        if j + 1 < n_col:
            pending = proj(j + 1)
        for b in range(batch):
            x1_s[rows(b), cs] = x_ref[b, :, cs] + gt1[b][:, cs] * mix[rows(b)]
        yield
    for b in range(batch):
        h_s[rows(b), :] = _norm_mod(x1_s[rows(b), :], g2_ref[...], sc2[b], sh2[b]).astype(BF16)
    yield
    n_ff = D_FF // DENSE_COLS
    up = lambda n: _dot(h_s[...], w1_ref[:, n * DENSE_COLS:(n + 1) * DENSE_COLS])
    pending = up(0)
    for n in range(n_ff):
        a = jnp.maximum(pending, 0.0)
        if n + 1 < n_ff:
            pending = up(n + 1)
        part = _dot((a * a).astype(BF16), w2_ref[n * DENSE_COLS:(n + 1) * DENSE_COLS, :])
        ff_s[...] = part if n == 0 else ff_s[...] + part
        yield
    for b in range(batch):
        out_ref[b] = x1_s[rows(b), :] + gt2[b] * ff_s[rows(b), :]


def _layer_kernel(sink_ref, ml_ref, ifc_ref, ift_ref, cw_ref, cb_ref,
                  ret_ref, inv_ref, lg_ref, ng_ref,
                  sb0_ref, sb1_ref, sb2_ref, sb_hbm,
                  swc_ref, swp_ref, tab_ref, posq_ref, posc_ref, posp_ref, qg_ref, kg_ref,
                  x_ref, g1_ref, sh1_ref, sc1_ref, gt1_ref, g2_ref, sh2_ref, sc2_ref, gt2_ref,
                  wg_ref, wu_ref, wo_ref, w1_ref, w2_ref,
                  out_ref,
                  halo, c_st, m_st, ret_st, qh_s, carry_s, acc_s, kbuf, vbuf, sem,
                  branch_s, h_s, merged_s, x1_s, ff_s, *, n_chunks):
    L = CHUNK
    batch = ml_ref.shape[0]
    step = pl.program_id(0)
    i = jnp.minimum(step, n_chunks - 1)
    slot = step % 2

    @pl.when(step == 0)
    def _init():
        halo[:, 0:8, :] = jnp.zeros((halo.shape[0], 8, LANES), F32)
        c_st[...] = jnp.zeros(c_st.shape, F32)
        m_st[...] = jnp.zeros(m_st.shape, F32)
        ret_st[...] = jnp.zeros(ret_st.shape, F32)
        branch_s[1] = jnp.zeros(branch_s.shape[1:], BF16)

    def put(n):
        def write(b, col, value):
            branch_s[slot, n, b * L:(b + 1) * L, col:col + LANES] = value.astype(BF16)
        return write

    def dense_chain():
        return _dense_chain(x_ref, g1_ref, (sh1_ref, sc1_ref, gt1_ref), g2_ref, (sh2_ref, sc2_ref, gt2_ref),
                            lambda n: branch_s[1 - slot, n], wg_ref, wu_ref, wo_ref, w1_ref, w2_ref, out_ref,
                            h_s, merged_s, x1_s, ff_s)

    def mixers(filler):
        chains = (_mlstm_chains(ml_ref, ifc_ref, ift_ref, cw_ref, cb_ref, put(0), halo, c_st, m_st)
                  + _retention_chains(ret_ref, posq_ref, inv_ref, lg_ref, ng_ref, put(1), ret_st)
                  + _sb_chains(i, (sb0_ref, sb1_ref, sb2_ref), qh_s, carry_s, acc_s)
                  + _swa_chains(i, sink_ref, swc_ref, swp_ref, tab_ref, posq_ref, posc_ref, posp_ref, qg_ref, kg_ref,
                                put(3)))
        _interleave(chains, filler=filler)

        _sb_older_blocks(i, sb_hbm, kbuf, vbuf, sem, qh_s, carry_s, acc_s)
        lane = _iota((L, LANES), 1)
        for b in range(batch):
            for p in range(2):
                c0 = (b * 2 + p) * 2
                put(2)(b, p * LANES, jnp.where(lane < HEAD_DIM, acc_s[c0], acc_s[c0 + 1]))

    @pl.when(step == 0)
    def _first_mixers_only():
        mixers(None)

    @pl.when(jnp.logical_and(step > 0, step < n_chunks))
    def _mixers_and_dense():
        mixers(dense_chain())

    @pl.when(step == n_chunks)
    def _last_dense_only():
        _interleave([], filler=dense_chain())


def _layer(xt, mods, g1, g2, ml, ifc, ift, ret, sb, swa, pos, small, weights, layer, batch, seq):
    d = xt.shape[1]
    nc = seq // CHUNK
    n_chain = batch * N_HEADS
    conv_w, conv_b, lg_tab, ret_g, sinks, q_g, k_g = small
    bias_tab, inv_row, pos_col, pos_row = pos
    w_gate, w_up, w_out, w_ff1, w_ff2 = weights
    mix = lambda c: jnp.minimum(c, nc - 1)
    lag = lambda c: jnp.maximum(c - 1, 0)
    chunk = lambda n, blk=0, back=0: pl.BlockSpec(
        (batch, CHUNK, n), lambda c: (0, jnp.maximum(mix(c) - back, 0), blk))
    sb3 = sb.reshape(batch, seq, -1)
    swa3 = swa.reshape(batch, seq, -1)
    x3 = xt.reshape(batch, seq, d)
    rowblk = pl.BlockSpec((batch, CHUNK, d), lambda c: (0, lag(c), 0))
    in_specs = (
        [pl.BlockSpec(memory_space=pltpu.SMEM),
         chunk(4 * W_BRANCH), chunk(LANES), pl.BlockSpec((batch, 16, CHUNK), lambda c: (0, 0, mix(c))),
         _resident(conv_w), _resident(conv_b),
         chunk(4 * W_BRANCH), _resident(inv_row), _resident(lg_tab), _resident(ret_g),
         ]
        + [chunk(3 * W_BRANCH, 0, back) for back in range(SB_FUSED_BLOCKS)]
        + [pl.BlockSpec(memory_space=pl.ANY),
           chunk(2 * W_BRANCH), chunk(W_BRANCH, 1, 1),
           _resident(bias_tab), chunk(LANES),
           pl.BlockSpec((batch, 1, 1, CHUNK), lambda c: (0, mix(c), 0, 0)),
           pl.BlockSpec((batch, 1, 1, CHUNK), lambda c: (0, jnp.maximum(mix(c) - 1, 0), 0, 0)),
           _resident(q_g), _resident(k_g),
           rowblk, _resident(g1)] + [_resident(m) for m in mods[0]] + [_resident(g2)] + [_resident(m) for m in mods[1]]
        + [_layer_of(w, layer) for w in (w_gate, w_up, w_out, w_ff1, w_ff2)])
    rows = batch * CHUNK
    out = pl.pallas_call(
        functools.partial(_layer_kernel, n_chunks=nc),
        grid=(nc + 1,),
        in_specs=in_specs,
        out_specs=rowblk,
        out_shape=jax.ShapeDtypeStruct((batch, seq, d), F32),
        scratch_shapes=[pltpu.VMEM((n_chain, CHUNK + 8, LANES), F32), pltpu.VMEM((n_chain, LANES, LANES), F32),
                        pltpu.VMEM((n_chain, 8, LANES), F32), pltpu.VMEM((batch * 2, LANES, LANES), F32),
                        pltpu.VMEM((n_chain, CHUNK, LANES), BF16), pltpu.VMEM((n_chain, CHUNK, LANES), F32),
                        pltpu.VMEM((n_chain, CHUNK, LANES), F32),
                        pltpu.VMEM((batch, CHUNK, W_BRANCH), BF16), pltpu.VMEM((batch, CHUNK, W_BRANCH), BF16),
                        pltpu.SemaphoreType.DMA((2,)),
                        pltpu.VMEM((2, N_HEADS, rows, W_BRANCH), BF16), pltpu.VMEM((rows, d), BF16),
                        pltpu.VMEM((rows, d), BF16), pltpu.VMEM((rows, d), F32), pltpu.VMEM((rows, d), F32)],
        compiler_params=pltpu.CompilerParams(dimension_semantics=("arbitrary",), vmem_limit_bytes=LAYER_VMEM_LIMIT),
        name="layer",
    )(sinks, ml.reshape(batch, seq, -1), ifc.reshape(batch, seq, -1), ift, conv_w, conv_b,
      ret.reshape(batch, seq, -1), inv_row, lg_tab, ret_g,
      sb3, sb3, sb3, sb3,
      swa3, swa3, bias_tab, pos_col.reshape(batch, seq, LANES), pos_row, pos_row, q_g, k_g,
      x3, g1, *mods[0], g2, *mods[1], w_gate, w_up, w_out, w_ff1, w_ff2)
    return out.reshape(batch * seq, d)


GATE_COL = 3 * W_BRANCH


def _w_in_kernel(wt_ref, perm_ref, gate_ref):
    one_hot = lambda hit: jnp.where(hit, 1.0, 0.0).astype(BF16)
    eye = one_hot(_iota((LANES, LANES), 0) == _iota((LANES, LANES), 1))
    transpose = lambda rows_bf: _dot_nt(eye, rows_bf).astype(BF16)
    cols = lambda lo, n: transpose(wt_ref[lo:lo + n, :].astype(BF16))
    o_out = GATE_COL + 2 * N_HEADS
    o_ret = o_out + W_BRANCH
    o_sb = o_ret + 4 * W_BRANCH
    o_swa = o_sb + 3 * W_BRANCH
    o_gate = o_swa + 2 * W_BRANCH

    perm_ref[:, 0:GATE_COL] = cols(0, GATE_COL)
    perm_ref[:, GATE_COL:SEG_ML[1]] = cols(o_out, W_BRANCH)
    perm_ref[:, SEG_RET[0]:SEG_RET[1]] = cols(o_ret, 4 * W_BRANCH)
    perm_ref[:, SEG_SB[0]:SEG_SB[1]] = cols(o_sb, 3 * W_BRANCH)
    perm_ref[:, SEG_SWA[0]:SEG_SWA[1]] = cols(o_swa, 2 * W_BRANCH)
    gate_ref[...] = cols(o_gate, gate_ref.shape[1])

    block = wt_ref[GATE_COL:GATE_COL + LANES, :].astype(BF16)
    dst = _iota((LANES, LANES), 0)
    j = dst % 4
    src = jnp.where(j < 2, j, j + 2) + 2 * (dst // 4)
    placement = one_hot((_iota((LANES, LANES), 1) == src) & (dst < 2 * N_HEADS))
    perm_ref[:, SEG_IFC[0]:SEG_IFC[1]] = transpose(_dot(placement, block).astype(BF16))


def _prep_w_in(w_in):
    depth, d, n_cols = w_in.shape
    tk = LANES
    return pl.pallas_call(
        _w_in_kernel,
        grid=(depth, d // tk),
        in_specs=[pl.BlockSpec((None, n_cols, tk), lambda l, i: (l, 0, i))],
        out_specs=[pl.BlockSpec((None, tk, N_IN), lambda l, i: (l, i, 0)),
                   pl.BlockSpec((None, tk, N_HEADS * d), lambda l, i: (l, i, 0))],
        out_shape=[jax.ShapeDtypeStruct((depth, d, N_IN), BF16), jax.ShapeDtypeStruct((depth, d, N_HEADS * d), BF16)],
        compiler_params=_params(("parallel", "parallel")),
        name="prep_w_in",
    )(jnp.swapaxes(w_in, 1, 2))


CAST_TILE_BYTES = 2 * 1024 * 1024


def _cast_kernel(x_ref, o_ref):
    o_ref[...] = x_ref[...].astype(BF16)


def _to_bf16(a):
    cols = a.shape[-1]
    rows = a.size // cols
    tm = CAST_TILE_BYTES // (4 * cols)
    spec = pl.BlockSpec((tm, cols), lambda i: (i, 0))
    out = pl.pallas_call(
        _cast_kernel,
        grid=(rows // tm,),
        in_specs=[spec],
        out_specs=spec,
        out_shape=jax.ShapeDtypeStruct((rows, cols), BF16),
        compiler_params=_params(("parallel",)),
        name="cast_bf16",
    )(a.reshape(rows, cols))
    return out.reshape(a.shape)


def _gate_bias_row(gate_b):
    ib, fb = gate_b[0], gate_b[1]
    order = jnp.concatenate([ib[0:2], fb[0:2], ib[2:4], fb[2:4]])
    return jnp.concatenate([order, jnp.zeros((LANES - 2 * N_HEADS,), F32)]).reshape(1, LANES)


def kernel(x, c, positions, w_ada, b_ada, norm_g, w_in, mlstm_conv_w, mlstm_conv_b, mlstm_gate_b,
           ret_norm_g, swa_q_norm_g, swa_k_norm_g, swa_sinks, rel_bias, w_up, w_out, w_ff1, w_ff2):
    batch, seq, d = x.shape
    depth = w_in.shape[0]
    t = batch * seq
    nb = seq // CHUNK

    c8 = jnp.concatenate([c, jnp.zeros((8 - batch, d), F32)], axis=0)
    mod = _ada(c8, w_ada.reshape(depth * 2, d, 3 * d), b_ada.reshape(depth * 2, 1, 3 * d))
    mod = mod[:, :batch].reshape(depth, 2, batch, 3, 1, d)

    pos_col = jnp.broadcast_to(positions.reshape(t, 1), (t, LANES))
    pos_row = positions.reshape(batch, nb, 1, CHUNK)
    half = HEAD_DIM // 2
    inv = ROPE_BASE ** (-(np.arange(LANES) % half).astype(np.float64) / half)
    inv_row = jnp.asarray(inv, F32).reshape(1, LANES)
    bias_tab = jnp.concatenate([rel_bias.T, jnp.zeros((N_HEADS, LANES - N_BUCKETS), F32)], axis=1)

    log_gamma = np.log(1.0 - np.exp2(-(RET_DECAY_BASE + np.arange(N_HEADS, dtype=np.float64))))
    lg_tab = jnp.asarray(np.broadcast_to(np.repeat(log_gamma, HEAD_DIM).reshape(2, 1, LANES), (2, 8, LANES)), F32)

    w_perm, w_gate = _prep_w_in(w_in)
    w_up_bf, w_out_bf, w_ff1_bf, w_ff2_bf = (_to_bf16(w) for w in (w_up, w_out, w_ff1, w_ff2))

    xt = x.reshape(t, d)
    for l in range(depth):
        gb_row = _gate_bias_row(mlstm_gate_b[l])
        g1 = norm_g[l, 0].reshape(1, d)
        g2 = norm_g[l, 1].reshape(1, d)
        shift1, scale1, gate1 = mod[l, 0, :, 0], mod[l, 0, :, 1], mod[l, 0, :, 2]
        shift2, scale2, gate2 = mod[l, 1, :, 0], mod[l, 1, :, 1], mod[l, 1, :, 2]

        ml, ifc, ret, sb, swa, ift = _in_proj(xt, g1, scale1, shift1, w_perm, gb_row, l, batch, seq)
        small = (mlstm_conv_w[l], mlstm_conv_b[l].reshape(1, 2 * W_BRANCH), lg_tab, ret_norm_g[l].reshape(1, W_BRANCH),
                 swa_sinks[l], jnp.tile(swa_q_norm_g[l], 2).reshape(1, LANES), jnp.tile(swa_k_norm_g[l], 2).reshape(1, LANES))
        xt = _layer(xt, ((shift1, scale1, gate1), (shift2, scale2, gate2)), g1, g2, ml, ifc, ift, ret, sb, swa,
                    (bias_tab, inv_row, pos_col, pos_row), small, (w_gate, w_up_bf, w_out_bf, w_ff1_bf, w_ff2_bf), l, batch, seq)
    return xt.reshape(batch, seq, d)
```
